```python
import math
import jax, jax.numpy as jnp
from jax import lax
import numpy as np

D_MODEL = 2048
BATCH = 32
SEQ = 256
DEPTH = 4
DEC_BATCH = 2
DEC_SEQ = 4096
PAST_LEN = 256

GRID_W = 64
Q_BLOCK = 128
RMS_EPS = 1e-6
ROPE_THETA = 10000.0
H_A = 4
HD_A = 128
H_B = 8
Q_LORA = 512
KV_LORA = 256
NOPE_B = 128
ROPE_B = 64
V_B = 128
H_C = 8
HD_C = 128
NA_KR_MAX = 8
NA_KC = 16
N_BRANCH = 3
BR_W = 1024
D_FF = 5632
N_EXPERTS = 8
TOP_K = 2
D_FF_E = 2816
N_DENSE = (DEPTH + 1) // 2
N_MOE = DEPTH // 2
IN_SPLITS = (H_A * 2 * HD_A, H_A * 2 * HD_A, H_A * 2 * HD_A,
             Q_LORA, KV_LORA, ROPE_B,
             H_C * HD_C, H_C * HD_C, H_C * HD_C,
             N_BRANCH * D_MODEL)
D_IN = sum(IN_SPLITS)

kernel_name = 'hybrid_diffusion_prefix_step'


def _rms_norm(x, g):
    x32 = x.astype(jnp.float32)
    y = x32 * lax.rsqrt(jnp.mean(x32 * x32, axis=-1, keepdims=True) + RMS_EPS)
    return (y * g.astype(jnp.float32)).astype(x.dtype)


def _split_cols(z):
    parts, start = [], 0
    for size in IN_SPLITS:
        parts.append(z[..., start:start + size])
        start += size
    return parts


def _axial_angles(n_tok, dim):
    half = dim // 2
    t = jnp.arange(n_tok)
    row = (t // GRID_W).astype(jnp.float32)
    col = (t % GRID_W).astype(jnp.float32)
    inv = ROPE_THETA ** (-jnp.arange(0, half, 2, dtype=jnp.float32) / half)
    ar = row[:, None] * inv[None, :]
    ac = col[:, None] * inv[None, :]
    return jnp.concatenate([ar, ar, ac, ac], axis=-1)


def _apply_axial_rope(x, ang):
    d = x.shape[-1]
    h, qd = d // 2, d // 4
    shape = (1, x.shape[1]) + (1,) * (x.ndim - 3) + (d,)
    cos = jnp.cos(ang).reshape(shape).astype(x.dtype)
    sin = jnp.sin(ang).reshape(shape).astype(x.dtype)
    xr, xc = x[..., :h], x[..., h:]
    rot = jnp.concatenate([-xr[..., qd:], xr[..., :qd], -xc[..., qd:], xc[..., :qd]], axis=-1)
    return x * cos + rot * sin


def _map_query_blocks(fn, q):
    b, nq = q.shape[0], q.shape[1]
    nb = nq // Q_BLOCK
    qb = jnp.swapaxes(q.reshape((b, nb, Q_BLOCK) + q.shape[2:]), 0, 1)
    out = jnp.swapaxes(lax.map(fn, qb), 0, 1)
    return out.reshape((b, nq) + out.shape[3:])


def _softmax_attention(q, k, v):
    scale = q.shape[-1] ** -0.5
    def block(qb):
        s = jnp.einsum('bqhd,bkhd->bhqk', qb, k).astype(jnp.float32) * scale
        p = jax.nn.softmax(s, axis=-1).astype(v.dtype)
        return jnp.einsum('bhqk,bkhd->bqhd', p, v)
    return _map_query_blocks(block, q)


def _diff_attention(q, k, v, lam):
    scale = q.shape[-1] ** -0.5
    def block(qb):
        s = jnp.einsum('bqhtd,bkhtd->bthqk', qb, k).astype(jnp.float32) * scale
        p = jax.nn.softmax(s, axis=-1)
        p = (p[:, 0] - lam * p[:, 1]).astype(v.dtype)
        return jnp.einsum('bhqk,bkhd->bqhd', p, v)
    return _map_query_blocks(block, q)


def _neighbourhood_attention(q, k, v, k_ctx, v_ctx, bias_table):
    b, n, h, d = q.shape
    rows = n // GRID_W
    kr = min(NA_KR_MAX, rows)
    kc = NA_KC
    scale = d ** -0.5
    kg = k.reshape(b, rows, GRID_W, h, d)
    vg = v.reshape(b, rows, GRID_W, h, d)
    row_start = jnp.clip(jnp.arange(rows) - kr // 2, 0, rows - kr)
    cols = jnp.arange(GRID_W)
    col_idx = jnp.clip(cols - kc // 2, 0, GRID_W - kc)[:, None] + jnp.arange(kc)[None, :]
    dc = col_idx - cols[:, None] + (kc - 1)
    table = bias_table.astype(jnp.float32)

    def one_row(args):
        qr, r, rs = args
        k_rows = lax.dynamic_slice_in_dim(kg, rs, kr, axis=1)
        v_rows = lax.dynamic_slice_in_dim(vg, rs, kr, axis=1)
        k_win = jnp.take(k_rows, col_idx, axis=2)
        v_win = jnp.take(v_rows, col_idx, axis=2)
        dr = rs + jnp.arange(kr) - r + (NA_KR_MAX - 1)
        bias = table[:, dr[None, :, None], dc[:, None, :]]
        s_loc = jnp.einsum('bwhd,biwjhd->bhwij', qr, k_win).astype(jnp.float32) * scale + bias[None]
        s_ctx = jnp.einsum('bwhd,blhd->bhwl', qr, k_ctx).astype(jnp.float32) * scale
        s = jnp.concatenate([s_loc.reshape(b, h, GRID_W, kr * kc), s_ctx], axis=-1)
        p = jax.nn.softmax(s, axis=-1).astype(v.dtype)
        p_loc = p[..., :kr * kc].reshape(b, h, GRID_W, kr, kc)
        p_ctx = p[..., kr * kc:]
        return (jnp.einsum('bhwij,biwjhd->bwhd', p_loc, v_win)
                + jnp.einsum('bhwl,blhd->bwhd', p_ctx, v_ctx))

    qrows = jnp.moveaxis(q.reshape(b, rows, GRID_W, h, d), 1, 0)
    out = lax.map(one_row, (qrows, jnp.arange(rows), row_start))
    return jnp.moveaxis(out, 0, 1).reshape(b, n, h, d)


def _token_mixer(h, cache, li, P):
    b, n, _ = h.shape
    z = jnp.einsum('bnd,de->bne', h, P['w_in'][li])
    qa, ka, va, cq, ckv, krope, qc, kc, vc, gates = _split_cols(z)
    qa = qa.reshape(b, n, H_A, 2, HD_A)
    ka = ka.reshape(b, n, H_A, 2, HD_A)
    va = va.reshape(b, n, H_A, 2 * HD_A)
    cq = _rms_norm(cq, P['mla_q_norm_g'][li])
    qb = jnp.einsum('bnr,re->bne', cq, P['mla_w_uq'][li]).reshape(b, n, H_B, NOPE_B + ROPE_B)
    ckv = _rms_norm(ckv, P['mla_kv_norm_g'][li])
    qc = qc.reshape(b, n, H_C, HD_C)
    kc = kc.reshape(b, n, H_C, HD_C)
    vc = vc.reshape(b, n, H_C, HD_C)
    if cache is not None:
        c_dk, c_dv, c_ckv, c_krope, c_nk, c_nv = cache
        ang_a = _axial_angles(n, HD_A)
        ang_b = _axial_angles(n, ROPE_B)
        qa = _apply_axial_rope(qa, ang_a)
        ka = _apply_axial_rope(ka, ang_a)
        qb = jnp.concatenate([qb[..., :NOPE_B], _apply_axial_rope(qb[..., NOPE_B:], ang_b)], axis=-1)
        ka_all = jnp.concatenate([ka, c_dk.reshape(b, -1, H_A, 2, HD_A)], axis=1)
        va_all = jnp.concatenate([va, c_dv], axis=1)
        ckv_all = jnp.concatenate([ckv, c_ckv], axis=1)
        krope_all = jnp.concatenate([_apply_axial_rope(krope, ang_b), c_krope], axis=1)
        new_cache = None
    else:
        ka_all, va_all, ckv_all, krope_all = ka, va, ckv, krope
        new_cache = (ka.reshape(b, n, H_A, 2 * HD_A), va, ckv, krope, kc, vc)
    lam_p = P['diff_lambda'][li].astype(jnp.float32)
    lam_init = 0.8 - 0.6 * math.exp(-0.3 * li)
    lam = jnp.exp(jnp.sum(lam_p[0] * lam_p[1])) - jnp.exp(jnp.sum(lam_p[2] * lam_p[3])) + lam_init
    oa = _diff_attention(qa, ka_all, va_all, lam)
    oa = _rms_norm(oa, P['diff_subln_g'][li]) * (1.0 - lam_init)
    kv = jnp.einsum('bkr,re->bke', ckv_all, P['mla_w_ukv'][li]).reshape(b, -1, H_B, NOPE_B + V_B)
    k_rope_h = jnp.broadcast_to(krope_all[:, :, None, :], kv.shape[:3] + (ROPE_B,))
    kb = jnp.concatenate([kv[..., :NOPE_B], k_rope_h], axis=-1)
    ob = _softmax_attention(qb, kb, kv[..., NOPE_B:])
    if cache is not None:
        oc = _neighbourhood_attention(qc, kc, vc, c_nk, c_nv, P['na_rel_bias'][li])
    else:
        oc = _softmax_attention(qc, kc, vc)
    branches = jnp.stack([oa.reshape(b, n, BR_W), ob.reshape(b, n, BR_W), oc.reshape(b, n, BR_W)], axis=2)
    yb = jnp.einsum('bnsk,skd->bnsd', branches, P['w_branch'][li])
    g = jax.nn.sigmoid(gates.reshape(b, n, N_BRANCH, D_MODEL))
    merged = jnp.sum(g * yb, axis=2)
    return jnp.einsum('bnd,de->bne', merged, P['w_out'][li]), new_cache


def _swiglu(h, w1, w3, w2):
    return jnp.einsum('bnf,fd->bnd', jax.nn.silu(jnp.einsum('bnd,df->bnf', h, w1)) * jnp.einsum('bnd,df->bnf', h, w3), w2)


def _moe(h, router, w1, w3, w2):
    logits = jnp.einsum('bnd,de->bne', h, router).astype(jnp.float32)
    top_v, top_i = lax.top_k(logits, TOP_K)
    top_w = jax.nn.softmax(top_v, axis=-1)
    out = jnp.zeros_like(h)
    for e in range(N_EXPERTS):
        w_e = jnp.sum(jnp.where(top_i == e, top_w, 0.0), axis=-1)
        out = out + w_e[..., None].astype(h.dtype) * _swiglu(h, w1[e], w3[e], w2[e])
    return out


def _layer(x, mod, cache, li, P):
    sh1, sc1, g1, sh2, sc2, g2 = jnp.split(mod, 6, axis=-1)
    hmix = _rms_norm(x, P['norm_g'][li, 0]) * (1.0 + sc1) + sh1
    y, new_cache = _token_mixer(hmix, cache, li, P)
    x = x + g1 * _rms_norm(y, P['norm_g'][li, 1])
    hff = _rms_norm(x, P['norm_g'][li, 2]) * (1.0 + sc2) + sh2
    j = li // 2
    if li % 2 == 0:
        y = _swiglu(hff, P['ffn_w1'][j], P['ffn_w3'][j], P['ffn_w2'][j])
    else:
        y = _moe(hff, P['moe_router'][j], P['moe_w1'][j], P['moe_w3'][j], P['moe_w2'][j])
    x = x + g2 * _rms_norm(y, P['norm_g'][li, 3])
    return x, new_cache


def setup_inputs(seed: int = 0) -> dict:
    key = jax.random.key(seed)
    ks = jax.random.split(key, 32)
    D = D_MODEL
    def nrm(k, shape, scale=1.0):
        return jax.random.normal(k, shape, jnp.float32) * scale
    return {
        'x_prompt': nrm(ks[0], (BATCH, SEQ, D)),
        'x_sample': nrm(ks[1], (DEC_BATCH, DEC_SEQ, D)),
        'cache_diff_k': nrm(ks[2], (DEC_BATCH, DEPTH, PAST_LEN, H_A, 2 * HD_A)),
        'cache_diff_v': nrm(ks[3], (DEC_BATCH, DEPTH, PAST_LEN, H_A, 2 * HD_A)),
        'cache_mla_ckv': nrm(ks[4], (DEC_BATCH, DEPTH, PAST_LEN, KV_LORA)),
        'cache_mla_krope': nrm(ks[5], (DEC_BATCH, DEPTH, PAST_LEN, ROPE_B)),
        'cache_na_k': nrm(ks[6], (DEC_BATCH, DEPTH, PAST_LEN, H_C, HD_C)),
        'cache_na_v': nrm(ks[7], (DEC_BATCH, DEPTH, PAST_LEN, H_C, HD_C)),
        'c': nrm(ks[8], (DEC_BATCH, D)),
        'c_ctx': nrm(ks[9], (D,)),
        'norm_g': 1.0 + nrm(ks[10], (DEPTH, 4, D), 0.02),
        'ada_w': nrm(ks[11], (DEPTH, D, 6 * D), 0.5 * D ** -0.5),
        'ada_b': nrm(ks[12], (DEPTH, 6 * D), 0.02),
        'w_in': nrm(ks[13], (DEPTH, D, D_IN), D ** -0.5),
        'diff_lambda': nrm(ks[14], (DEPTH, 4, HD_A), 0.1),
        'diff_subln_g': 1.0 + nrm(ks[15], (DEPTH, 2 * HD_A), 0.02),
        'mla_q_norm_g': 1.0 + nrm(ks[16], (DEPTH, Q_LORA), 0.02),
        'mla_kv_norm_g': 1.0 + nrm(ks[17], (DEPTH, KV_LORA), 0.02),
        'mla_w_uq': nrm(ks[18], (DEPTH, Q_LORA, H_B * (NOPE_B + ROPE_B)), Q_LORA ** -0.5),
        'mla_w_ukv': nrm(ks[19], (DEPTH, KV_LORA, H_B * (NOPE_B + V_B)), KV_LORA ** -0.5),
        'na_rel_bias': nrm(ks[20], (DEPTH, H_C, 2 * NA_KR_MAX - 1, 2 * NA_KC - 1), 0.1),
        'w_branch': nrm(ks[21], (DEPTH, N_BRANCH, BR_W, D), BR_W ** -0.5),
        'w_out': nrm(ks[22], (DEPTH, D, D), D ** -0.5),
        'ffn_w1': nrm(ks[23], (N_DENSE, D, D_FF), D ** -0.5),
        'ffn_w3': nrm(ks[24], (N_DENSE, D, D_FF), D ** -0.5),
        'ffn_w2': nrm(ks[25], (N_DENSE, D_FF, D), D_FF ** -0.5),
        'moe_router': nrm(ks[26], (N_MOE, D, N_EXPERTS), D ** -0.5),
        'moe_w1': nrm(ks[27], (N_MOE, N_EXPERTS, D, D_FF_E), D ** -0.5),
        'moe_w3': nrm(ks[28], (N_MOE, N_EXPERTS, D, D_FF_E), D ** -0.5),
        'moe_w2': nrm(ks[29], (N_MOE, N_EXPERTS, D_FF_E, D), D_FF_E ** -0.5),
    }


def reference(x_prompt, x_sample, cache_diff_k, cache_diff_v, cache_mla_ckv, cache_mla_krope,
              cache_na_k, cache_na_v, c, c_ctx, norm_g, ada_w, ada_b, w_in, diff_lambda,
              diff_subln_g, mla_q_norm_g, mla_kv_norm_g, mla_w_uq, mla_w_ukv, na_rel_bias,
              w_branch, w_out, ffn_w1, ffn_w3, ffn_w2, moe_router, moe_w1, moe_w3, moe_w2):
    P = {'norm_g': norm_g, 'w_in': w_in, 'diff_lambda': diff_lambda, 'diff_subln_g': diff_subln_g,
         'mla_q_norm_g': mla_q_norm_g, 'mla_kv_norm_g': mla_kv_norm_g, 'mla_w_uq': mla_w_uq,
         'mla_w_ukv': mla_w_ukv, 'na_rel_bias': na_rel_bias, 'w_branch': w_branch, 'w_out': w_out,
         'ffn_w1': ffn_w1, 'ffn_w3': ffn_w3, 'ffn_w2': ffn_w2, 'moe_router': moe_router,
         'moe_w1': moe_w1, 'moe_w3': moe_w3, 'moe_w2': moe_w2}
    x = x_prompt
    layer_caches = []
    for li in range(DEPTH):
        mod = (jnp.einsum('d,de->e', jax.nn.silu(c_ctx), ada_w[li]) + ada_b[li])[None, None, :]
        x, nc = _layer(x, mod, None, li, P)
        layer_caches.append(nc)
    y_prompt = x
    state_diff_k = jnp.stack([lc[0] for lc in layer_caches], axis=1)
    state_diff_v = jnp.stack([lc[1] for lc in layer_caches], axis=1)
    state_mla_ckv = jnp.stack([lc[2] for lc in layer_caches], axis=1)
    state_mla_krope = jnp.stack([lc[3] for lc in layer_caches], axis=1)
    state_na_k = jnp.stack([lc[4] for lc in layer_caches], axis=1)
    state_na_v = jnp.stack([lc[5] for lc in layer_caches], axis=1)
    x = x_sample
    for li in range(DEPTH):
        mod = (jnp.einsum('bd,de->be', jax.nn.silu(c), ada_w[li]) + ada_b[li])[:, None, :]
        cache = (cache_diff_k[:, li], cache_diff_v[:, li], cache_mla_ckv[:, li],
                 cache_mla_krope[:, li], cache_na_k[:, li], cache_na_v[:, li])
        x, _ = _layer(x, mod, cache, li, P)
    y_sample = x
    return (y_prompt, y_sample, state_diff_k, state_diff_v, state_mla_ckv, state_mla_krope, state_na_k, state_na_v)
```

```python
import functools
import math

import numpy as np
import jax
import jax.numpy as jnp
from jax import lax
from jax.experimental import pallas as pl
from jax.experimental.pallas import tpu as pltpu

GRID_W = 64
RMS_EPS = 1e-6
ROPE_THETA = 10000.0
H_A, HD_A = 4, 128
H_B, NOPE_B, ROPE_B, V_B = 8, 128, 64, 128
H_C, HD_C = 8, 128
NA_KR_MAX, NA_KC = 8, 16
N_BRANCH = 3
N_EXPERTS = 8
TOP_K = 2
BR_W = 1024

LANES = 128
V7X_VMEM_BYTES = 64 * 1024 * 1024
VMEM_CAP_BYTES = V7X_VMEM_BYTES - 8 * 1024 * 1024
NA_ROWS_PER_BLOCK = 4
MASK_VALUE = -1e30

F32 = jnp.float32
BF16 = jnp.bfloat16


def _pick(n, pref, mult=8):
    t = min(pref, n)
    t -= t % mult
    while t > mult and n % t:
        t -= mult
    assert t > 0 and n % t == 0, (n, pref, mult)
    return t


def _cparams(sems, vmem_est):
    limit = int(min(max(vmem_est, 16 * 1024 * 1024), VMEM_CAP_BYTES))
    return pltpu.CompilerParams(dimension_semantics=sems, vmem_limit_bytes=limit)


def _rms(v):
    return v * lax.rsqrt(jnp.mean(v * v, axis=-1, keepdims=True) + RMS_EPS)


def _rope128(a, cos, sa, sb, shift):
    return a * cos + pltpu.roll(a, LANES - shift, 1) * sa + pltpu.roll(a, shift, 1) * sb


def _ada_body(c_ref, w_ref, b_ref, o_ref):
    c = c_ref[...]
    s = (c * jax.nn.sigmoid(c)).astype(BF16)
    o_ref[...] = jnp.dot(s, w_ref[...].astype(BF16), preferred_element_type=F32) + b_ref[...]


def _ada_mod(cvec, ada_w, ada_b):
    depth, d, n = ada_w.shape
    r = cvec.shape[0]
    tn = _pick(n, 1024, LANES)
    return pl.pallas_call(
        _ada_body,
        grid=(depth, n // tn),
        in_specs=[pl.BlockSpec((r, d), lambda l, j: (0, 0)),
                  pl.BlockSpec((None, d, tn), lambda l, j: (l, 0, j)),
                  pl.BlockSpec((None, 1, tn), lambda l, j: (l, 0, j))],
        out_specs=pl.BlockSpec((None, r, tn), lambda l, j: (l, 0, j)),
        out_shape=jax.ShapeDtypeStruct((depth, r, n), F32),
        compiler_params=_cparams(("arbitrary", "arbitrary"), 3 * d * tn * 4 + d * tn * 2),
        name="ada_mod",
    )(cvec, ada_w, ada_b.reshape(depth, 1, n))


def _resid_norm_body(*refs, has_y, want_x, want_h, has_router):
    it = iter(refs)
    x_ref = next(it)
    x = x_ref[...]
    if has_y:
        y_ref, gpost_ref, gate_ref = next(it), next(it), next(it)
        x = x + gate_ref[...] * (_rms(y_ref[...].astype(F32)) * gpost_ref[...])
    if want_h:
        gpre_ref, sc_ref, sh_ref = next(it), next(it), next(it)
    if has_router:
        r_ref = next(it)
    if want_x:
        next(it)[...] = x
    if want_h:
        h = _rms(x) * gpre_ref[...]
        h = h * (1.0 + sc_ref[...]) + sh_ref[...]
        next(it)[...] = h.astype(BF16)
        if has_router:
            next(it)[...] = jnp.dot(h, r_ref[...], precision=lax.Precision.HIGHEST,
                                    preferred_element_type=F32)


def _resid_norm(x, y, mods, norm_g, *, li_post, k_gate, li_pre, k_mod, m_ctx, seg_len,
                want_x, want_h, router=None):
    m, d = x.shape
    tm = _pick(math.gcd(m_ctx, seg_len), 256)

    def seg(i):
        return jnp.maximum((i * tm - m_ctx) // seg_len + 1, 0)

    row = pl.BlockSpec((tm, d), lambda i: (i, 0))
    in_specs, args = [row], [x]
    if y is not None:
        k_post = 1 if k_gate == 2 else 3
        in_specs += [row,
                     pl.BlockSpec((None, None, 1, d), lambda i: (li_post, k_post, 0, 0)),
                     pl.BlockSpec((None, None, None, 1, d), lambda i: (li_post, seg(i), k_gate, 0, 0))]
        args += [y, norm_g, mods]
    if want_h:
        k_norm = 0 if k_mod == 0 else 2
        in_specs += [pl.BlockSpec((None, None, 1, d), lambda i: (li_pre, k_norm, 0, 0)),
                     pl.BlockSpec((None, None, None, 1, d), lambda i: (li_pre, seg(i), k_mod + 1, 0, 0)),
                     pl.BlockSpec((None, None, None, 1, d), lambda i: (li_pre, seg(i), k_mod, 0, 0))]
        args += [norm_g, mods, mods]
    if router is not None:
        in_specs.append(pl.BlockSpec(router.shape, lambda i: (0, 0)))
        args.append(router)
    out_specs, out_shape = [], []
    if want_x:
        out_specs.append(row)
        out_shape.append(jax.ShapeDtypeStruct((m, d), F32))
    if want_h:
        out_specs.append(row)
        out_shape.append(jax.ShapeDtypeStruct((m, d), BF16))
        if router is not None:
            out_specs.append(pl.BlockSpec((tm, router.shape[1]), lambda i: (i, 0)))
            out_shape.append(jax.ShapeDtypeStruct((m, router.shape[1]), F32))
    body = functools.partial(_resid_norm_body, has_y=y is not None, want_x=want_x, want_h=want_h,
                             has_router=router is not None)
    return pl.pallas_call(
        body, grid=(m // tm,), in_specs=in_specs, out_specs=out_specs, out_shape=out_shape,
        compiler_params=_cparams(("arbitrary",), 12 * tm * d * 4 + d * LANES * 8),
        name="resid_norm",
    )(*args)


def _matmul_body(*refs, tn, scale, act, rope_shift, rope_pattern):
    x_ref, w_ref = refs[0], refs[1]
    o_ref = refs[-1]
    acc = jnp.dot(x_ref[...], w_ref[...], preferred_element_type=F32)
    if scale is not None:
        acc = acc * scale
    if act == "sigmoid":
        acc = jax.nn.sigmoid(acc)
    if rope_shift is None:
        o_ref[...] = acc.astype(o_ref.dtype)
        return
    cos, sa, sb = refs[2][...], refs[3][...], refs[4][...]
    for g in range(tn // LANES):
        a = acc[:, g * LANES:(g + 1) * LANES]
        if rope_pattern[g % len(rope_pattern)]:
            a = _rope128(a, cos, sa, sb, rope_shift)
        o_ref[:, g * LANES:(g + 1) * LANES] = a.astype(o_ref.dtype)


def _matmul(x, w, *, rows, row0, col0, ncols, wsel, out_dtype, tm, tn, scale=None, act=None,
            rope=None, name="matmul"):
    k = x.shape[1]
    assert row0 % tm == 0 and rows % tm == 0 and col0 % tn == 0 and ncols % tn == 0, (row0, rows, col0, ncols, tm, tn)
    rb0, cb0 = row0 // tm, col0 // tn
    in_specs = [pl.BlockSpec((tm, k), lambda j, i: (rb0 + i, 0)),
                pl.BlockSpec((None, k, tn), lambda j, i: (wsel, 0, cb0 + j))]
    args = [x, w]
    rope_shift = rope_pattern = None
    if rope is not None:
        tabs, rope_shift, rope_pattern = rope
        nper = tabs[0].shape[0] // tm
        assert tabs[0].shape[0] % tm == 0
        for t in tabs:
            in_specs.append(pl.BlockSpec((tm, LANES), lambda j, i: (i % nper, 0)))
            args.append(t)
    osz = jnp.dtype(out_dtype).itemsize
    est = 2 * (tm * k * 2 + k * tn * 2 + tm * tn * osz) + 3 * tm * tn * 4 + 6 * tm * LANES * 4
    body = functools.partial(_matmul_body, tn=tn, scale=scale, act=act, rope_shift=rope_shift,
                             rope_pattern=rope_pattern)
    return pl.pallas_call(
        body, grid=(ncols // tn, rows // tm), in_specs=in_specs,
        out_specs=pl.BlockSpec((tm, tn), lambda j, i: (i, j)),
        out_shape=jax.ShapeDtypeStruct((rows, ncols), out_dtype),
        compiler_params=_cparams(("arbitrary", "arbitrary"), est),
        name=name,
    )(*args)


def _mla_prep_body(zm_ref, kr_ref, qg_ref, kg_ref, cos_ref, sa_ref, sb_ref, cq_o, ckv_o, ckr_o, *,
                   q_lora, kv_lora, n_ctx_tiles):
    z = zm_ref[...]
    cq_o[...] = (_rms(z[:, :q_lora]) * qg_ref[...]).astype(BF16)
    ckv = _rms(z[:, q_lora:]) * kg_ref[...]
    ckv_o[...] = ckv
    ckr_o[:, :kv_lora] = ckv.astype(BF16)
    i = pl.program_id(0)

    @pl.when(i < n_ctx_tiles)
    def _():
        ckr_o[:, kv_lora:] = kr_ref[...].astype(BF16)

    @pl.when(i >= n_ctx_tiles)
    def _():
        ckr_o[:, kv_lora:] = _rope128(kr_ref[...], cos_ref[...], sa_ref[...], sb_ref[...],
                                      ROPE_B // 4).astype(BF16)


def _mla_prep(zm, kr, qg, kg, tabs, *, q_lora, kv_lora, m_ctx, seg_len):
    m = zm.shape[0]
    tm = _pick(math.gcd(m_ctx, seg_len), 512)
    nct, nper = m_ctx // tm, seg_len // tm
    tab_spec = pl.BlockSpec((tm, LANES), lambda i: (jnp.maximum(i - nct, 0) % nper, 0))
    body = functools.partial(_mla_prep_body, q_lora=q_lora, kv_lora=kv_lora, n_ctx_tiles=nct)
    return pl.pallas_call(
        body, grid=(m // tm,),
        in_specs=[pl.BlockSpec((tm, q_lora + kv_lora), lambda i: (i, 0)),
                  pl.BlockSpec((tm, LANES), lambda i: (i, 0)),
                  pl.BlockSpec((1, q_lora), lambda i: (0, 0)),
                  pl.BlockSpec((1, kv_lora), lambda i: (0, 0)),
                  tab_spec, tab_spec, tab_spec],
        out_specs=[pl.BlockSpec((tm, q_lora), lambda i: (i, 0)),
                   pl.BlockSpec((tm, kv_lora), lambda i: (i, 0)),
                   pl.BlockSpec((tm, kv_lora + LANES), lambda i: (i, 0))],
        out_shape=[jax.ShapeDtypeStruct((m, q_lora), BF16),
                   jax.ShapeDtypeStruct((m, kv_lora), F32),
                   jax.ShapeDtypeStruct((m, kv_lora + LANES), BF16)],
        compiler_params=_cparams(("arbitrary",), 8 * tm * (q_lora + kv_lora + 4 * LANES) * 4),
        name="mla_prep",
    )(zm, kr, qg, kg, *tabs)


_NT = (((1,), (1,)), ((), ()))


def _softmax_parts(q, k):
    s = lax.dot_general(q, k, _NT, preferred_element_type=F32)
    p = jnp.exp(s - jnp.max(s, axis=-1, keepdims=True))
    return p, 1.0 / jnp.sum(p, axis=-1, keepdims=True)


def _diff_attn_body(lam_ref, q_ref, k_ref, v_ref, g_ref, *rest, q_scale, out_scale):
    o_ref = rest[-1]
    q = q_ref[...]
    if q_scale is not None:
        q = (q.astype(F32) * q_scale).astype(BF16)
    k = k_ref[...].astype(BF16)
    p1, r1 = _softmax_parts(q[:, :HD_A], k[:, :HD_A])
    p2, r2 = _softmax_parts(q[:, HD_A:], k[:, HD_A:])
    p = (p1 * r1 - p2 * (lam_ref[0] * r2)).astype(BF16)
    o = jnp.dot(p, v_ref[...].astype(BF16), preferred_element_type=F32)
    o_ref[...] = (_rms(o) * g_ref[...] * out_scale).astype(o_ref.dtype)


def _softmax_attn_body(q_ref, k_ref, v_ref, *rest, q_scale):
    o_ref = rest[-1]
    q = q_ref[...]
    if q_scale is not None:
        q = (q.astype(F32) * q_scale).astype(BF16)
    p, r = _softmax_parts(q, k_ref[...].astype(BF16))
    o = jnp.dot(p.astype(BF16), v_ref[...].astype(BF16), preferred_element_type=F32)
    o_ref[...] = (o * r).astype(o_ref.dtype)


def _attn_call(body, arrays, maps, *, nb, nh, nq, nk, tq, dq, dk, dv, out_rows, out_cols, out_row0,
               prev_out, extra_in=(), extra_specs=(), smem_in=(), name):
    q, k, v = arrays
    qmap, kmap, vmap = maps
    in_specs = [pl.BlockSpec(memory_space=pltpu.SMEM) for _ in smem_in]
    in_specs += [pl.BlockSpec((None, tq, dq), qmap), pl.BlockSpec((None, nk, dk), kmap),
                 pl.BlockSpec((None, nk, dv), vmap)]
    in_specs += list(extra_specs)
    args = list(smem_in) + [q, k, v] + list(extra_in)
    aliases = {}
    if prev_out is not None:
        in_specs.append(pl.BlockSpec(memory_space=pl.ANY))
        aliases = {len(args): 0}
        args.append(prev_out)
    orb0 = out_row0 // tq
    assert out_row0 % tq == 0
    est = 2 * (tq * dq + nk * dk + nk * dv) * 4 + 8 * tq * nk * 4
    return pl.pallas_call(
        body, grid=(nb, nh, nq // tq), in_specs=in_specs,
        out_specs=pl.BlockSpec((tq, dv), lambda b, h, i: (orb0 + b * (nq // tq) + i, h)),
        out_shape=jax.ShapeDtypeStruct((out_rows, out_cols), BF16),
        input_output_aliases=aliases,
        compiler_params=_cparams(("arbitrary", "arbitrary", "arbitrary"), est),
        name=name,
    )(*args)


def _na_geometry(rows):
    rpb = NA_ROWS_PER_BLOCK
    kr = min(NA_KR_MAX, rows)
    span = rpb + kr - 1
    assert rows % rpb == 0 and rows >= span, rows
    nblk = rows // rpb
    r = np.arange(rows)
    row_start = np.clip(r - kr // 2, 0, rows - kr)
    blk_start = np.clip(rpb * np.arange(nblk) - kr // 2, 0, rows - span)
    cols = np.arange(GRID_W)
    col_start = np.clip(cols - NA_KC // 2, 0, GRID_W - NA_KC)
    sigs, cls_of_blk, dr_l, dc_l, ok_l = {}, [], [], [], []
    for b in range(nblk):
        qr = rpb * b + np.arange(rpb)
        key_row = blk_start[b] + np.arange(span)
        dr = key_row[None, :] - qr[:, None] + (NA_KR_MAX - 1)
        ok_r = (key_row[None, :] >= row_start[qr][:, None]) & (key_row[None, :] < row_start[qr][:, None] + kr)
        sig = (dr.tobytes(), ok_r.tobytes())
        if sig not in sigs:
            sigs[sig] = len(sigs)
            dc = cols[None, :] - cols[:, None] + (NA_KC - 1)
            ok_c = (cols[None, :] >= col_start[:, None]) & (cols[None, :] < col_start[:, None] + NA_KC)
            shape = (rpb, GRID_W, span, GRID_W)
            ok = (ok_r[:, None, :, None] & ok_c[None, :, None, :])
            dr_l.append(np.broadcast_to(np.clip(dr, 0, 2 * NA_KR_MAX - 2)[:, None, :, None], shape).reshape(rpb * GRID_W, span * GRID_W))
            dc_l.append(np.broadcast_to(np.clip(dc, 0, 2 * NA_KC - 2)[None, :, None, :], shape).reshape(rpb * GRID_W, span * GRID_W))
            ok_l.append(ok.reshape(rpb * GRID_W, span * GRID_W))
        cls_of_blk.append(sigs[sig])
    return dict(span=span, nblk=nblk, blk_start=blk_start, cls=np.asarray(cls_of_blk, np.int32),
                dr=np.stack(dr_l), dc=np.stack(dc_l), ok=np.stack(ok_l))


def _na_body(cls_ref, start_ref, q_ref, k_ref, v_ref, kc_ref, vc_ref, bias_ref, prev_ref, o_ref, *,
             q_scale, span_tok):
    del cls_ref, prev_ref
    blk = pl.program_id(2)
    start = pl.multiple_of(start_ref[blk] * GRID_W, GRID_W)
    q = (q_ref[...].astype(F32) * q_scale).astype(BF16)
    kl = k_ref[pl.ds(start, span_tok), :].astype(BF16)
    vl = v_ref[pl.ds(start, span_tok), :].astype(BF16)
    s_loc = lax.dot_general(q, kl, _NT, preferred_element_type=F32) + bias_ref[...]
    s_ctx = lax.dot_general(q, kc_ref[...].astype(BF16), _NT, preferred_element_type=F32)
    mx = jnp.maximum(jnp.max(s_loc, axis=-1, keepdims=True), jnp.max(s_ctx, axis=-1, keepdims=True))
    p_loc = jnp.exp(s_loc - mx)
    p_ctx = jnp.exp(s_ctx - mx)
    r = 1.0 / (jnp.sum(p_loc, axis=-1, keepdims=True) + jnp.sum(p_ctx, axis=-1, keepdims=True))
    o = jnp.dot(p_loc.astype(BF16), vl, preferred_element_type=F32)
    o = o + jnp.dot(p_ctx.astype(BF16), vc_ref[...].astype(BF16), preferred_element_type=F32)
    o_ref[...] = (o * r).astype(o_ref.dtype)


def _merge_body(a_ref, b_ref, c_ref, ga_ref, gb_ref, gc_ref, wa_ref, wb_ref, wc_ref, o_ref):
    acc = ga_ref[...].astype(F32) * jnp.dot(a_ref[...], wa_ref[...], preferred_element_type=F32)
    acc += gb_ref[...].astype(F32) * jnp.dot(b_ref[...], wb_ref[...], preferred_element_type=F32)
    acc += gc_ref[...].astype(F32) * jnp.dot(c_ref[...], wc_ref[...], preferred_element_type=F32)
    o_ref[...] = acc.astype(o_ref.dtype)


def _merge(oa, ob, oc, gates, wb, li):
    m, kb = oa.shape
    d = wb.shape[-1]
    tm, tn = _pick(m, 1024), _pick(d, 1024, 256)
    nj = d // tn
    br = pl.BlockSpec((tm, kb), lambda j, i: (i, 0))
    gspec = [pl.BlockSpec((tm, tn), functools.partial(lambda j, i, s: (i, s * nj + j), s=s)) for s in range(N_BRANCH)]
    wspec = [pl.BlockSpec((None, None, kb, tn), functools.partial(lambda j, i, s: (li, s, 0, j), s=s)) for s in range(N_BRANCH)]
    est = 2 * (3 * tm * kb * 2 + 3 * tm * tn * 2 + 3 * kb * tn * 2 + tm * tn * 2) + 4 * tm * tn * 4
    return pl.pallas_call(
        _merge_body, grid=(nj, m // tm), in_specs=[br, br, br] + gspec + wspec,
        out_specs=pl.BlockSpec((tm, tn), lambda j, i: (i, j)),
        out_shape=jax.ShapeDtypeStruct((m, d), BF16),
        compiler_params=_cparams(("arbitrary", "arbitrary"), est),
        name="branch_merge",
    )(oa, ob, oc, gates, gates, gates, wb, wb, wb)


def _ffn1_body(te_ref, ts_ref, nu_ref, x_ref, w1_ref, w3_ref, o_ref):
    del te_ref, ts_ref

    @pl.when(pl.program_id(1) < nu_ref[0])
    def _():
        x = x_ref[...]
        h1 = jnp.dot(x, w1_ref[...], preferred_element_type=F32)
        h3 = jnp.dot(x, w3_ref[...], preferred_element_type=F32)
        o_ref[...] = (h1 * jax.nn.sigmoid(h1) * h3).astype(o_ref.dtype)


def _ffn2_body(te_ref, ts_ref, nu_ref, h_ref, w2_ref, o_ref):
    del te_ref, ts_ref

    @pl.when(pl.program_id(1) < nu_ref[0])
    def _():
        o_ref[...] = jnp.dot(h_ref[...], w2_ref[...], preferred_element_type=F32).astype(o_ref.dtype)


def _grouped_ffn(xs, w1, w3, w2, tile_expert, tile_src, n_used, tm):
    r, d = xs.shape
    f = w1.shape[-1]
    nt = r // tm
    tf = _pick(f, 1408, LANES)
    tn = _pick(d, 1024, 256)
    gs1 = pltpu.PrefetchScalarGridSpec(
        num_scalar_prefetch=3, grid=(f // tf, nt),
        in_specs=[pl.BlockSpec((tm, d), lambda j, t, te, ts, nu: (ts[t], 0)),
                  pl.BlockSpec((None, d, tf), lambda j, t, te, ts, nu: (te[t], 0, j)),
                  pl.BlockSpec((None, d, tf), lambda j, t, te, ts, nu: (te[t], 0, j))],
        out_specs=pl.BlockSpec((tm, tf), lambda j, t, te, ts, nu: (ts[t], j)))
    hid = pl.pallas_call(
        _ffn1_body, grid_spec=gs1, out_shape=jax.ShapeDtypeStruct((r, f), BF16),
        compiler_params=_cparams(("arbitrary", "arbitrary"),
                                 2 * (tm * d * 2 + 2 * d * tf * 2 + tm * tf * 2) + 4 * tm * tf * 4),
        name="ffn_up",
    )(tile_expert, tile_src, n_used, xs, w1, w3)
    gs2 = pltpu.PrefetchScalarGridSpec(
        num_scalar_prefetch=3, grid=(d // tn, nt),
        in_specs=[pl.BlockSpec((tm, f), lambda j, t, te, ts, nu: (ts[t], 0)),
                  pl.BlockSpec((None, f, tn), lambda j, t, te, ts, nu: (te[t], 0, j))],
        out_specs=pl.BlockSpec((tm, tn), lambda j, t, te, ts, nu: (ts[t], j)))
    return pl.pallas_call(
        _ffn2_body, grid_spec=gs2, out_shape=jax.ShapeDtypeStruct((r, d), F32),
        compiler_params=_cparams(("arbitrary", "arbitrary"),
                                 2 * (tm * f * 2 + f * tn * 2 + tm * tn * 4) + 2 * tm * tn * 4),
        name="ffn_down",
    )(tile_expert, tile_src, n_used, hid, w2)


def _moe_plan(logits, tm):
    m = logits.shape[0]
    top_v, top_i = lax.top_k(logits, TOP_K)
    top_w = jax.nn.softmax(top_v, axis=-1)
    flat_e = top_i.reshape(-1)
    onehot = (flat_e[:, None] == jnp.arange(N_EXPERTS)[None, :]).astype(jnp.int32)
    csum = jnp.cumsum(onehot, axis=0)
    counts = csum[-1]
    rank = jnp.take_along_axis(csum, flat_e[:, None], axis=1)[:, 0] - 1
    padded = ((counts + tm - 1) // tm) * tm
    ends = jnp.cumsum(padded)
    offs = ends - padded
    dest = offs[flat_e] + rank
    n_rows = TOP_K * m + N_EXPERTS * tm
    nt = n_rows // tm
    row_token = jnp.zeros((n_rows,), jnp.int32).at[dest].set(jnp.arange(TOP_K * m, dtype=jnp.int32) // TOP_K)
    n_used = (ends[-1] // tm).astype(jnp.int32)
    tile_src = jnp.minimum(jnp.arange(nt, dtype=jnp.int32), n_used - 1)
    tile_expert = jnp.sum((tile_src[:, None] * tm >= ends[None, :]).astype(jnp.int32), axis=1)
    tile_expert = jnp.minimum(tile_expert, N_EXPERTS - 1).astype(jnp.int32)
    return row_token, dest.reshape(m, TOP_K), top_w, tile_expert, tile_src, n_used.reshape(1)


def _rope_tables(n_tok, dim):
    half = dim // 2
    t = jnp.arange(n_tok)
    row = (t // GRID_W).astype(F32)
    col = (t % GRID_W).astype(F32)
    inv = ROPE_THETA ** (-jnp.arange(0, half, 2, dtype=F32) / half)
    ar = row[:, None] * inv[None, :]
    ac = col[:, None] * inv[None, :]
    ang = jnp.concatenate([ar, ar, ac, ac], axis=-1)
    cos, sin = jnp.cos(ang), jnp.sin(ang)
    first = (np.arange(dim) % half) < (dim // 4)
    sa = jnp.where(first[None, :], -sin, 0.0)
    sb = jnp.where(first[None, :], 0.0, sin)
    pad = LANES - dim
    if pad:
        cos = jnp.pad(cos, ((0, 0), (0, pad)), constant_values=1.0)
        sa = jnp.pad(sa, ((0, 0), (0, pad)))
        sb = jnp.pad(sb, ((0, 0), (0, pad)))
    return cos, sa, sb


def kernel(x_prompt, x_sample, cache_diff_k, cache_diff_v, cache_mla_ckv, cache_mla_krope, cache_na_k, cache_na_v, c, c_ctx, norm_g, ada_w, ada_b, w_in, diff_lambda, diff_subln_g, mla_q_norm_g, mla_kv_norm_g, mla_w_uq, mla_w_ukv, na_rel_bias, w_branch, w_out, ffn_w1, ffn_w3, ffn_w2, moe_router, moe_w1, moe_w3, moe_w2):
    batch, seq, d = x_prompt.shape
    dec_batch, dec_seq, _ = x_sample.shape
    depth = norm_g.shape[0]
    past = cache_diff_k.shape[2]
    q_lora, kv_lora = mla_q_norm_g.shape[1], mla_kv_norm_g.shape[1]
    m_ctx, m_lat = batch * seq, dec_batch * dec_seq
    m = m_ctx + m_lat
    nk_lat = dec_seq + past
    assert dec_seq % GRID_W == 0 and kv_lora % LANES == 0 and q_lora % LANES == 0

    a3 = 3 * BR_W
    c_krope = a3 + q_lora + kv_lora
    c_rest = c_krope + ROPE_B
    w_inp = jnp.concatenate(
        [w_in[..., :c_krope], w_in[..., c_rest:], w_in[..., c_krope:c_rest],
         jnp.zeros(w_in.shape[:2] + (LANES - ROPE_B,), w_in.dtype)], axis=-1).astype(BF16)
    col_va, col_cq = 2 * BR_W, a3
    col_c = a3 + q_lora + kv_lora
    col_g = col_c + 3 * BR_W
    col_kr = col_g + N_BRANCH * d
    wq = mla_w_uq.reshape(depth, q_lora, H_B, NOPE_B + ROPE_B)
    w_uqp = jnp.pad(wq, ((0, 0), (0, 0), (0, 0), (0, 2 * LANES - NOPE_B - ROPE_B))).reshape(depth, q_lora, H_B * 2 * LANES).astype(BF16)
    wkv = mla_w_ukv.reshape(depth, kv_lora, H_B, NOPE_B + V_B)
    wk_top = jnp.pad(wkv[..., :NOPE_B], ((0, 0), (0, 0), (0, 0), (0, 2 * LANES - NOPE_B))).reshape(depth, kv_lora, H_B * 2 * LANES)
    eye = np.zeros((LANES, H_B, 2 * LANES), np.float32)
    for r_ in range(ROPE_B):
        eye[r_, :, NOPE_B + r_] = 1.0
    wk_aug = jnp.concatenate([wk_top, jnp.broadcast_to(jnp.asarray(eye.reshape(LANES, -1)), (depth, LANES, H_B * 2 * LANES))], axis=1).astype(BF16)
    wv_aug = jnp.pad(wkv[..., NOPE_B:].reshape(depth, kv_lora, H_B * V_B), ((0, 0), (0, LANES), (0, 0))).astype(BF16)
    w_branch_b = w_branch.astype(BF16)
    w_out_b = w_out.astype(BF16)
    ffn_w1b, ffn_w3b, ffn_w2b = ffn_w1.astype(BF16), ffn_w3.astype(BF16), ffn_w2.astype(BF16)
    moe_w1b, moe_w3b, moe_w2b = moe_w1.astype(BF16), moe_w3.astype(BF16), moe_w2.astype(BF16)
    router_p = jnp.pad(moe_router, ((0, 0), (0, 0), (0, LANES - N_EXPERTS)))
    norm_g4 = norm_g.reshape(depth, 4, 1, d)

    n_seg = 1 + dec_batch
    r_pad = -(-n_seg // 8) * 8
    cvec = jnp.zeros((r_pad, d), F32).at[0].set(c_ctx).at[1:n_seg].set(c)
    mods = _ada_mod(cvec, ada_w, ada_b).reshape(depth, r_pad, 6, 1, d)

    tabs_a = _rope_tables(dec_seq, HD_A)
    tabs_b = _rope_tables(dec_seq, ROPE_B)
    na = _na_geometry(dec_seq // GRID_W)
    lam_inits = [0.8 - 0.6 * math.exp(-0.3 * li) for li in range(depth)]
    lp = diff_lambda.astype(F32)
    lams = jnp.exp(jnp.sum(lp[:, 0] * lp[:, 1], axis=-1)) - jnp.exp(jnp.sum(lp[:, 2] * lp[:, 3], axis=-1)) + jnp.asarray(lam_inits, F32)

    tm_big = _pick(math.gcd(m_ctx, dec_seq), 1024)
    tq_lat = _pick(dec_seq, 256)
    rn = functools.partial(_resid_norm, m_ctx=m_ctx, seg_len=dec_seq)
    mm = functools.partial(_matmul, tm=tm_big)

    x = jnp.concatenate([x_prompt.reshape(m_ctx, d), x_sample.reshape(m_lat, d)], axis=0)
    (hmix,) = rn(x, None, mods, norm_g4, li_post=0, k_gate=2, li_pre=0, k_mod=0, want_x=False, want_h=True)
    states = [[] for _ in range(6)]

    for li in range(depth):
        za_ctx = mm(hmix, w_inp, rows=m_ctx, row0=0, col0=0, ncols=2 * BR_W, wsel=li, out_dtype=F32, tn=1024, name="in_qk_ctx")
        qa_lat = mm(hmix, w_inp, rows=m_lat, row0=m_ctx, col0=0, ncols=BR_W, wsel=li, out_dtype=BF16, tn=1024,
                    scale=HD_A ** -0.5, rope=(tabs_a, HD_A // 4, (True,)), name="in_qa_lat")
        ka_lat = mm(hmix, w_inp, rows=m_lat, row0=m_ctx, col0=BR_W, ncols=BR_W, wsel=li, out_dtype=BF16, tn=1024,
                    rope=(tabs_a, HD_A // 4, (True,)), name="in_ka_lat")
        zv = mm(hmix, w_inp, rows=m, row0=0, col0=col_va, ncols=BR_W, wsel=li, out_dtype=F32, tn=1024, name="in_va")
        zm = mm(hmix, w_inp, rows=m, row0=0, col0=col_cq, ncols=q_lora + kv_lora, wsel=li, out_dtype=F32, tn=q_lora + kv_lora, name="in_mla")
        zc = mm(hmix, w_inp, rows=m, row0=0, col0=col_c, ncols=3 * BR_W, wsel=li, out_dtype=F32, tn=768, name="in_na")
        gates = mm(hmix, w_inp, rows=m, row0=0, col0=col_g, ncols=N_BRANCH * d, wsel=li, out_dtype=BF16, tn=768, act="sigmoid", name="in_gates")
        kr = mm(hmix, w_inp, rows=m, row0=0, col0=col_kr, ncols=LANES, wsel=li, out_dtype=F32, tn=LANES, name="in_krope")

        g_sub = diff_subln_g[li].reshape(1, 2 * HD_A)
        g_spec = pl.BlockSpec((1, 2 * HD_A), lambda b, h, i: (0, 0))
        lam = lams[li].reshape(1)
        diff_body = functools.partial(_diff_attn_body, out_scale=1.0 - lam_inits[li])
        za3 = za_ctx.reshape(batch, seq, 2 * BR_W)
        zv3 = zv.reshape(m // seq, seq, BR_W)
        oa = _attn_call(functools.partial(diff_body, q_scale=HD_A ** -0.5), (za3, za3, zv3),
                        (lambda b, h, i: (b, i, h), lambda b, h, i: (b, 0, H_A + h), lambda b, h, i: (b, 0, h)),
                        nb=batch, nh=H_A, nq=seq, nk=seq, tq=seq, dq=2 * HD_A, dk=2 * HD_A, dv=2 * HD_A,
                        out_rows=m, out_cols=BR_W, out_row0=0, prev_out=None,
                        extra_in=(g_sub,), extra_specs=(g_spec,), smem_in=(lam,), name="diff_attn_ctx")
        ka_all = jnp.concatenate([ka_lat.reshape(dec_batch, dec_seq, BR_W),
                                  cache_diff_k[:, li].reshape(dec_batch, past, BR_W).astype(BF16)], axis=1)
        va_all = jnp.concatenate([zv[m_ctx:].reshape(dec_batch, dec_seq, BR_W).astype(BF16),
                                  cache_diff_v[:, li].reshape(dec_batch, past, BR_W).astype(BF16)], axis=1)
        oa = _attn_call(functools.partial(diff_body, q_scale=None),
                        (qa_lat.reshape(dec_batch, dec_seq, BR_W), ka_all, va_all),
                        (lambda b, h, i: (b, i, h), lambda b, h, i: (b, 0, h), lambda b, h, i: (b, 0, h)),
                        nb=dec_batch, nh=H_A, nq=dec_seq, nk=nk_lat, tq=tq_lat, dq=2 * HD_A, dk=2 * HD_A, dv=2 * HD_A,
                        out_rows=m, out_cols=BR_W, out_row0=m_ctx, prev_out=oa,
                        extra_in=(g_sub,), extra_specs=(g_spec,), smem_in=(lam,), name="diff_attn_lat")

        cqn, ckvn, ckr = _mla_prep(zm, kr, mla_q_norm_g[li].reshape(1, q_lora), mla_kv_norm_g[li].reshape(1, kv_lora),
                                   tabs_b, q_lora=q_lora, kv_lora=kv_lora, m_ctx=m_ctx, seg_len=dec_seq)
        hb2 = H_B * 2 * LANES
        q_scale_b = (NOPE_B + ROPE_B) ** -0.5
        qb_ctx = mm(cqn, w_uqp, rows=m_ctx, row0=0, col0=0, ncols=hb2, wsel=li, out_dtype=BF16, tn=1024, scale=q_scale_b, name="mla_q_ctx")
        qb_lat = mm(cqn, w_uqp, rows=m_lat, row0=m_ctx, col0=0, ncols=hb2, wsel=li, out_dtype=BF16, tn=1024, scale=q_scale_b,
                    rope=(tabs_b, ROPE_B // 4, (False, True)), name="mla_q_lat")
        cache_ckr = jnp.concatenate([cache_mla_ckv[:, li], cache_mla_krope[:, li],
                                     jnp.zeros((dec_batch, past, LANES - ROPE_B), F32)], axis=-1).astype(BF16)
        ckr_lat = jnp.concatenate([ckr[m_ctx:].reshape(dec_batch, dec_seq, kv_lora + LANES), cache_ckr], axis=1)
        ckr_lat = ckr_lat.reshape(dec_batch * nk_lat, kv_lora + LANES)
        tm_kv = _pick(math.gcd(m_ctx, dec_batch * nk_lat), 1024)
        kb_ctx = _matmul(ckr, wk_aug, rows=m_ctx, row0=0, col0=0, ncols=hb2, wsel=li, out_dtype=BF16, tm=tm_kv, tn=1024, name="mla_k_ctx")
        vb_ctx = _matmul(ckr, wv_aug, rows=m_ctx, row0=0, col0=0, ncols=H_B * V_B, wsel=li, out_dtype=BF16, tm=tm_kv, tn=1024, name="mla_v_ctx")
        kb_lat = _matmul(ckr_lat, wk_aug, rows=dec_batch * nk_lat, row0=0, col0=0, ncols=hb2, wsel=li, out_dtype=BF16, tm=tm_kv, tn=1024, name="mla_k_lat")
        vb_lat = _matmul(ckr_lat, wv_aug, rows=dec_batch * nk_lat, row0=0, col0=0, ncols=H_B * V_B, wsel=li, out_dtype=BF16, tm=tm_kv, tn=1024, name="mla_v_lat")
        sm_body = functools.partial(_softmax_attn_body, q_scale=None)
        bmaps = (lambda b, h, i: (b, i, h), lambda b, h, i: (b, 0, h), lambda b, h, i: (b, 0, h))
        ob = _attn_call(sm_body, (qb_ctx.reshape(batch, seq, hb2), kb_ctx.reshape(batch, seq, hb2), vb_ctx.reshape(batch, seq, H_B * V_B)),
                        bmaps, nb=batch, nh=H_B, nq=seq, nk=seq, tq=seq, dq=2 * LANES, dk=2 * LANES, dv=V_B,
                        out_rows=m, out_cols=BR_W, out_row0=0, prev_out=None, name="mla_attn_ctx")
        ob = _attn_call(sm_body, (qb_lat.reshape(dec_batch, dec_seq, hb2), kb_lat.reshape(dec_batch, nk_lat, hb2),
                                  vb_lat.reshape(dec_batch, nk_lat, H_B * V_B)),
                        bmaps, nb=dec_batch, nh=H_B, nq=dec_seq, nk=nk_lat, tq=tq_lat, dq=2 * LANES, dk=2 * LANES, dv=V_B,
                        out_rows=m, out_cols=BR_W, out_row0=m_ctx, prev_out=ob, name="mla_attn_lat")

        zc3 = zc.reshape(m // seq, seq, 3 * BR_W)
        oc = _attn_call(functools.partial(_softmax_attn_body, q_scale=HD_C ** -0.5), (zc3, zc3, zc3),
                        (lambda b, h, i: (b, i, h), lambda b, h, i: (b, 0, H_C + h), lambda b, h, i: (b, 0, 2 * H_C + h)),
                        nb=batch, nh=H_C, nq=seq, nk=seq, tq=seq, dq=HD_C, dk=HD_C, dv=HD_C,
                        out_rows=m, out_cols=BR_W, out_row0=0, prev_out=None, name="na_attn_ctx")
        table = na_rel_bias[li].astype(F32)
        bias = jnp.where(na["ok"][None], table[:, na["dr"], na["dc"]], MASK_VALUE)
        rpb_tok, span_tok = NA_ROWS_PER_BLOCK * GRID_W, na["span"] * GRID_W
        zl = zc.reshape(m // dec_seq, dec_seq, 3 * BR_W)
        lb0 = m_ctx // dec_seq
        assert m_ctx % dec_seq == 0
        ck4 = cache_na_k.reshape(dec_batch, depth, past, H_C * HD_C)
        cv4 = cache_na_v.reshape(dec_batch, depth, past, H_C * HD_C)
        nblk = na["nblk"]
        gs = pltpu.PrefetchScalarGridSpec(
            num_scalar_prefetch=2, grid=(dec_batch, H_C, nblk),
            in_specs=[pl.BlockSpec((None, rpb_tok, HD_C), lambda b, h, i, cl, st: (lb0 + b, i, h)),
                      pl.BlockSpec((None, dec_seq, HD_C), lambda b, h, i, cl, st: (lb0 + b, 0, H_C + h)),
                      pl.BlockSpec((None, dec_seq, HD_C), lambda b, h, i, cl, st: (lb0 + b, 0, 2 * H_C + h)),
                      pl.BlockSpec((None, None, past, HD_C), lambda b, h, i, cl, st: (b, li, 0, h)),
                      pl.BlockSpec((None, None, past, HD_C), lambda b, h, i, cl, st: (b, li, 0, h)),
                      pl.BlockSpec((None, None, rpb_tok, span_tok), lambda b, h, i, cl, st: (h, cl[i], 0, 0)),
                      pl.BlockSpec(memory_space=pl.ANY)],
            out_specs=pl.BlockSpec((rpb_tok, HD_C), lambda b, h, i, cl, st: (m_ctx // rpb_tok + b * nblk + i, h)))
        oc = pl.pallas_call(
            functools.partial(_na_body, q_scale=HD_C ** -0.5, span_tok=span_tok),
            grid_spec=gs, out_shape=jax.ShapeDtypeStruct((m, BR_W), BF16),
            input_output_aliases={8: 0},
            compiler_params=_cparams(("arbitrary", "arbitrary", "arbitrary"),
                                     4 * dec_seq * HD_C * 4 + 10 * rpb_tok * span_tok * 4),
            name="na_attn_lat",
        )(jnp.asarray(na["cls"]), jnp.asarray(na["blk_start"], jnp.int32), zl, zl, zl, ck4, cv4, bias, oc)

        merged = _merge(oa, ob, oc, gates, w_branch_b, li)
        y = mm(merged, w_out_b, rows=m, row0=0, col0=0, ncols=d, wsel=li, out_dtype=F32, tn=_pick(d, 1024, 256), name="w_out")

        states[0].append(za_ctx[:, BR_W:].reshape(batch, seq, H_A, 2 * HD_A))
        states[1].append(zv[:m_ctx].reshape(batch, seq, H_A, 2 * HD_A))
        states[2].append(ckvn[:m_ctx].reshape(batch, seq, kv_lora))
        states[3].append(kr[:m_ctx, :ROPE_B].reshape(batch, seq, ROPE_B))
        states[4].append(zc[:m_ctx, BR_W:2 * BR_W].reshape(batch, seq, H_C, HD_C))
        states[5].append(zc[:m_ctx, 2 * BR_W:].reshape(batch, seq, H_C, HD_C))

        j = li // 2
        if li % 2 == 0:
            x, hff = rn(x, y, mods, norm_g4, li_post=li, k_gate=2, li_pre=li, k_mod=3, want_x=True, want_h=True)
            tm_f = _pick(m, 512)
            nt = m // tm_f
            yff = _grouped_ffn(hff, ffn_w1b[j][None], ffn_w3b[j][None], ffn_w2b[j][None],
                               jnp.zeros((nt,), jnp.int32), jnp.arange(nt, dtype=jnp.int32),
                               jnp.full((1,), nt, jnp.int32), tm_f)
        else:
            x, hff, logits = rn(x, y, mods, norm_g4, li_post=li, k_gate=2, li_pre=li, k_mod=3, want_x=True, want_h=True,
                                router=router_p[j])
            tm_e = _pick(TOP_K * m, 512)
            row_token, dest, top_w, tile_expert, tile_src, n_used = _moe_plan(logits[:, :N_EXPERTS], tm_e)
            xs = jnp.take(hff, row_token, axis=0)
            ys = _grouped_ffn(xs, moe_w1b[j], moe_w3b[j], moe_w2b[j], tile_expert, tile_src, n_used, tm_e)
            yff = (top_w[:, 0:1] * jnp.take(ys, dest[:, 0], axis=0)
                   + top_w[:, 1:2] * jnp.take(ys, dest[:, 1], axis=0))
        if li + 1 < depth:
            x, hmix = rn(x, yff, mods, norm_g4, li_post=li, k_gate=5, li_pre=li + 1, k_mod=0, want_x=True, want_h=True)
        else:
            (x,) = rn(x, yff, mods, norm_g4, li_post=li, k_gate=5, li_pre=li, k_mod=0, want_x=True, want_h=False)

    y_prompt = x[:m_ctx].reshape(batch, seq, d)
    y_sample = x[m_ctx:].reshape(dec_batch, dec_seq, d)
    return (y_prompt, y_sample) + tuple(jnp.stack(s, axis=1) for s in states)
```

```python
import functools
import math

import numpy as np
import jax
import jax.numpy as jnp
from jax import lax
from jax.experimental import pallas as pl
from jax.experimental.pallas import tpu as pltpu

GRID_W = 64
RMS_EPS = 1e-6
ROPE_THETA = 10000.0
H_A, HD_A = 4, 128
H_B, NOPE_B, ROPE_B, V_B = 8, 128, 64, 128
H_C, HD_C = 8, 128
NA_KR_MAX, NA_KC = 8, 16
N_BRANCH = 3
N_EXPERTS = 8
TOP_K = 2
BR_W = 1024

LANES = 128
V7X_VMEM_BYTES = 64 * 1024 * 1024
VMEM_CAP_BYTES = V7X_VMEM_BYTES - 8 * 1024 * 1024
NA_ROWS_PER_BLOCK = 4
MASK_VALUE = -1e30
LOG2E = math.log2(math.e)

F32 = jnp.float32
BF16 = jnp.bfloat16
HIGHEST = lax.Precision.HIGHEST


def _pick(n, pref, mult=8):
    t = min(pref, n)
    t -= t % mult
    while t > mult and n % t:
        t -= mult
    assert t > 0 and n % t == 0, (n, pref, mult)
    return t


def _cparams(sems, vmem_est):
    limit = int(min(max(vmem_est, 16 * 1024 * 1024), VMEM_CAP_BYTES))
    return pltpu.CompilerParams(dimension_semantics=sems, vmem_limit_bytes=limit)


def _rms(v):
    return v * lax.rsqrt(jnp.mean(v * v, axis=-1, keepdims=True) + RMS_EPS)


def _rope128(a, cos, sa, sb, shift):
    return a * cos + pltpu.roll(a, LANES - shift, 1) * sa + pltpu.roll(a, shift, 1) * sb


def _ada_body(c_ref, w_ref, b_ref, o_ref):
    c = c_ref[...]
    s = (c * jax.nn.sigmoid(c)).astype(BF16)
    o_ref[...] = jnp.dot(s, w_ref[...].astype(BF16), preferred_element_type=F32) + b_ref[...]


def _ada_mod(cvec, ada_w, ada_b):
    depth, d, n = ada_w.shape
    r = cvec.shape[0]
    tn = _pick(n, 1024, LANES)
    return pl.pallas_call(
        _ada_body,
        grid=(depth, n // tn),
        in_specs=[pl.BlockSpec((r, d), lambda l, j: (0, 0)),
                  pl.BlockSpec((None, d, tn), lambda l, j: (l, 0, j)),
                  pl.BlockSpec((None, 1, tn), lambda l, j: (l, 0, j))],
        out_specs=pl.BlockSpec((None, r, tn), lambda l, j: (l, 0, j)),
        out_shape=jax.ShapeDtypeStruct((depth, r, n), F32),
        compiler_params=_cparams(("arbitrary", "arbitrary"), 3 * d * tn * 4 + d * tn * 2),
        name="ada_mod",
    )(cvec, ada_w, ada_b.reshape(depth, 1, n))


def _resid_norm_body(*refs, n_ctx_tiles, x_split, has_y, want_x, out_split, want_h, has_router):
    it = iter(refs)
    i = pl.program_id(0)
    if x_split:
        xa_ref, xb_ref = next(it), next(it)
        x = jnp.where(i < n_ctx_tiles, xa_ref[...], xb_ref[...])
    else:
        x = next(it)[...]
    if has_y:
        y_ref, gpost_ref, gate_ref = next(it), next(it), next(it)
        x = x + gate_ref[...] * (_rms(y_ref[...].astype(F32)) * gpost_ref[...])
    if want_h:
        gpre_ref, sc_ref, sh_ref = next(it), next(it), next(it)
    if has_router:
        r_ref = next(it)
    if want_x and out_split:
        xa_o, xb_o = next(it), next(it)

        @pl.when(i < n_ctx_tiles)
        def _():
            xa_o[...] = x

        @pl.when(i >= n_ctx_tiles)
        def _():
            xb_o[...] = x
    elif want_x:
        next(it)[...] = x
    if want_h:
        h = _rms(x) * gpre_ref[...]
        h = h * (1.0 + sc_ref[...]) + sh_ref[...]
        next(it)[...] = h.astype(BF16)
        if has_router:
            next(it)[...] = jnp.dot(h, r_ref[...], precision=HIGHEST, preferred_element_type=F32)


def _resid_norm(x, y, mods, norm_g, *, li_post, k_gate, li_pre, k_mod, m_ctx, seg_len,
                want_x, want_h, out_split=False, router=None):
    x_split = isinstance(x, tuple)
    d = x[0].shape[1] if x_split else x.shape[1]
    m = m_ctx + x[1].shape[0] if x_split else x.shape[0]
    tm = _pick(math.gcd(m_ctx, seg_len), 256)
    nct = m_ctx // tm

    def seg(i):
        return jnp.maximum((i * tm - m_ctx) // seg_len + 1, 0)

    row = pl.BlockSpec((tm, d), lambda i: (i, 0))
    row_a = pl.BlockSpec((tm, d), lambda i: (jnp.minimum(i, nct - 1), 0))
    row_b = pl.BlockSpec((tm, d), lambda i: (jnp.maximum(i - nct, 0), 0))
    in_specs, args = ([row_a, row_b], list(x)) if x_split else ([row], [x])
    if y is not None:
        k_post = 1 if k_gate == 2 else 3
        in_specs += [row,
                     pl.BlockSpec((None, None, 1, d), lambda i: (li_post, k_post, 0, 0)),
                     pl.BlockSpec((None, None, None, 1, d), lambda i: (li_post, seg(i), k_gate, 0, 0))]
        args += [y, norm_g, mods]
    if want_h:
        k_norm = 0 if k_mod == 0 else 2
        in_specs += [pl.BlockSpec((None, None, 1, d), lambda i: (li_pre, k_norm, 0, 0)),
                     pl.BlockSpec((None, None, None, 1, d), lambda i: (li_pre, seg(i), k_mod + 1, 0, 0)),
                     pl.BlockSpec((None, None, None, 1, d), lambda i: (li_pre, seg(i), k_mod, 0, 0))]
        args += [norm_g, mods, mods]
    if router is not None:
        in_specs.append(pl.BlockSpec(router.shape, lambda i: (0, 0)))
        args.append(router)
    out_specs, out_shape = [], []
    if want_x and out_split:
        out_specs += [row_a, row_b]
        out_shape += [jax.ShapeDtypeStruct((m_ctx, d), F32), jax.ShapeDtypeStruct((m - m_ctx, d), F32)]
    elif want_x:
        out_specs.append(row)
        out_shape.append(jax.ShapeDtypeStruct((m, d), F32))
    if want_h:
        out_specs.append(row)
        out_shape.append(jax.ShapeDtypeStruct((m, d), BF16))
        if router is not None:
            out_specs.append(pl.BlockSpec((tm, router.shape[1]), lambda i: (i, 0)))
            out_shape.append(jax.ShapeDtypeStruct((m, router.shape[1]), F32))
    body = functools.partial(_resid_norm_body, n_ctx_tiles=nct, x_split=x_split, has_y=y is not None,
                             want_x=want_x, out_split=out_split, want_h=want_h,
                             has_router=router is not None)
    return pl.pallas_call(
        body, grid=(m // tm,), in_specs=in_specs, out_specs=out_specs, out_shape=out_shape,
        compiler_params=_cparams(("arbitrary",), 14 * tm * d * 4 + d * LANES * 8),
        name="resid_norm",
    )(*args)


def _matmul_body(*refs, tn, scale, act, rope_shift, rope_pattern, seq_split):
    x_ref, w_ref = refs[0], refs[1]
    o_ref = refs[-1]
    acc = jnp.dot(x_ref[...], w_ref[...], preferred_element_type=F32)
    if scale is not None:
        acc = acc * scale
    if act == "sigmoid":
        acc = jax.nn.sigmoid(acc)
    if seq_split is not None:
        o_ref[...] = acc.reshape(acc.shape[0] // seq_split, seq_split, tn).astype(o_ref.dtype)
        return
    if rope_shift is None:
        o_ref[...] = acc.astype(o_ref.dtype)
        return
    cos, sa, sb = refs[2][...], refs[3][...], refs[4][...]
    for g in range(tn // LANES):
        a = acc[:, g * LANES:(g + 1) * LANES]
        if rope_pattern[g % len(rope_pattern)]:
            a = _rope128(a, cos, sa, sb, rope_shift)
        o_ref[:, g * LANES:(g + 1) * LANES] = a.astype(o_ref.dtype)


def _matmul(x, w, *, rows, row0, col0, ncols, wsel, out_dtype, tm, tn, scale=None, act=None,
            rope=None, state=None, name="matmul"):
    k = x.shape[1]
    assert row0 % tm == 0 and rows % tm == 0 and col0 % tn == 0 and ncols % tn == 0, (row0, rows, col0, ncols, tm, tn)
    rb0, cb0 = row0 // tm, col0 // tn
    in_specs = [pl.BlockSpec((tm, k), lambda j, i: (rb0 + i, 0)),
                pl.BlockSpec((None, k, tn), lambda j, i: (wsel, 0, cb0 + j))]
    args = [x, w]
    rope_shift = rope_pattern = None
    if rope is not None:
        tabs, rope_shift, rope_pattern = rope
        nper = tabs[0].shape[0] // tm
        assert tabs[0].shape[0] % tm == 0
        for t in tabs:
            in_specs.append(pl.BlockSpec((tm, LANES), lambda j, i: (i % nper, 0)))
            args.append(t)
    aliases, seq_split = {}, None
    out_spec = pl.BlockSpec((tm, tn), lambda j, i: (i, j))
    out_shape = jax.ShapeDtypeStruct((rows, ncols), out_dtype)
    if state is not None:
        prev, li, depth, seq_split = state
        assert tm % seq_split == 0 and rope is None
        out_spec = pl.BlockSpec((tm // seq_split, None, seq_split, tn), lambda j, i: (i, li, 0, j))
        out_shape = jax.ShapeDtypeStruct((rows // seq_split, depth, seq_split, ncols), out_dtype)
        if prev is not None:
            in_specs.append(pl.BlockSpec(memory_space=pl.ANY))
            aliases = {len(args): 0}
            args.append(prev)
    osz = jnp.dtype(out_dtype).itemsize
    est = 2 * (tm * k * 2 + k * tn * 2 + tm * tn * osz) + 3 * tm * tn * 4 + 6 * tm * LANES * 4
    body = functools.partial(_matmul_body, tn=tn, scale=scale, act=act, rope_shift=rope_shift,
                             rope_pattern=rope_pattern, seq_split=seq_split)
    return pl.pallas_call(
        body, grid=(ncols // tn, rows // tm), in_specs=in_specs, out_specs=out_spec, out_shape=out_shape,
        input_output_aliases=aliases,
        compiler_params=_cparams(("arbitrary", "arbitrary"), est),
        name=name,
    )(*args)


def _mla_prep_body(zm_ref, kr_ref, qg_ref, kg_ref, cos_ref, sa_ref, sb_ref, *rest, q_lora, kv_lora,
                   n_ctx_tiles, seq):
    cq_o, ckv_o, ckr_o = rest[-3:]
    z = zm_ref[...]
    cq_o[...] = (_rms(z[:, :q_lora]) * qg_ref[...]).astype(BF16)
    ckv = _rms(z[:, q_lora:]) * kg_ref[...]
    ckr_o[:, :kv_lora] = ckv.astype(BF16)
    i = pl.program_id(0)

    @pl.when(i < n_ctx_tiles)
    def _():
        ckv_o[...] = ckv.reshape(ckv.shape[0] // seq, seq, kv_lora)
        ckr_o[:, kv_lora:] = kr_ref[...].astype(BF16)

    @pl.when(i >= n_ctx_tiles)
    def _():
        ckr_o[:, kv_lora:] = _rope128(kr_ref[...], cos_ref[...], sa_ref[...], sb_ref[...],
                                      ROPE_B // 4).astype(BF16)


def _mla_prep(zm, kr, qg, kg, tabs, state, *, q_lora, kv_lora, m_ctx, seg_len):
    prev, li, depth, seq = state
    m = zm.shape[0]
    tm = _pick(math.gcd(m_ctx, seg_len), 512)
    assert tm % seq == 0
    nct, nper = m_ctx // tm, seg_len // tm
    tab_spec = pl.BlockSpec((tm, LANES), lambda i: (jnp.maximum(i - nct, 0) % nper, 0))
    in_specs = [pl.BlockSpec((tm, q_lora + kv_lora), lambda i: (i, 0)),
                pl.BlockSpec((tm, LANES), lambda i: (i, 0)),
                pl.BlockSpec((1, q_lora), lambda i: (0, 0)),
                pl.BlockSpec((1, kv_lora), lambda i: (0, 0)),
                tab_spec, tab_spec, tab_spec]
    args = [zm, kr, qg, kg, *tabs]
    aliases = {}
    if prev is not None:
        in_specs.append(pl.BlockSpec(memory_space=pl.ANY))
        aliases = {len(args): 1}
        args.append(prev)
    body = functools.partial(_mla_prep_body, q_lora=q_lora, kv_lora=kv_lora, n_ctx_tiles=nct, seq=seq)
    return pl.pallas_call(
        body, grid=(m // tm,), in_specs=in_specs,
        out_specs=[pl.BlockSpec((tm, q_lora), lambda i: (i, 0)),
                   pl.BlockSpec((tm // seq, None, seq, kv_lora), lambda i: (jnp.minimum(i, nct - 1), li, 0, 0)),
                   pl.BlockSpec((tm, kv_lora + LANES), lambda i: (i, 0))],
        out_shape=[jax.ShapeDtypeStruct((m, q_lora), BF16),
                   jax.ShapeDtypeStruct((m_ctx // seq, depth, seq, kv_lora), F32),
                   jax.ShapeDtypeStruct((m, kv_lora + LANES), BF16)],
        input_output_aliases=aliases,
        compiler_params=_cparams(("arbitrary",), 8 * tm * (q_lora + kv_lora + 4 * LANES) * 4),
        name="mla_prep",
    )(*args)


_NT = (((1,), (1,)), ((), ()))


def _softmax_parts(q, k):
    s = lax.dot_general(q, k, _NT, preferred_element_type=F32)
    p = jnp.exp2(s - jnp.max(s, axis=-1, keepdims=True))
    return p, 1.0 / jnp.sum(p, axis=-1, keepdims=True)


def _one_head(q, k, v, *, diff, lam, g, out_scale):
    k = k.astype(BF16)
    v = v.astype(BF16)
    if not diff:
        p, r = _softmax_parts(q, k)
        return jnp.dot(p.astype(BF16), v, preferred_element_type=F32) * r
    p1, r1 = _softmax_parts(q[:, :HD_A], k[:, :HD_A])
    p2, r2 = _softmax_parts(q[:, HD_A:], k[:, HD_A:])
    p = (p1 * r1 - p2 * (lam * r2)).astype(BF16)
    o = jnp.dot(p, v, preferred_element_type=F32)
    return _rms(o) * g * out_scale


def _ctx_attn_body(*refs, nb, nh, dq, dv, diff, out_scale):
    it = iter(refs)
    lam = next(it)[0] if diff else None
    q_ref, k_ref, v_ref = next(it), next(it), next(it)
    g = next(it)[...] if diff else None
    o_ref = refs[-1]
    s = q_ref.shape[1]
    for b in range(nb):
        for h in range(nh):
            o = _one_head(q_ref[b, :, h * dq:(h + 1) * dq], k_ref[b, :, h * dq:(h + 1) * dq],
                          v_ref[b, :, h * dv:(h + 1) * dv], diff=diff, lam=lam, g=g, out_scale=out_scale)
            o_ref[b * s:(b + 1) * s, h * dv:(h + 1) * dv] = o.astype(o_ref.dtype)


def _ctx_attn(q, k, v, *, li, nh, dq, dv, out_rows, diff=False, lam=None, g=None, out_scale=None, name):
    bsz, s, _ = q.shape
    nb = 2 if bsz % 2 == 0 else 1

    def kv_spec(a, width):
        if a.ndim == 4:
            return pl.BlockSpec((nb, None, s, width), lambda i: (i, li, 0, 0))
        return pl.BlockSpec((nb, s, width), lambda i: (i, 0, 0))

    in_specs, args = [], []
    if diff:
        in_specs.append(pl.BlockSpec(memory_space=pltpu.SMEM))
        args.append(lam)
    in_specs += [pl.BlockSpec((nb, s, nh * dq), lambda i: (i, 0, 0)), kv_spec(k, nh * dq), kv_spec(v, nh * dv)]
    args += [q, k, v]
    if diff:
        in_specs.append(pl.BlockSpec((1, dv), lambda i: (0, 0)))
        args.append(g)
    body = functools.partial(_ctx_attn_body, nb=nb, nh=nh, dq=dq, dv=dv, diff=diff, out_scale=out_scale)
    return pl.pallas_call(
        body, grid=(bsz // nb,), in_specs=in_specs,
        out_specs=pl.BlockSpec((nb * s, nh * dv), lambda i: (i, 0)),
        out_shape=jax.ShapeDtypeStruct((out_rows, nh * dv), BF16),
        compiler_params=_cparams(("arbitrary",), 4 * nb * s * nh * (2 * dq + dv) * 4 + 16 * s * s * 4 * nb * nh),
        name=name,
    )(*args)


def _lat_attn_body(*refs, hps, dq, dv, diff, out_scale):
    it = iter(refs)
    lam = next(it)[0] if diff else None
    q_ref, k_ref, v_ref = next(it), next(it), next(it)
    g = next(it)[...] if diff else None
    o_ref = refs[-1]
    for h in range(hps):
        o = _one_head(q_ref[:, h * dq:(h + 1) * dq], k_ref[:, h * dq:(h + 1) * dq],
                      v_ref[:, h * dv:(h + 1) * dv], diff=diff, lam=lam, g=g, out_scale=out_scale)
        o_ref[:, h * dv:(h + 1) * dv] = o.astype(o_ref.dtype)


def _lat_attn(q, k, v, prev_out, *, nh, hps, dq, dv, tq, out_row0, diff=False, lam=None, g=None,
              out_scale=None, name):
    bsz, n, _ = q.shape
    nk = k.shape[1]
    assert nh % hps == 0 and n % tq == 0 and out_row0 % tq == 0
    in_specs, args = [], []
    if diff:
        in_specs.append(pl.BlockSpec(memory_space=pltpu.SMEM))
        args.append(lam)
    in_specs += [pl.BlockSpec((None, tq, hps * dq), lambda b, h, i: (b, i, h)),
                 pl.BlockSpec((None, nk, hps * dq), lambda b, h, i: (b, 0, h)),
                 pl.BlockSpec((None, nk, hps * dv), lambda b, h, i: (b, 0, h))]
    args += [q, k, v]
    if diff:
        in_specs.append(pl.BlockSpec((1, dv), lambda b, h, i: (0, 0)))
        args.append(g)
    in_specs.append(pl.BlockSpec(memory_space=pl.ANY))
    aliases = {len(args): 0}
    args.append(prev_out)
    orb0, nqb = out_row0 // tq, n // tq
    n_chain = hps * (2 if diff else 1)
    est = 2 * (tq * hps * dq + nk * hps * (dq + dv)) * 2 + 5 * n_chain * tq * nk * 4
    body = functools.partial(_lat_attn_body, hps=hps, dq=dq, dv=dv, diff=diff, out_scale=out_scale)
    return pl.pallas_call(
        body, grid=(bsz, nh // hps, nqb), in_specs=in_specs,
        out_specs=pl.BlockSpec((tq, hps * dv), lambda b, h, i: (orb0 + b * nqb + i, h)),
        out_shape=jax.ShapeDtypeStruct(prev_out.shape, prev_out.dtype),
        input_output_aliases=aliases,
        compiler_params=_cparams(("arbitrary", "arbitrary", "arbitrary"), est),
        name=name,
    )(*args)


def _na_geometry(rows):
    rpb = NA_ROWS_PER_BLOCK
    kr = min(NA_KR_MAX, rows)
    span = rpb + kr - 1
    assert rows % rpb == 0 and rows >= span, rows
    nblk = rows // rpb
    r = np.arange(rows)
    row_start = np.clip(r - kr // 2, 0, rows - kr)
    blk_start = np.clip(rpb * np.arange(nblk) - kr // 2, 0, rows - span)
    cols = np.arange(GRID_W)
    col_start = np.clip(cols - NA_KC // 2, 0, GRID_W - NA_KC)
    dc = cols[None, :] - cols[:, None] + (NA_KC - 1)
    ok_c = (cols[None, :] >= col_start[:, None]) & (cols[None, :] < col_start[:, None] + NA_KC)
    sigs, cls_of_blk, dr_l, ok_l = {}, [], [], []
    for b in range(nblk):
        qr = rpb * b + np.arange(rpb)
        key_row = blk_start[b] + np.arange(span)
        dr = key_row[None, :] - qr[:, None] + (NA_KR_MAX - 1)
        ok_r = (key_row[None, :] >= row_start[qr][:, None]) & (key_row[None, :] < row_start[qr][:, None] + kr)
        sig = (dr.tobytes(), ok_r.tobytes())
        if sig not in sigs:
            sigs[sig] = len(sigs)
            dr_l.append(np.clip(dr, 0, 2 * NA_KR_MAX - 2))
            ok = ok_r[:, None, :, None] & ok_c[None, :, None, :]
            ok_l.append(ok.reshape(rpb * GRID_W, span * GRID_W))
        cls_of_blk.append(sigs[sig])
    return dict(span=span, nblk=nblk, blk_start=blk_start.astype(np.int32), cls=np.asarray(cls_of_blk, np.int32),
                dr=np.stack(dr_l), dc=np.clip(dc, 0, 2 * NA_KC - 2), ok=np.stack(ok_l))


def _na_bias(tables, na):
    depth, nh, n_dr, n_dc = tables.shape
    ncls, rpb, span = na["dr"].shape
    oh_r = jax.nn.one_hot(na["dr"].reshape(-1), n_dr, dtype=F32)
    oh_c = jax.nn.one_hot(na["dc"].reshape(-1), n_dc, dtype=F32).T
    t = jnp.einsum("xr,lhrc->lhxc", oh_r, tables.astype(F32), precision=HIGHEST)
    t = jnp.einsum("lhxc,cy->lhxy", t, oh_c, precision=HIGHEST)
    t = t.reshape(depth, nh, ncls, rpb, span, GRID_W, GRID_W).transpose(0, 1, 2, 3, 5, 4, 6)
    t = t.reshape(depth, nh, ncls, rpb * GRID_W, span * GRID_W)
    return jnp.where(na["ok"][None, None], t, MASK_VALUE)


def _na_body(cls_ref, start_ref, q_ref, k_ref, v_ref, kc_ref, vc_ref, bias_ref, prev_ref, o_ref, *, span_tok):
    del cls_ref, prev_ref
    blk = pl.program_id(2)
    start = pl.multiple_of(start_ref[blk] * GRID_W, GRID_W)
    q = q_ref[...]
    kl = k_ref[pl.ds(start, span_tok), :]
    vl = v_ref[pl.ds(start, span_tok), :]
    s_loc = lax.dot_general(q, kl, _NT, preferred_element_type=F32) + bias_ref[...]
    s_ctx = lax.dot_general(q, kc_ref[...].astype(BF16), _NT, preferred_element_type=F32)
    mx = jnp.maximum(jnp.max(s_loc, axis=-1, keepdims=True), jnp.max(s_ctx, axis=-1, keepdims=True))
    p_loc = jnp.exp2(s_loc - mx)
    p_ctx = jnp.exp2(s_ctx - mx)
    r = 1.0 / (jnp.sum(p_loc, axis=-1, keepdims=True) + jnp.sum(p_ctx, axis=-1, keepdims=True))
    o = jnp.dot(p_loc.astype(BF16), vl, preferred_element_type=F32)
    o = o + jnp.dot(p_ctx.astype(BF16), vc_ref[...].astype(BF16), preferred_element_type=F32)
    o_ref[...] = (o * r).astype(o_ref.dtype)


def _na_lat_attn(q, kv, cache_k, cache_v, bias, na, prev_out, *, li, out_row0):
    bsz, n, _ = q.shape
    past = cache_k.shape[2]
    rpb_tok, span_tok, nblk = NA_ROWS_PER_BLOCK * GRID_W, na["span"] * GRID_W, na["nblk"]
    gs = pltpu.PrefetchScalarGridSpec(
        num_scalar_prefetch=2, grid=(bsz, H_C, nblk),
        in_specs=[pl.BlockSpec((None, rpb_tok, HD_C), lambda b, h, i, cl, st: (b, i, h)),
                  pl.BlockSpec((None, n, HD_C), lambda b, h, i, cl, st: (b, 0, h)),
                  pl.BlockSpec((None, n, HD_C), lambda b, h, i, cl, st: (b, 0, H_C + h)),
                  pl.BlockSpec((None, None, past, HD_C), lambda b, h, i, cl, st: (b, li, 0, h)),
                  pl.BlockSpec((None, None, past, HD_C), lambda b, h, i, cl, st: (b, li, 0, h)),
                  pl.BlockSpec((None, None, None, rpb_tok, span_tok), lambda b, h, i, cl, st: (li, h, cl[i], 0, 0)),
                  pl.BlockSpec(memory_space=pl.ANY)],
        out_specs=pl.BlockSpec((rpb_tok, HD_C), lambda b, h, i, cl, st: (out_row0 // rpb_tok + b * nblk + i, h)))
    return pl.pallas_call(
        functools.partial(_na_body, span_tok=span_tok),
        grid_spec=gs, out_shape=jax.ShapeDtypeStruct(prev_out.shape, prev_out.dtype),
        input_output_aliases={8: 0},
        compiler_params=_cparams(("arbitrary", "arbitrary", "arbitrary"),
                                 8 * n * HD_C * 2 + 10 * rpb_tok * span_tok * 4),
        name="na_attn_lat",
    )(jnp.asarray(na["cls"]), jnp.asarray(na["blk_start"]), q, kv, kv, cache_k, cache_v, bias, prev_out)


def _merge_body(a_ref, b_ref, c_ref, ga_ref, gb_ref, gc_ref, wa_ref, wb_ref, wc_ref, o_ref):
    acc = ga_ref[...].astype(F32) * jnp.dot(a_ref[...], wa_ref[...], preferred_element_type=F32)
    acc += gb_ref[...].astype(F32) * jnp.dot(b_ref[...], wb_ref[...], preferred_element_type=F32)
    acc += gc_ref[...].astype(F32) * jnp.dot(c_ref[...], wc_ref[...], preferred_element_type=F32)
    o_ref[...] = acc.astype(o_ref.dtype)


def _merge(oa, ob, oc, gates, wb, li):
    m, kb = oa.shape
    d = wb.shape[-1]
    tm, tn = _pick(m, 1024), _pick(d, 1024, 256)
    nj = d // tn
    br = pl.BlockSpec((tm, kb), lambda j, i: (i, 0))
    gspec = [pl.BlockSpec((tm, tn), functools.partial(lambda j, i, s: (i, s * nj + j), s=s)) for s in range(N_BRANCH)]
    wspec = [pl.BlockSpec((None, None, kb, tn), functools.partial(lambda j, i, s: (li, s, 0, j), s=s)) for s in range(N_BRANCH)]
    est = 2 * (3 * tm * kb * 2 + 3 * tm * tn * 2 + 3 * kb * tn * 2 + tm * tn * 2) + 4 * tm * tn * 4
    return pl.pallas_call(
        _merge_body, grid=(nj, m // tm), in_specs=[br, br, br] + gspec + wspec,
        out_specs=pl.BlockSpec((tm, tn), lambda j, i: (i, j)),
        out_shape=jax.ShapeDtypeStruct((m, d), BF16),
        compiler_params=_cparams(("arbitrary", "arbitrary"), est),
        name="branch_merge",
    )(oa, ob, oc, gates, gates, gates, wb, wb, wb)


def _ffn1_body(te_ref, ts_ref, nu_ref, x_ref, w1_ref, w3_ref, o_ref):
    del te_ref, ts_ref

    @pl.when(pl.program_id(1) < nu_ref[0])
    def _():
        x = x_ref[...]
        h1 = jnp.dot(x, w1_ref[...], preferred_element_type=F32)
        h3 = jnp.dot(x, w3_ref[...], preferred_element_type=F32)
        o_ref[...] = (h1 * jax.nn.sigmoid(h1) * h3).astype(o_ref.dtype)


def _ffn2_body(te_ref, ts_ref, nu_ref, h_ref, w2_ref, o_ref):
    del te_ref, ts_ref

    @pl.when(pl.program_id(1) < nu_ref[0])
    def _():
        o_ref[...] = jnp.dot(h_ref[...], w2_ref[...], preferred_element_type=F32).astype(o_ref.dtype)


def _grouped_ffn(xs, w1, w3, w2, tile_expert, tile_src, n_used, tm):
    r, d = xs.shape
    f = w1.shape[-1]
    nt = r // tm
    tf = _pick(f, 1408, LANES)
    tn = _pick(d, 1024, 256)
    gs1 = pltpu.PrefetchScalarGridSpec(
        num_scalar_prefetch=3, grid=(f // tf, nt),
        in_specs=[pl.BlockSpec((tm, d), lambda j, t, te, ts, nu: (ts[t], 0)),
                  pl.BlockSpec((None, d, tf), lambda j, t, te, ts, nu: (te[t], 0, j)),
                  pl.BlockSpec((None, d, tf), lambda j, t, te, ts, nu: (te[t], 0, j))],
        out_specs=pl.BlockSpec((tm, tf), lambda j, t, te, ts, nu: (ts[t], j)))
    hid = pl.pallas_call(
        _ffn1_body, grid_spec=gs1, out_shape=jax.ShapeDtypeStruct((r, f), BF16),
        compiler_params=_cparams(("arbitrary", "arbitrary"),
                                 2 * (tm * d * 2 + 2 * d * tf * 2 + tm * tf * 2) + 4 * tm * tf * 4),
        name="ffn_up",
    )(tile_expert, tile_src, n_used, xs, w1, w3)
    gs2 = pltpu.PrefetchScalarGridSpec(
        num_scalar_prefetch=3, grid=(d // tn, nt),
        in_specs=[pl.BlockSpec((tm, f), lambda j, t, te, ts, nu: (ts[t], 0)),
                  pl.BlockSpec((None, f, tn), lambda j, t, te, ts, nu: (te[t], 0, j))],
        out_specs=pl.BlockSpec((tm, tn), lambda j, t, te, ts, nu: (ts[t], j)))
    return pl.pallas_call(
        _ffn2_body, grid_spec=gs2, out_shape=jax.ShapeDtypeStruct((r, d), F32),
        compiler_params=_cparams(("arbitrary", "arbitrary"),
                                 2 * (tm * f * 2 + f * tn * 2 + tm * tn * 4) + 2 * tm * tn * 4),
        name="ffn_down",
    )(tile_expert, tile_src, n_used, hid, w2)


def _moe_plan(logits, tm):
    m = logits.shape[0]
    top_v, top_i = lax.top_k(logits, TOP_K)
    top_w = jax.nn.softmax(top_v, axis=-1)
    flat_e = top_i.reshape(-1)
    onehot = (flat_e[:, None] == jnp.arange(N_EXPERTS)[None, :]).astype(jnp.int32)
    csum = jnp.cumsum(onehot, axis=0)
    counts = csum[-1]
    rank = jnp.take_along_axis(csum, flat_e[:, None], axis=1)[:, 0] - 1
    padded = ((counts + tm - 1) // tm) * tm
    ends = jnp.cumsum(padded)
    offs = ends - padded
    dest = offs[flat_e] + rank
    n_rows = TOP_K * m + N_EXPERTS * tm
    nt = n_rows // tm
    row_token = jnp.zeros((n_rows,), jnp.int32).at[dest].set(jnp.arange(TOP_K * m, dtype=jnp.int32) // TOP_K)
    n_used = (ends[-1] // tm).astype(jnp.int32)
    tile_src = jnp.minimum(jnp.arange(nt, dtype=jnp.int32), n_used - 1)
    tile_expert = jnp.sum((tile_src[:, None] * tm >= ends[None, :]).astype(jnp.int32), axis=1)
    tile_expert = jnp.minimum(tile_expert, N_EXPERTS - 1).astype(jnp.int32)
    return row_token, dest.reshape(m, TOP_K), top_w, tile_expert, tile_src, n_used.reshape(1)


def _rows(a, idx):
    return a.at[idx].get(mode="promise_in_bounds")


def _rope_tables(n_tok, dim):
    half = dim // 2
    t = jnp.arange(n_tok)
    row = (t // GRID_W).astype(F32)
    col = (t % GRID_W).astype(F32)
    inv = ROPE_THETA ** (-jnp.arange(0, half, 2, dtype=F32) / half)
    ar = row[:, None] * inv[None, :]
    ac = col[:, None] * inv[None, :]
    ang = jnp.concatenate([ar, ar, ac, ac], axis=-1)
    cos, sin = jnp.cos(ang), jnp.sin(ang)
    first = (np.arange(dim) % half) < (dim // 4)
    sa = jnp.where(first[None, :], -sin, 0.0)
    sb = jnp.where(first[None, :], 0.0, sin)
    pad = LANES - dim
    if pad:
        cos = jnp.pad(cos, ((0, 0), (0, pad)), constant_values=1.0)
        sa = jnp.pad(sa, ((0, 0), (0, pad)))
        sb = jnp.pad(sb, ((0, 0), (0, pad)))
    return cos, sa, sb


def kernel(x_prompt, x_sample, cache_diff_k, cache_diff_v, cache_mla_ckv, cache_mla_krope, cache_na_k, cache_na_v, c, c_ctx, norm_g, ada_w, ada_b, w_in, diff_lambda, diff_subln_g, mla_q_norm_g, mla_kv_norm_g, mla_w_uq, mla_w_ukv, na_rel_bias, w_branch, w_out, ffn_w1, ffn_w3, ffn_w2, moe_router, moe_w1, moe_w3, moe_w2):
    batch, seq, d = x_prompt.shape
    dec_batch, dec_seq, _ = x_sample.shape
    depth = norm_g.shape[0]
    past = cache_diff_k.shape[2]
    q_lora, kv_lora = mla_q_norm_g.shape[1], mla_kv_norm_g.shape[1]
    m_ctx, m_lat = batch * seq, dec_batch * dec_seq
    m = m_ctx + m_lat
    nk_lat = dec_seq + past
    assert dec_seq % GRID_W == 0 and kv_lora % LANES == 0 and q_lora % LANES == 0

    a3 = 3 * BR_W
    c_kr = a3 + q_lora + kv_lora
    c_na = c_kr + ROPE_B
    c_gate = c_na + 3 * BR_W
    w_inp = jnp.concatenate(
        [w_in[..., :a3], w_in[..., c_na:c_gate], w_in[..., c_gate:], w_in[..., a3:c_kr], w_in[..., c_kr:c_na],
         jnp.zeros(w_in.shape[:2] + (LANES - ROPE_B,), w_in.dtype)], axis=-1).astype(BF16)
    col_c, col_g = a3, 2 * a3
    col_m = col_g + N_BRANCH * d
    col_kr = col_m + q_lora + kv_lora
    wq = mla_w_uq.reshape(depth, q_lora, H_B, NOPE_B + ROPE_B)
    w_uqp = jnp.pad(wq, ((0, 0), (0, 0), (0, 0), (0, 2 * LANES - NOPE_B - ROPE_B))).reshape(depth, q_lora, H_B * 2 * LANES).astype(BF16)
    wkv = mla_w_ukv.reshape(depth, kv_lora, H_B, NOPE_B + V_B)
    wk_top = jnp.pad(wkv[..., :NOPE_B], ((0, 0), (0, 0), (0, 0), (0, 2 * LANES - NOPE_B))).reshape(depth, kv_lora, H_B * 2 * LANES)
    eye = np.zeros((LANES, H_B, 2 * LANES), np.float32)
    for r_ in range(ROPE_B):
        eye[r_, :, NOPE_B + r_] = 1.0
    wk_aug = jnp.concatenate([wk_top, jnp.broadcast_to(jnp.asarray(eye.reshape(LANES, -1)), (depth, LANES, H_B * 2 * LANES))], axis=1).astype(BF16)
    wv_aug = jnp.pad(wkv[..., NOPE_B:].reshape(depth, kv_lora, H_B * V_B), ((0, 0), (0, LANES), (0, 0))).astype(BF16)
    w_branch_b = w_branch.astype(BF16)
    w_out_b = w_out.astype(BF16)
    ffn_w1b, ffn_w3b, ffn_w2b = ffn_w1.astype(BF16), ffn_w3.astype(BF16), ffn_w2.astype(BF16)
    moe_w1b, moe_w3b, moe_w2b = moe_w1.astype(BF16), moe_w3.astype(BF16), moe_w2.astype(BF16)
    router_p = jnp.pad(moe_router, ((0, 0), (0, 0), (0, LANES - N_EXPERTS)))
    norm_g4 = norm_g.reshape(depth, 4, 1, d)

    n_seg = 1 + dec_batch
    r_pad = -(-n_seg // 8) * 8
    cvec = jnp.zeros((r_pad, d), F32).at[0].set(c_ctx).at[1:n_seg].set(c)
    mods = _ada_mod(cvec, ada_w, ada_b).reshape(depth, r_pad, 6, 1, d)

    tabs_a = _rope_tables(dec_seq, HD_A)
    tabs_b = _rope_tables(dec_seq, ROPE_B)
    na = _na_geometry(dec_seq // GRID_W)
    na_bias = _na_bias(na_rel_bias, na)
    lam_inits = [0.8 - 0.6 * math.exp(-0.3 * li) for li in range(depth)]
    lp = diff_lambda.astype(F32)
    lams = jnp.exp(jnp.sum(lp[:, 0] * lp[:, 1], axis=-1)) - jnp.exp(jnp.sum(lp[:, 2] * lp[:, 3], axis=-1)) + jnp.asarray(lam_inits, F32)

    tm_big = _pick(math.gcd(m_ctx, dec_seq), 1024)
    assert tm_big % seq == 0
    tq_lat = _pick(dec_seq, 256)
    rn = functools.partial(_resid_norm, m_ctx=m_ctx, seg_len=dec_seq)
    mm = functools.partial(_matmul, tm=tm_big)
    qs_a, qs_b, qs_c = HD_A ** -0.5 * LOG2E, (NOPE_B + ROPE_B) ** -0.5 * LOG2E, HD_C ** -0.5 * LOG2E
    hb2 = H_B * 2 * LANES
    cache_k_c = cache_na_k.reshape(dec_batch, depth, past, H_C * HD_C)
    cache_v_c = cache_na_v.reshape(dec_batch, depth, past, H_C * HD_C)

    x = (x_prompt.reshape(m_ctx, d), x_sample.reshape(m_lat, d))
    (hmix,) = rn(x, None, mods, norm_g4, li_post=0, k_gate=2, li_pre=0, k_mod=0, want_x=False, want_h=True)
    st_dk = st_dv = st_ckv = st_nk = st_nv = None
    st_kr = []

    for li in range(depth):
        def st(prev):
            return (prev, li, depth, seq)

        ctx = dict(rows=m_ctx, row0=0, wsel=li, tn=1024)
        lat = dict(rows=m_lat, row0=m_ctx, wsel=li, tn=1024)
        qa_ctx = mm(hmix, w_inp, col0=0, ncols=BR_W, out_dtype=BF16, scale=qs_a, name="in_qa_ctx", **ctx)
        st_dk = mm(hmix, w_inp, col0=BR_W, ncols=BR_W, out_dtype=F32, state=st(st_dk), name="in_ka_ctx", **ctx)
        st_dv = mm(hmix, w_inp, col0=2 * BR_W, ncols=BR_W, out_dtype=F32, state=st(st_dv), name="in_va_ctx", **ctx)
        qa_lat = mm(hmix, w_inp, col0=0, ncols=BR_W, out_dtype=BF16, scale=qs_a,
                    rope=(tabs_a, HD_A // 4, (True,)), name="in_qa_lat", **lat)
        ka_lat = mm(hmix, w_inp, col0=BR_W, ncols=BR_W, out_dtype=BF16,
                    rope=(tabs_a, HD_A // 4, (True,)), name="in_ka_lat", **lat)
        va_lat = mm(hmix, w_inp, col0=2 * BR_W, ncols=BR_W, out_dtype=BF16, name="in_va_lat", **lat)
        qc_ctx = mm(hmix, w_inp, col0=col_c, ncols=BR_W, out_dtype=BF16, scale=qs_c, name="in_qc_ctx", **ctx)
        st_nk = mm(hmix, w_inp, col0=col_c + BR_W, ncols=BR_W, out_dtype=F32, state=st(st_nk), name="in_kc_ctx", **ctx)
        st_nv = mm(hmix, w_inp, col0=col_c + 2 * BR_W, ncols=BR_W, out_dtype=F32, state=st(st_nv), name="in_vc_ctx", **ctx)
        qc_lat = mm(hmix, w_inp, col0=col_c, ncols=BR_W, out_dtype=BF16, scale=qs_c, name="in_qc_lat", **lat)
        kvc_lat = mm(hmix, w_inp, col0=col_c + BR_W, ncols=2 * BR_W, out_dtype=BF16, name="in_kvc_lat", **lat)
        gates = mm(hmix, w_inp, rows=m, row0=0, col0=col_g, ncols=N_BRANCH * d, wsel=li, out_dtype=BF16, tn=768,
                   act="sigmoid", name="in_gates")
        zm = mm(hmix, w_inp, rows=m, row0=0, col0=col_m, ncols=q_lora + kv_lora, wsel=li, out_dtype=F32,
                tn=q_lora + kv_lora, name="in_mla")
        kr = mm(hmix, w_inp, rows=m, row0=0, col0=col_kr, ncols=LANES, wsel=li, out_dtype=F32, tn=LANES, name="in_krope")
        st_kr.append(kr[:m_ctx, :ROPE_B].reshape(batch, seq, ROPE_B))

        g_sub = diff_subln_g[li].reshape(1, 2 * HD_A)
        lam = lams[li].reshape(1)
        oa = _ctx_attn(qa_ctx.reshape(batch, seq, BR_W), st_dk, st_dv, li=li, nh=H_A, dq=2 * HD_A, dv=2 * HD_A,
                       out_rows=m, diff=True, lam=lam, g=g_sub, out_scale=1.0 - lam_inits[li], name="diff_attn_ctx")
        ka_all = jnp.concatenate([ka_lat.reshape(dec_batch, dec_seq, BR_W),
                                  cache_diff_k[:, li].reshape(dec_batch, past, BR_W).astype(BF16)], axis=1)
        va_all = jnp.concatenate([va_lat.reshape(dec_batch, dec_seq, BR_W),
                                  cache_diff_v[:, li].reshape(dec_batch, past, BR_W).astype(BF16)], axis=1)
        oa = _lat_attn(qa_lat.reshape(dec_batch, dec_seq, BR_W), ka_all, va_all, oa, nh=H_A, hps=1, dq=2 * HD_A,
                       dv=2 * HD_A, tq=tq_lat, out_row0=m_ctx, diff=True, lam=lam, g=g_sub,
                       out_scale=1.0 - lam_inits[li], name="diff_attn_lat")

        cqn, st_ckv, ckr = _mla_prep(zm, kr, mla_q_norm_g[li].reshape(1, q_lora), mla_kv_norm_g[li].reshape(1, kv_lora),
                                     tabs_b, st(st_ckv), q_lora=q_lora, kv_lora=kv_lora, m_ctx=m_ctx, seg_len=dec_seq)
        qb_ctx = mm(cqn, w_uqp, col0=0, ncols=hb2, out_dtype=BF16, scale=qs_b, name="mla_q_ctx", **ctx)
        qb_lat = mm(cqn, w_uqp, col0=0, ncols=hb2, out_dtype=BF16, scale=qs_b,
                    rope=(tabs_b, ROPE_B // 4, (False, True)), name="mla_q_lat", **lat)
        cache_ckr = jnp.concatenate([cache_mla_ckv[:, li], cache_mla_krope[:, li],
                                     jnp.zeros((dec_batch, past, LANES - ROPE_B), F32)], axis=-1).astype(BF16)
        ckr_lat = jnp.concatenate([ckr[m_ctx:].reshape(dec_batch, dec_seq, kv_lora + LANES), cache_ckr], axis=1)
        ckr_lat = ckr_lat.reshape(dec_batch * nk_lat, kv_lora + LANES)
        tm_kv = _pick(math.gcd(m_ctx, dec_batch * nk_lat), 1024)
        kv_mm = functools.partial(_matmul, row0=0, col0=0, wsel=li, out_dtype=BF16, tm=tm_kv, tn=1024)
        kb_ctx = kv_mm(ckr, wk_aug, rows=m_ctx, ncols=hb2, name="mla_k_ctx")
        vb_ctx = kv_mm(ckr, wv_aug, rows=m_ctx, ncols=H_B * V_B, name="mla_v_ctx")
        kb_lat = kv_mm(ckr_lat, wk_aug, rows=dec_batch * nk_lat, ncols=hb2, name="mla_k_lat")
        vb_lat = kv_mm(ckr_lat, wv_aug, rows=dec_batch * nk_lat, ncols=H_B * V_B, name="mla_v_lat")
        ob = _ctx_attn(qb_ctx.reshape(batch, seq, hb2), kb_ctx.reshape(batch, seq, hb2),
                       vb_ctx.reshape(batch, seq, H_B * V_B), li=li, nh=H_B, dq=2 * LANES, dv=V_B, out_rows=m,
                       name="mla_attn_ctx")
        ob = _lat_attn(qb_lat.reshape(dec_batch, dec_seq, hb2), kb_lat.reshape(dec_batch, nk_lat, hb2),
                       vb_lat.reshape(dec_batch, nk_lat, H_B * V_B), ob, nh=H_B, hps=2, dq=2 * LANES, dv=V_B,
                       tq=tq_lat, out_row0=m_ctx, name="mla_attn_lat")

        oc = _ctx_attn(qc_ctx.reshape(batch, seq, BR_W), st_nk, st_nv, li=li, nh=H_C, dq=HD_C, dv=HD_C, out_rows=m,
                       name="na_attn_ctx")
        oc = _na_lat_attn(qc_lat.reshape(dec_batch, dec_seq, BR_W), kvc_lat.reshape(dec_batch, dec_seq, 2 * BR_W),
                          cache_k_c, cache_v_c, na_bias, na, oc, li=li, out_row0=m_ctx)

        merged = _merge(oa, ob, oc, gates, w_branch_b, li)
        y = mm(merged, w_out_b, rows=m, row0=0, col0=0, ncols=d, wsel=li, out_dtype=F32, tn=_pick(d, 1024, 256), name="w_out")

        j = li // 2
        if li % 2 == 0:
            x, hff = rn(x, y, mods, norm_g4, li_post=li, k_gate=2, li_pre=li, k_mod=3, want_x=True, want_h=True)
            tm_f = _pick(m, 512)
            nt = m // tm_f
            yff = _grouped_ffn(hff, ffn_w1b[j][None], ffn_w3b[j][None], ffn_w2b[j][None],
                               jnp.zeros((nt,), jnp.int32), jnp.arange(nt, dtype=jnp.int32),
                               jnp.full((1,), nt, jnp.int32), tm_f)
        else:
            x, hff, logits = rn(x, y, mods, norm_g4, li_post=li, k_gate=2, li_pre=li, k_mod=3, want_x=True, want_h=True,
                                router=router_p[j])
            tm_e = _pick(TOP_K * m, 512)
            row_token, dest, top_w, tile_expert, tile_src, n_used = _moe_plan(logits[:, :N_EXPERTS], tm_e)
            xs = _rows(hff, row_token)
            ys = _grouped_ffn(xs, moe_w1b[j], moe_w3b[j], moe_w2b[j], tile_expert, tile_src, n_used, tm_e)
            yff = top_w[:, 0:1] * _rows(ys, dest[:, 0]) + top_w[:, 1:2] * _rows(ys, dest[:, 1])
        if li + 1 < depth:
            x, hmix = rn(x, yff, mods, norm_g4, li_post=li, k_gate=5, li_pre=li + 1, k_mod=0, want_x=True, want_h=True)
        else:
            y_prompt, y_sample = rn(x, yff, mods, norm_g4, li_post=li, k_gate=5, li_pre=li, k_mod=0, want_x=True,
                                    want_h=False, out_split=True)

    return (y_prompt.reshape(batch, seq, d), y_sample.reshape(dec_batch, dec_seq, d),
            st_dk.reshape(batch, depth, seq, H_A, 2 * HD_A), st_dv.reshape(batch, depth, seq, H_A, 2 * HD_A),
            st_ckv, jnp.stack(st_kr, axis=1),
            st_nk.reshape(batch, depth, seq, H_C, HD_C), st_nv.reshape(batch, depth, seq, H_C, HD_C))
```

```python
import functools
import math

import numpy as np
import jax
import jax.numpy as jnp
from jax import lax
from jax.experimental import pallas as pl
from jax.experimental.pallas import tpu as pltpu

GRID_W = 64
RMS_EPS = 1e-6
ROPE_THETA = 10000.0
H_A, HD_A = 4, 128
H_B, NOPE_B, ROPE_B, V_B = 8, 128, 64, 128
H_C, HD_C = 8, 128
NA_KR_MAX, NA_KC = 8, 16
N_BRANCH = 3
N_EXPERTS = 8
TOP_K = 2
BR_W = 1024

LANES = 128
V7X_VMEM_BYTES = 64 * 1024 * 1024
VMEM_CAP_BYTES = V7X_VMEM_BYTES - 8 * 1024 * 1024
NA_ROWS_PER_BLOCK = 4
MASK_VALUE = -1e30
LOG2E = math.log2(math.e)

F32 = jnp.float32
BF16 = jnp.bfloat16
HIGHEST = lax.Precision.HIGHEST


def _pick(n, pref, mult=8):
    t = min(pref, n)
    t -= t % mult
    while t > mult and n % t:
        t -= mult
    assert t > 0 and n % t == 0, (n, pref, mult)
    return t


def _cparams(sems, vmem_est):
    limit = int(min(max(vmem_est, 16 * 1024 * 1024), VMEM_CAP_BYTES))
    return pltpu.CompilerParams(dimension_semantics=sems, vmem_limit_bytes=limit)


def _rms(v):
    return v * lax.rsqrt(jnp.mean(v * v, axis=-1, keepdims=True) + RMS_EPS)


def _rope128(a, cos, sa, sb, shift):
    return a * cos + pltpu.roll(a, LANES - shift, 1) * sa + pltpu.roll(a, shift, 1) * sb


def _ada_body(c_ref, w_ref, b_ref, o_ref):
    c = c_ref[...]
    s = (c * jax.nn.sigmoid(c)).astype(BF16)
    o_ref[...] = jnp.dot(s, w_ref[...].astype(BF16), preferred_element_type=F32) + b_ref[...]


def _ada_mod(cvec, ada_w, ada_b):
    depth, d, n = ada_w.shape
    r = cvec.shape[0]
    tn = _pick(n, 1024, LANES)
    return pl.pallas_call(
        _ada_body,
        grid=(depth, n // tn),
        in_specs=[pl.BlockSpec((r, d), lambda l, j: (0, 0)),
                  pl.BlockSpec((None, d, tn), lambda l, j: (l, 0, j)),
                  pl.BlockSpec((None, 1, tn), lambda l, j: (l, 0, j))],
        out_specs=pl.BlockSpec((None, r, tn), lambda l, j: (l, 0, j)),
        out_shape=jax.ShapeDtypeStruct((depth, r, n), F32),
        compiler_params=_cparams(("arbitrary", "arbitrary"), 3 * d * tn * 4 + d * tn * 2),
        name="ada_mod",
    )(cvec, ada_w, ada_b.reshape(depth, 1, n))


def _cast_shift_body(a_ref, b_ref, o_ref, *, tn):
    half = LANES // 2
    a = pltpu.roll(a_ref[...], tn - half, 1)
    b = pltpu.roll(b_ref[...], half, 1)
    lane = lax.broadcasted_iota(jnp.int32, b.shape, 1)
    o_ref[:, :tn - LANES] = a[:, :tn - LANES].astype(BF16)
    o_ref[:, tn - LANES:] = jnp.where(lane < half, a[:, tn - LANES:], b).astype(BF16)


def _cast_shift(w, c0, n, tn):
    depth, k, _ = w.shape
    a0 = c0 - LANES // 2
    assert a0 % tn == 0 and n % tn == 0 and tn % LANES == 0
    ab, r = a0 // tn, tn // LANES
    return pl.pallas_call(
        functools.partial(_cast_shift_body, tn=tn), grid=(depth, n // tn),
        in_specs=[pl.BlockSpec((None, k, tn), lambda l, j: (l, 0, ab + j)),
                  pl.BlockSpec((None, k, LANES), lambda l, j: (l, 0, (ab + j + 1) * r))],
        out_specs=pl.BlockSpec((None, k, tn), lambda l, j: (l, 0, j)),
        out_shape=jax.ShapeDtypeStruct((depth, k, n), BF16),
        compiler_params=_cparams(("arbitrary", "arbitrary"), 2 * k * (tn + LANES) * 4 + 2 * k * tn * 2 + 2 * k * tn * 4),
        name="cast_shift",
    )(w, w)


def _resid_norm_body(*refs, n_ctx_tiles, x_split, has_y, want_x, out_split, want_h, has_router):
    it = iter(refs)
    i = pl.program_id(0)
    if x_split:
        xa_ref, xb_ref = next(it), next(it)
        x = jnp.where(i < n_ctx_tiles, xa_ref[...], xb_ref[...])
    else:
        x = next(it)[...]
    if has_y:
        y_ref, gpost_ref, gate_ref = next(it), next(it), next(it)
        x = x + gate_ref[...] * (_rms(y_ref[...].astype(F32)) * gpost_ref[...])
    if want_h:
        gpre_ref, sc_ref, sh_ref = next(it), next(it), next(it)
    if has_router:
        r_ref = next(it)
    if want_x and out_split:
        xa_o, xb_o = next(it), next(it)

        @pl.when(i < n_ctx_tiles)
        def _():
            xa_o[...] = x

        @pl.when(i >= n_ctx_tiles)
        def _():
            xb_o[...] = x
    elif want_x:
        next(it)[...] = x
    if want_h:
        h = _rms(x) * gpre_ref[...]
        h = h * (1.0 + sc_ref[...]) + sh_ref[...]
        next(it)[...] = h.astype(BF16)
        if has_router:
            next(it)[...] = jnp.dot(h, r_ref[...], precision=HIGHEST, preferred_element_type=F32)


def _resid_norm(x, y, mods, norm_g, *, li_post, k_gate, li_pre, k_mod, m_ctx, seg_len,
                want_x, want_h, out_split=False, router=None):
    x_split = isinstance(x, tuple)
    d = x[0].shape[1] if x_split else x.shape[1]
    m = m_ctx + x[1].shape[0] if x_split else x.shape[0]
    tm = _pick(math.gcd(m_ctx, seg_len), 256)
    nct = m_ctx // tm

    def seg(i):
        return jnp.maximum((i * tm - m_ctx) // seg_len + 1, 0)

    row = pl.BlockSpec((tm, d), lambda i: (i, 0))
    row_a = pl.BlockSpec((tm, d), lambda i: (jnp.minimum(i, nct - 1), 0))
    row_b = pl.BlockSpec((tm, d), lambda i: (jnp.maximum(i - nct, 0), 0))
    in_specs, args = ([row_a, row_b], list(x)) if x_split else ([row], [x])
    if y is not None:
        k_post = 1 if k_gate == 2 else 3
        in_specs += [row,
                     pl.BlockSpec((None, None, 1, d), lambda i: (li_post, k_post, 0, 0)),
                     pl.BlockSpec((None, None, None, 1, d), lambda i: (li_post, seg(i), k_gate, 0, 0))]
        args += [y, norm_g, mods]
    if want_h:
        k_norm = 0 if k_mod == 0 else 2
        in_specs += [pl.BlockSpec((None, None, 1, d), lambda i: (li_pre, k_norm, 0, 0)),
                     pl.BlockSpec((None, None, None, 1, d), lambda i: (li_pre, seg(i), k_mod + 1, 0, 0)),
                     pl.BlockSpec((None, None, None, 1, d), lambda i: (li_pre, seg(i), k_mod, 0, 0))]
        args += [norm_g, mods, mods]
    if router is not None:
        in_specs.append(pl.BlockSpec(router.shape, lambda i: (0, 0)))
        args.append(router)
    out_specs, out_shape = [], []
    if want_x and out_split:
        out_specs += [row_a, row_b]
        out_shape += [jax.ShapeDtypeStruct((m_ctx, d), F32), jax.ShapeDtypeStruct((m - m_ctx, d), F32)]
    elif want_x:
        out_specs.append(row)
        out_shape.append(jax.ShapeDtypeStruct((m, d), F32))
    if want_h:
        out_specs.append(row)
        out_shape.append(jax.ShapeDtypeStruct((m, d), BF16))
        if router is not None:
            out_specs.append(pl.BlockSpec((tm, router.shape[1]), lambda i: (i, 0)))
            out_shape.append(jax.ShapeDtypeStruct((m, router.shape[1]), F32))
    body = functools.partial(_resid_norm_body, n_ctx_tiles=nct, x_split=x_split, has_y=y is not None,
                             want_x=want_x, out_split=out_split, want_h=want_h,
                             has_router=router is not None)
    return pl.pallas_call(
        body, grid=(m // tm,), in_specs=in_specs, out_specs=out_specs, out_shape=out_shape,
        compiler_params=_cparams(("arbitrary",), 14 * tm * d * 4 + d * LANES * 8),
        name="resid_norm",
    )(*args)


def _matmul_body(*refs, tn, scale, act, rope_shift, rope_pattern, seq_split, n_heads):
    x_ref, w_ref = refs[0], refs[1]
    o_ref = refs[-1]
    acc = jnp.dot(x_ref[...], w_ref[...], preferred_element_type=F32)
    if scale is not None:
        acc = acc * scale
    if act == "sigmoid":
        acc = jax.nn.sigmoid(acc)
    if seq_split is not None:
        refs[-2][...] = acc.astype(BF16)
        dh = tn // n_heads
        for b in range(acc.shape[0] // seq_split):
            for h in range(n_heads):
                o_ref[b, :, h, :] = acc[b * seq_split:(b + 1) * seq_split, h * dh:(h + 1) * dh]
        return
    if rope_shift is None:
        o_ref[...] = acc.astype(o_ref.dtype)
        return
    cos, sa, sb = refs[2][...], refs[3][...], refs[4][...]
    for g in range(tn // LANES):
        a = acc[:, g * LANES:(g + 1) * LANES]
        if rope_pattern[g % len(rope_pattern)]:
            a = _rope128(a, cos, sa, sb, rope_shift)
        o_ref[:, g * LANES:(g + 1) * LANES] = a.astype(o_ref.dtype)


def _matmul(x, w, *, rows, row0, col0, ncols, wsel, out_dtype, tm, tn, scale=None, act=None,
            rope=None, state=None, name="matmul"):
    k = x.shape[1]
    assert row0 % tm == 0 and rows % tm == 0 and col0 % tn == 0 and ncols % tn == 0, (row0, rows, col0, ncols, tm, tn)
    rb0, cb0 = row0 // tm, col0 // tn
    in_specs = [pl.BlockSpec((tm, k), lambda j, i: (rb0 + i, 0)),
                pl.BlockSpec((None, k, tn), lambda j, i: (wsel, 0, cb0 + j))]
    args = [x, w]
    rope_shift = rope_pattern = None
    if rope is not None:
        tabs, rope_shift, rope_pattern = rope
        nper = tabs[0].shape[0] // tm
        assert tabs[0].shape[0] % tm == 0
        for t in tabs:
            in_specs.append(pl.BlockSpec((tm, LANES), lambda j, i: (i % nper, 0)))
            args.append(t)
    aliases, seq_split, n_heads = {}, None, None
    out_spec = pl.BlockSpec((tm, tn), lambda j, i: (i, j))
    out_shape = jax.ShapeDtypeStruct((rows, ncols), out_dtype)
    if state is not None:
        prev, li, depth, seq_split, n_heads = state
        assert tm % seq_split == 0 and rope is None and tn == ncols and out_dtype == BF16
        dh = ncols // n_heads
        out_spec = [out_spec, pl.BlockSpec((tm // seq_split, None, seq_split, n_heads, dh),
                                           lambda j, i: (i, li, 0, 0, 0))]
        out_shape = [out_shape, jax.ShapeDtypeStruct((rows // seq_split, depth, seq_split, n_heads, dh), F32)]
        if prev is not None:
            in_specs.append(pl.BlockSpec(memory_space=pl.ANY))
            aliases = {len(args): 1}
            args.append(prev)
    osz = jnp.dtype(out_dtype).itemsize
    est = 2 * (tm * k * 2 + k * tn * 2 + tm * tn * osz) + 3 * tm * tn * 4 + 6 * tm * LANES * 4
    body = functools.partial(_matmul_body, tn=tn, scale=scale, act=act, rope_shift=rope_shift,
                             rope_pattern=rope_pattern, seq_split=seq_split, n_heads=n_heads)
    return pl.pallas_call(
        body, grid=(ncols // tn, rows // tm), in_specs=in_specs, out_specs=out_spec, out_shape=out_shape,
        input_output_aliases=aliases,
        compiler_params=_cparams(("arbitrary", "arbitrary"), est),
        name=name,
    )(*args)


def _mla_prep_body(zm_ref, kr_ref, qg_ref, kg_ref, cos_ref, sa_ref, sb_ref, *rest, q_lora, kv_lora,
                   n_ctx_tiles, seq):
    cq_o, ckv_o, ckr_o = rest[-3:]
    z = zm_ref[...]
    cq_o[...] = (_rms(z[:, :q_lora]) * qg_ref[...]).astype(BF16)
    ckv = _rms(z[:, q_lora:]) * kg_ref[...]
    ckr_o[:, :kv_lora] = ckv.astype(BF16)
    i = pl.program_id(0)
    lane = lax.broadcasted_iota(jnp.int32, kr_ref.shape, 1)
    kr = jnp.where(lane < ROPE_B, kr_ref[...], 0.0)

    @pl.when(i < n_ctx_tiles)
    def _():
        ckv_o[...] = ckv.reshape(ckv.shape[0] // seq, seq, kv_lora)
        ckr_o[:, kv_lora:] = kr.astype(BF16)

    @pl.when(i >= n_ctx_tiles)
    def _():
        ckr_o[:, kv_lora:] = _rope128(kr, cos_ref[...], sa_ref[...], sb_ref[...], ROPE_B // 4).astype(BF16)


def _mla_prep(zm, kr, qg, kg, tabs, state, *, q_lora, kv_lora, m_ctx, seg_len):
    prev, li, depth, seq = state[:4]
    m = zm.shape[0]
    tm = _pick(math.gcd(m_ctx, seg_len), 512)
    assert tm % seq == 0
    nct, nper = m_ctx // tm, seg_len // tm
    tab_spec = pl.BlockSpec((tm, LANES), lambda i: (jnp.maximum(i - nct, 0) % nper, 0))
    in_specs = [pl.BlockSpec((tm, q_lora + kv_lora), lambda i: (i, 0)),
                pl.BlockSpec((tm, LANES), lambda i: (i, 0)),
                pl.BlockSpec((1, q_lora), lambda i: (0, 0)),
                pl.BlockSpec((1, kv_lora), lambda i: (0, 0)),
                tab_spec, tab_spec, tab_spec]
    args = [zm, kr, qg, kg, *tabs]
    aliases = {}
    if prev is not None:
        in_specs.append(pl.BlockSpec(memory_space=pl.ANY))
        aliases = {len(args): 1}
        args.append(prev)
    body = functools.partial(_mla_prep_body, q_lora=q_lora, kv_lora=kv_lora, n_ctx_tiles=nct, seq=seq)
    return pl.pallas_call(
        body, grid=(m // tm,), in_specs=in_specs,
        out_specs=[pl.BlockSpec((tm, q_lora), lambda i: (i, 0)),
                   pl.BlockSpec((tm // seq, None, seq, kv_lora), lambda i: (jnp.minimum(i, nct - 1), li, 0, 0)),
                   pl.BlockSpec((tm, kv_lora + LANES), lambda i: (i, 0))],
        out_shape=[jax.ShapeDtypeStruct((m, q_lora), BF16),
                   jax.ShapeDtypeStruct((m_ctx // seq, depth, seq, kv_lora), F32),
                   jax.ShapeDtypeStruct((m, kv_lora + LANES), BF16)],
        input_output_aliases=aliases,
        compiler_params=_cparams(("arbitrary",), 8 * tm * (q_lora + kv_lora + 4 * LANES) * 4),
        name="mla_prep",
    )(*args)


_NT = (((1,), (1,)), ((), ()))


def _softmax_parts(q, k):
    s = lax.dot_general(q, k, _NT, preferred_element_type=F32)
    p = jnp.exp2(s - jnp.max(s, axis=-1, keepdims=True))
    return p, 1.0 / jnp.sum(p, axis=-1, keepdims=True)


def _one_head(q, k, v, *, diff, lam, g, out_scale):
    k = k.astype(BF16)
    v = v.astype(BF16)
    if not diff:
        p, r = _softmax_parts(q, k)
        return jnp.dot(p.astype(BF16), v, preferred_element_type=F32) * r
    p1, r1 = _softmax_parts(q[:, :HD_A], k[:, :HD_A])
    p2, r2 = _softmax_parts(q[:, HD_A:], k[:, HD_A:])
    p = (p1 * r1 - p2 * (lam * r2)).astype(BF16)
    o = jnp.dot(p, v, preferred_element_type=F32)
    return _rms(o) * g * out_scale


def _ctx_attn_body(*refs, nb, nh, dq, dv, diff, out_scale):
    it = iter(refs)
    lam = next(it)[0] if diff else None
    q_ref, k_ref, v_ref = next(it), next(it), next(it)
    g = next(it)[...] if diff else None
    o_ref = refs[-1]
    s = q_ref.shape[1]
    for b in range(nb):
        for h in range(nh):
            o = _one_head(q_ref[b, :, h * dq:(h + 1) * dq], k_ref[b, :, h * dq:(h + 1) * dq],
                          v_ref[b, :, h * dv:(h + 1) * dv], diff=diff, lam=lam, g=g, out_scale=out_scale)
            o_ref[b * s:(b + 1) * s, h * dv:(h + 1) * dv] = o.astype(o_ref.dtype)


def _ctx_attn(q, k, v, *, li, nh, dq, dv, out_rows, diff=False, lam=None, g=None, out_scale=None, name):
    bsz, s, _ = q.shape
    nb = 2 if bsz % 2 == 0 else 1

    def kv_spec(a, width):
        if a.ndim == 4:
            return pl.BlockSpec((nb, None, s, width), lambda i: (i, li, 0, 0))
        return pl.BlockSpec((nb, s, width), lambda i: (i, 0, 0))

    in_specs, args = [], []
    if diff:
        in_specs.append(pl.BlockSpec(memory_space=pltpu.SMEM))
        args.append(lam)
    in_specs += [pl.BlockSpec((nb, s, nh * dq), lambda i: (i, 0, 0)), kv_spec(k, nh * dq), kv_spec(v, nh * dv)]
    args += [q, k, v]
    if diff:
        in_specs.append(pl.BlockSpec((1, dv), lambda i: (0, 0)))
        args.append(g)
    body = functools.partial(_ctx_attn_body, nb=nb, nh=nh, dq=dq, dv=dv, diff=diff, out_scale=out_scale)
    return pl.pallas_call(
        body, grid=(bsz // nb,), in_specs=in_specs,
        out_specs=pl.BlockSpec((nb * s, nh * dv), lambda i: (i, 0)),
        out_shape=jax.ShapeDtypeStruct((out_rows, nh * dv), BF16),
        compiler_params=_cparams(("arbitrary",), 4 * nb * s * nh * (2 * dq + dv) * 4 + 16 * s * s * 4 * nb * nh),
        name=name,
    )(*args)


def _softmax_parts_t(k, q):
    st = lax.dot_general(k, q, _NT, preferred_element_type=F32)
    p = jnp.exp2(st - jnp.max(st, axis=0, keepdims=True))
    return p.astype(BF16), 1.0 / jnp.sum(p, axis=0, keepdims=True)


def _lat_attn_t_body(*refs, hps, dq, dv, diff, out_scale):
    it = iter(refs)
    lam = next(it)[0] if diff else None
    q_ref, k_ref, vt_ref = next(it), next(it), next(it)
    g = next(it)[...] if diff else None
    o_ref = refs[-1]
    if diff:
        q, k, vt = q_ref[...], k_ref[...], vt_ref[...]
        p1, r1 = _softmax_parts_t(k[:, :HD_A], q[:, :HD_A])
        p2, r2 = _softmax_parts_t(k[:, HD_A:], q[:, HD_A:])
        ot = (jnp.dot(vt, p1, preferred_element_type=F32) * r1
              - jnp.dot(vt, p2, preferred_element_type=F32) * (lam * r2))
        o_ref[...] = (_rms(ot.T) * g * out_scale).astype(o_ref.dtype)
        return
    ps = [_softmax_parts_t(k_ref[:, h * dq:(h + 1) * dq], q_ref[:, h * dq:(h + 1) * dq]) for h in range(hps)]
    for h, (p, r) in enumerate(ps):
        ot = jnp.dot(vt_ref[h * dv:(h + 1) * dv, :], p, preferred_element_type=F32) * r
        o_ref[:, h * dv:(h + 1) * dv] = ot.T.astype(o_ref.dtype)


def _lat_attn_body(*refs, hps, dq, dv, diff, out_scale):
    it = iter(refs)
    lam = next(it)[0] if diff else None
    q_ref, k_ref, v_ref = next(it), next(it), next(it)
    g = next(it)[...] if diff else None
    o_ref = refs[-1]
    for h in range(hps):
        o = _one_head(q_ref[:, h * dq:(h + 1) * dq], k_ref[:, h * dq:(h + 1) * dq],
                      v_ref[:, h * dv:(h + 1) * dv], diff=diff, lam=lam, g=g, out_scale=out_scale)
        o_ref[:, h * dv:(h + 1) * dv] = o.astype(o_ref.dtype)


def _lat_attn(q, k, v, prev_out, *, nh, hps, dq, dv, tq, out_row0, diff=False, lam=None, g=None,
              out_scale=None, v_transposed=False, name):
    bsz, n, _ = q.shape
    nk = k.shape[1]
    assert nh % hps == 0 and n % tq == 0 and out_row0 % tq == 0
    in_specs, args = [], []
    if diff:
        in_specs.append(pl.BlockSpec(memory_space=pltpu.SMEM))
        args.append(lam)
    v_spec = (pl.BlockSpec((None, hps * dv, nk), lambda b, h, i: (b, h, 0)) if v_transposed
              else pl.BlockSpec((None, nk, hps * dv), lambda b, h, i: (b, 0, h)))
    in_specs += [pl.BlockSpec((None, tq, hps * dq), lambda b, h, i: (b, i, h)),
                 pl.BlockSpec((None, nk, hps * dq), lambda b, h, i: (b, 0, h)), v_spec]
    args += [q, k, v]
    if diff:
        in_specs.append(pl.BlockSpec((1, dv), lambda b, h, i: (0, 0)))
        args.append(g)
    in_specs.append(pl.BlockSpec(memory_space=pl.ANY))
    aliases = {len(args): 0}
    args.append(prev_out)
    orb0, nqb = out_row0 // tq, n // tq
    n_chain = hps * (2 if diff else 1)
    est = 2 * (tq * hps * dq + nk * hps * (dq + dv)) * 2 + 5 * n_chain * tq * nk * 4
    if v_transposed:
        assert hps == 1 or not diff
        body = functools.partial(_lat_attn_t_body, hps=hps, dq=dq, dv=dv, diff=diff, out_scale=out_scale)
    else:
        body = functools.partial(_lat_attn_body, hps=hps, dq=dq, dv=dv, diff=diff, out_scale=out_scale)
    return pl.pallas_call(
        body, grid=(bsz, nh // hps, nqb), in_specs=in_specs,
        out_specs=pl.BlockSpec((tq, hps * dv), lambda b, h, i: (orb0 + b * nqb + i, h)),
        out_shape=jax.ShapeDtypeStruct(prev_out.shape, prev_out.dtype),
        input_output_aliases=aliases,
        compiler_params=_cparams(("arbitrary", "arbitrary", "arbitrary"), est),
        name=name,
    )(*args)


def _na_geometry(rows):
    rpb = NA_ROWS_PER_BLOCK
    kr = min(NA_KR_MAX, rows)
    span = rpb + kr - 1
    assert rows % rpb == 0 and rows >= span, rows
    nblk = rows // rpb
    r = np.arange(rows)
    row_start = np.clip(r - kr // 2, 0, rows - kr)
    blk_start = np.clip(rpb * np.arange(nblk) - kr // 2, 0, rows - span)
    cols = np.arange(GRID_W)
    col_start = np.clip(cols - NA_KC // 2, 0, GRID_W - NA_KC)
    dc = cols[None, :] - cols[:, None] + (NA_KC - 1)
    ok_c = (cols[None, :] >= col_start[:, None]) & (cols[None, :] < col_start[:, None] + NA_KC)
    sigs, cls_of_blk, dr_l, ok_l = {}, [], [], []
    for b in range(nblk):
        qr = rpb * b + np.arange(rpb)
        key_row = blk_start[b] + np.arange(span)
        dr = key_row[None, :] - qr[:, None] + (NA_KR_MAX - 1)
        ok_r = (key_row[None, :] >= row_start[qr][:, None]) & (key_row[None, :] < row_start[qr][:, None] + kr)
        sig = (dr.tobytes(), ok_r.tobytes())
        if sig not in sigs:
            sigs[sig] = len(sigs)
            dr_l.append(np.clip(dr, 0, 2 * NA_KR_MAX - 2))
            ok = ok_r[:, None, :, None] & ok_c[None, :, None, :]
            ok_l.append(ok.reshape(rpb * GRID_W, span * GRID_W))
        cls_of_blk.append(sigs[sig])
    return dict(span=span, nblk=nblk, blk_start=blk_start.astype(np.int32), cls=np.asarray(cls_of_blk, np.int32),
                dr=np.stack(dr_l), dc=np.clip(dc, 0, 2 * NA_KC - 2), ok=np.stack(ok_l))


def _na_bias(tables, na):
    depth, nh, n_dr, n_dc = tables.shape
    ncls, rpb, span = na["dr"].shape
    oh_r = jax.nn.one_hot(na["dr"].reshape(-1), n_dr, dtype=F32)
    oh_c = jax.nn.one_hot(na["dc"].reshape(-1), n_dc, dtype=F32).T
    t = jnp.einsum("xr,lhrc->lhxc", oh_r, tables.astype(F32), precision=HIGHEST)
    t = jnp.einsum("lhxc,cy->lhxy", t, oh_c, precision=HIGHEST)
    t = t.reshape(depth, nh, ncls, rpb, span, GRID_W, GRID_W).transpose(0, 1, 2, 3, 5, 4, 6)
    t = t.reshape(depth, nh, ncls, rpb * GRID_W, span * GRID_W)
    return jnp.where(na["ok"][None, None], t, MASK_VALUE)


def _na_body(cls_ref, start_ref, q_ref, k_ref, v_ref, kc_ref, vc_ref, bias_ref, prev_ref, o_ref, *, span_tok):
    del cls_ref, prev_ref
    blk = pl.program_id(2)
    start = pl.multiple_of(start_ref[blk] * GRID_W, GRID_W)
    q = q_ref[...]
    kl = k_ref[pl.ds(start, span_tok), :]
    vl = v_ref[pl.ds(start, span_tok), :]
    s_loc = lax.dot_general(q, kl, _NT, preferred_element_type=F32) + bias_ref[...]
    s_ctx = lax.dot_general(q, kc_ref[...].astype(BF16), _NT, preferred_element_type=F32)
    mx = jnp.maximum(jnp.max(s_loc, axis=-1, keepdims=True), jnp.max(s_ctx, axis=-1, keepdims=True))
    p_loc = jnp.exp2(s_loc - mx)
    p_ctx = jnp.exp2(s_ctx - mx)
    r = 1.0 / (jnp.sum(p_loc, axis=-1, keepdims=True) + jnp.sum(p_ctx, axis=-1, keepdims=True))
    o = jnp.dot(p_loc.astype(BF16), vl, preferred_element_type=F32)
    o = o + jnp.dot(p_ctx.astype(BF16), vc_ref[...].astype(BF16), preferred_element_type=F32)
    o_ref[...] = (o * r).astype(o_ref.dtype)


def _na_lat_attn(q, kv, cache_k, cache_v, bias, na, prev_out, *, li, out_row0):
    bsz, n, _ = q.shape
    past = cache_k.shape[2]
    rpb_tok, span_tok, nblk = NA_ROWS_PER_BLOCK * GRID_W, na["span"] * GRID_W, na["nblk"]
    gs = pltpu.PrefetchScalarGridSpec(
        num_scalar_prefetch=2, grid=(bsz, H_C, nblk),
        in_specs=[pl.BlockSpec((None, rpb_tok, HD_C), lambda b, h, i, cl, st: (b, i, h)),
                  pl.BlockSpec((None, n, HD_C), lambda b, h, i, cl, st: (b, 0, h)),
                  pl.BlockSpec((None, n, HD_C), lambda b, h, i, cl, st: (b, 0, H_C + h)),
                  pl.BlockSpec((None, None, past, HD_C), lambda b, h, i, cl, st: (b, li, 0, h)),
                  pl.BlockSpec((None, None, past, HD_C), lambda b, h, i, cl, st: (b, li, 0, h)),
                  pl.BlockSpec((None, None, None, rpb_tok, span_tok), lambda b, h, i, cl, st: (li, h, cl[i], 0, 0)),
                  pl.BlockSpec(memory_space=pl.ANY)],
        out_specs=pl.BlockSpec((rpb_tok, HD_C), lambda b, h, i, cl, st: (out_row0 // rpb_tok + b * nblk + i, h)))
    return pl.pallas_call(
        functools.partial(_na_body, span_tok=span_tok),
        grid_spec=gs, out_shape=jax.ShapeDtypeStruct(prev_out.shape, prev_out.dtype),
        input_output_aliases={8: 0},
        compiler_params=_cparams(("arbitrary", "arbitrary", "arbitrary"),
                                 8 * n * HD_C * 2 + 10 * rpb_tok * span_tok * 4),
        name="na_attn_lat",
    )(jnp.asarray(na["cls"]), jnp.asarray(na["blk_start"]), q, kv, kv, cache_k, cache_v, bias, prev_out)


def _merge_body(a_ref, b_ref, c_ref, ga_ref, gb_ref, gc_ref, wa_ref, wb_ref, wc_ref, o_ref):
    acc = ga_ref[...].astype(F32) * jnp.dot(a_ref[...], wa_ref[...], preferred_element_type=F32)
    acc += gb_ref[...].astype(F32) * jnp.dot(b_ref[...], wb_ref[...], preferred_element_type=F32)
    acc += gc_ref[...].astype(F32) * jnp.dot(c_ref[...], wc_ref[...], preferred_element_type=F32)
    o_ref[...] = acc.astype(o_ref.dtype)


def _merge(oa, ob, oc, gates, wb, li):
    m, kb = oa.shape
    d = wb.shape[-1]
    tm, tn = _pick(m, 1024), _pick(d, 1024, 256)
    nj = d // tn
    br = pl.BlockSpec((tm, kb), lambda j, i: (i, 0))
    gspec = [pl.BlockSpec((tm, tn), functools.partial(lambda j, i, s: (i, s * nj + j), s=s)) for s in range(N_BRANCH)]
    wspec = [pl.BlockSpec((None, None, kb, tn), functools.partial(lambda j, i, s: (li, s, 0, j), s=s)) for s in range(N_BRANCH)]
    est = 2 * (3 * tm * kb * 2 + 3 * tm * tn * 2 + 3 * kb * tn * 2 + tm * tn * 2) + 4 * tm * tn * 4
    return pl.pallas_call(
        _merge_body, grid=(nj, m // tm), in_specs=[br, br, br] + gspec + wspec,
        out_specs=pl.BlockSpec((tm, tn), lambda j, i: (i, j)),
        out_shape=jax.ShapeDtypeStruct((m, d), BF16),
        compiler_params=_cparams(("arbitrary", "arbitrary"), est),
        name="branch_merge",
    )(oa, ob, oc, gates, gates, gates, wb, wb, wb)


def _ffn1_body(te_ref, ts_ref, nu_ref, x_ref, w13_ref, o_ref):
    del te_ref, ts_ref

    @pl.when(pl.program_id(1) < nu_ref[0])
    def _():
        tf = o_ref.shape[1]
        h = jnp.dot(x_ref[...], w13_ref[...], preferred_element_type=F32)
        h1, h3 = h[:, :tf], h[:, tf:]
        o_ref[...] = (h1 * jax.nn.sigmoid(h1) * h3).astype(o_ref.dtype)


def _ffn2_body(te_ref, ts_ref, nu_ref, h_ref, w2_ref, o_ref):
    del te_ref, ts_ref

    @pl.when(pl.program_id(1) < nu_ref[0])
    def _():
        o_ref[...] = jnp.dot(h_ref[...], w2_ref[...], preferred_element_type=F32).astype(o_ref.dtype)


def _ffn_tile(f):
    return _pick(f, 1408, LANES)


def _interleave_w13(w1, w3):
    f = w1.shape[-1]
    tf = _ffn_tile(f)
    lead = w1.shape[:-1]
    w = jnp.concatenate([w1.reshape(lead + (f // tf, tf)), w3.reshape(lead + (f // tf, tf))], axis=-1)
    return w.reshape(lead + (2 * f,)).astype(BF16)


def _grouped_ffn(xs, w13, w2, tile_expert, tile_src, n_used, tm):
    r, d = xs.shape
    f = w2.shape[1]
    nt = r // tm
    tf = _ffn_tile(f)
    tn = _pick(d, 1024, 256)
    gs1 = pltpu.PrefetchScalarGridSpec(
        num_scalar_prefetch=3, grid=(f // tf, nt),
        in_specs=[pl.BlockSpec((tm, d), lambda j, t, te, ts, nu: (ts[t], 0)),
                  pl.BlockSpec((None, d, 2 * tf), lambda j, t, te, ts, nu: (te[t], 0, j))],
        out_specs=pl.BlockSpec((tm, tf), lambda j, t, te, ts, nu: (ts[t], j)))
    hid = pl.pallas_call(
        _ffn1_body, grid_spec=gs1, out_shape=jax.ShapeDtypeStruct((r, f), BF16),
        compiler_params=_cparams(("arbitrary", "arbitrary"),
                                 2 * (tm * d * 2 + 2 * d * tf * 2 + tm * tf * 2) + 4 * tm * tf * 4),
        name="ffn_up",
    )(tile_expert, tile_src, n_used, xs, w13)
    gs2 = pltpu.PrefetchScalarGridSpec(
        num_scalar_prefetch=3, grid=(d // tn, nt),
        in_specs=[pl.BlockSpec((tm, f), lambda j, t, te, ts, nu: (ts[t], 0)),
                  pl.BlockSpec((None, f, tn), lambda j, t, te, ts, nu: (te[t], 0, j))],
        out_specs=pl.BlockSpec((tm, tn), lambda j, t, te, ts, nu: (ts[t], j)))
    return pl.pallas_call(
        _ffn2_body, grid_spec=gs2, out_shape=jax.ShapeDtypeStruct((r, d), F32),
        compiler_params=_cparams(("arbitrary", "arbitrary"),
                                 2 * (tm * f * 2 + f * tn * 2 + tm * tn * 4) + 2 * tm * tn * 4),
        name="ffn_down",
    )(tile_expert, tile_src, n_used, hid, w2)


def _moe_plan(logits, tm):
    m = logits.shape[0]
    top_v, top_i = lax.top_k(logits, TOP_K)
    top_w = jax.nn.softmax(top_v, axis=-1)
    flat_e = top_i.reshape(-1)
    onehot = (flat_e[:, None] == jnp.arange(N_EXPERTS)[None, :]).astype(jnp.int32)
    csum = jnp.cumsum(onehot, axis=0)
    counts = csum[-1]
    rank = jnp.take_along_axis(csum, flat_e[:, None], axis=1)[:, 0] - 1
    padded = ((counts + tm - 1) // tm) * tm
    ends = jnp.cumsum(padded)
    offs = ends - padded
    dest = offs[flat_e] + rank
    n_rows = TOP_K * m + N_EXPERTS * tm
    nt = n_rows // tm
    row_token = jnp.zeros((n_rows,), jnp.int32).at[dest].set(jnp.arange(TOP_K * m, dtype=jnp.int32) // TOP_K)
    n_used = (ends[-1] // tm).astype(jnp.int32)
    tile_src = jnp.minimum(jnp.arange(nt, dtype=jnp.int32), n_used - 1)
    tile_expert = jnp.sum((tile_src[:, None] * tm >= ends[None, :]).astype(jnp.int32), axis=1)
    tile_expert = jnp.minimum(tile_expert, N_EXPERTS - 1).astype(jnp.int32)
    return row_token, dest.reshape(m, TOP_K), top_w, tile_expert, tile_src, n_used.reshape(1)


def _rows(a, idx):
    return a.at[idx].get(mode="promise_in_bounds")


def _rope_tables(n_tok, dim):
    half = dim // 2
    t = jnp.arange(n_tok)
    row = (t // GRID_W).astype(F32)
    col = (t % GRID_W).astype(F32)
    inv = ROPE_THETA ** (-jnp.arange(0, half, 2, dtype=F32) / half)
    ar = row[:, None] * inv[None, :]
    ac = col[:, None] * inv[None, :]
    ang = jnp.concatenate([ar, ar, ac, ac], axis=-1)
    cos, sin = jnp.cos(ang), jnp.sin(ang)
    first = (np.arange(dim) % half) < (dim // 4)
    sa = jnp.where(first[None, :], -sin, 0.0)
    sb = jnp.where(first[None, :], 0.0, sin)
    pad = LANES - dim
    if pad:
        cos = jnp.pad(cos, ((0, 0), (0, pad)), constant_values=1.0)
        sa = jnp.pad(sa, ((0, 0), (0, pad)))
        sb = jnp.pad(sb, ((0, 0), (0, pad)))
    return cos, sa, sb


def kernel(x_prompt, x_sample, cache_diff_k, cache_diff_v, cache_mla_ckv, cache_mla_krope, cache_na_k, cache_na_v, c, c_ctx, norm_g, ada_w, ada_b, w_in, diff_lambda, diff_subln_g, mla_q_norm_g, mla_kv_norm_g, mla_w_uq, mla_w_ukv, na_rel_bias, w_branch, w_out, ffn_w1, ffn_w3, ffn_w2, moe_router, moe_w1, moe_w3, moe_w2):
    batch, seq, d = x_prompt.shape
    dec_batch, dec_seq, _ = x_sample.shape
    depth = norm_g.shape[0]
    past = cache_diff_k.shape[2]
    q_lora, kv_lora = mla_q_norm_g.shape[1], mla_kv_norm_g.shape[1]
    m_ctx, m_lat = batch * seq, dec_batch * dec_seq
    m = m_ctx + m_lat
    nk_lat = dec_seq + past
    assert dec_seq % GRID_W == 0 and kv_lora % LANES == 0 and q_lora % LANES == 0

    a3 = 3 * BR_W
    col_m = a3
    col_kr = a3 + q_lora + kv_lora
    c_na = col_kr + ROPE_B
    n_rest = 3 * BR_W + N_BRANCH * d
    w_head = w_in[..., :col_kr + LANES].astype(BF16)
    w_rest = _cast_shift(w_in, c_na, n_rest, 768)
    col_c, col_g = 0, 3 * BR_W
    wq = mla_w_uq.reshape(depth, q_lora, H_B, NOPE_B + ROPE_B)
    w_uqp = jnp.pad(wq, ((0, 0), (0, 0), (0, 0), (0, 2 * LANES - NOPE_B - ROPE_B))).reshape(depth, q_lora, H_B * 2 * LANES).astype(BF16)
    wkv = mla_w_ukv.reshape(depth, kv_lora, H_B, NOPE_B + V_B)
    wk_top = jnp.pad(wkv[..., :NOPE_B], ((0, 0), (0, 0), (0, 0), (0, 2 * LANES - NOPE_B))).reshape(depth, kv_lora, H_B * 2 * LANES)
    eye = np.zeros((LANES, H_B, 2 * LANES), np.float32)
    for r_ in range(ROPE_B):
        eye[r_, :, NOPE_B + r_] = 1.0
    wk_aug = jnp.concatenate([wk_top, jnp.broadcast_to(jnp.asarray(eye.reshape(LANES, -1)), (depth, LANES, H_B * 2 * LANES))], axis=1).astype(BF16)
    wv_aug = jnp.pad(wkv[..., NOPE_B:].reshape(depth, kv_lora, H_B * V_B), ((0, 0), (0, LANES), (0, 0))).astype(BF16)
    w_branch_b = w_branch.astype(BF16)
    w_out_b = w_out.astype(BF16)
    ffn_w13b, ffn_w2b = _interleave_w13(ffn_w1, ffn_w3), ffn_w2.astype(BF16)
    moe_w13b, moe_w2b = _interleave_w13(moe_w1, moe_w3), moe_w2.astype(BF16)
    router_p = jnp.pad(moe_router, ((0, 0), (0, 0), (0, LANES - N_EXPERTS)))
    norm_g4 = norm_g.reshape(depth, 4, 1, d)

    n_seg = 1 + dec_batch
    r_pad = -(-n_seg // 8) * 8
    cvec = jnp.zeros((r_pad, d), F32).at[0].set(c_ctx).at[1:n_seg].set(c)
    mods = _ada_mod(cvec, ada_w, ada_b).reshape(depth, r_pad, 6, 1, d)

    tabs_a = _rope_tables(dec_seq, HD_A)
    tabs_b = _rope_tables(dec_seq, ROPE_B)
    na = _na_geometry(dec_seq // GRID_W)
    na_bias = _na_bias(na_rel_bias, na)
    lam_inits = [0.8 - 0.6 * math.exp(-0.3 * li) for li in range(depth)]
    lp = diff_lambda.astype(F32)
    lams = jnp.exp(jnp.sum(lp[:, 0] * lp[:, 1], axis=-1)) - jnp.exp(jnp.sum(lp[:, 2] * lp[:, 3], axis=-1)) + jnp.asarray(lam_inits, F32)

    tm_big = _pick(math.gcd(m_ctx, dec_seq), 1024)
    assert tm_big % seq == 0
    tq_lat = _pick(dec_seq, 512)
    rn = functools.partial(_resid_norm, m_ctx=m_ctx, seg_len=dec_seq)
    mm = functools.partial(_matmul, tm=tm_big)
    qs_a, qs_b, qs_c = HD_A ** -0.5 * LOG2E, (NOPE_B + ROPE_B) ** -0.5 * LOG2E, HD_C ** -0.5 * LOG2E
    hb2 = H_B * 2 * LANES
    cache_k_c = cache_na_k.reshape(dec_batch, depth, past, H_C * HD_C)
    cache_v_c = cache_na_v.reshape(dec_batch, depth, past, H_C * HD_C)

    x = (x_prompt.reshape(m_ctx, d), x_sample.reshape(m_lat, d))
    (hmix,) = rn(x, None, mods, norm_g4, li_post=0, k_gate=2, li_pre=0, k_mod=0, want_x=False, want_h=True)
    st_dk = st_dv = st_ckv = st_nk = st_nv = None
    st_kr = []

    for li in range(depth):
        def st(prev, n_heads=None):
            return (prev, li, depth, seq, n_heads)

        def b3(a):
            return a.reshape(batch, seq, a.shape[-1])

        ctx = dict(rows=m_ctx, row0=0, wsel=li, tn=1024)
        lat = dict(rows=m_lat, row0=m_ctx, wsel=li, tn=1024)
        qa_ctx = mm(hmix, w_head, col0=0, ncols=BR_W, out_dtype=BF16, scale=qs_a, name="in_qa_ctx", **ctx)
        ka_ctx, st_dk = mm(hmix, w_head, col0=BR_W, ncols=BR_W, out_dtype=BF16, state=st(st_dk, H_A), name="in_ka_ctx", **ctx)
        va_ctx, st_dv = mm(hmix, w_head, col0=2 * BR_W, ncols=BR_W, out_dtype=BF16, state=st(st_dv, H_A), name="in_va_ctx", **ctx)
        qa_lat = mm(hmix, w_head, col0=0, ncols=BR_W, out_dtype=BF16, scale=qs_a,
                    rope=(tabs_a, HD_A // 4, (True,)), name="in_qa_lat", **lat)
        ka_lat = mm(hmix, w_head, col0=BR_W, ncols=BR_W, out_dtype=BF16,
                    rope=(tabs_a, HD_A // 4, (True,)), name="in_ka_lat", **lat)
        va_lat = mm(hmix, w_head, col0=2 * BR_W, ncols=BR_W, out_dtype=BF16, name="in_va_lat", **lat)
        qc_ctx = mm(hmix, w_rest, col0=col_c, ncols=BR_W, out_dtype=BF16, scale=qs_c, name="in_qc_ctx", **ctx)
        kc_ctx, st_nk = mm(hmix, w_rest, col0=col_c + BR_W, ncols=BR_W, out_dtype=BF16, state=st(st_nk, H_C), name="in_kc_ctx", **ctx)
        vc_ctx, st_nv = mm(hmix, w_rest, col0=col_c + 2 * BR_W, ncols=BR_W, out_dtype=BF16, state=st(st_nv, H_C), name="in_vc_ctx", **ctx)
        qc_lat = mm(hmix, w_rest, col0=col_c, ncols=BR_W, out_dtype=BF16, scale=qs_c, name="in_qc_lat", **lat)
        kvc_lat = mm(hmix, w_rest, col0=col_c + BR_W, ncols=2 * BR_W, out_dtype=BF16, name="in_kvc_lat", **lat)
        gates = mm(hmix, w_rest, rows=m, row0=0, col0=col_g, ncols=N_BRANCH * d, wsel=li, out_dtype=BF16, tn=768,
                   act="sigmoid", name="in_gates")
        zm = mm(hmix, w_head, rows=m, row0=0, col0=col_m, ncols=q_lora + kv_lora, wsel=li, out_dtype=F32,
                tn=q_lora + kv_lora, name="in_mla")
        kr = mm(hmix, w_head, rows=m, row0=0, col0=col_kr, ncols=LANES, wsel=li, out_dtype=F32, tn=LANES, name="in_krope")
        st_kr.append(kr[:m_ctx, :ROPE_B].reshape(batch, seq, ROPE_B))

        g_sub = diff_subln_g[li].reshape(1, 2 * HD_A)
        lam = lams[li].reshape(1)
        oa = _ctx_attn(b3(qa_ctx), b3(ka_ctx), b3(va_ctx), li=li, nh=H_A, dq=2 * HD_A, dv=2 * HD_A,
                       out_rows=m, diff=True, lam=lam, g=g_sub, out_scale=1.0 - lam_inits[li], name="diff_attn_ctx")
        ka_all = jnp.concatenate([ka_lat.reshape(dec_batch, dec_seq, BR_W),
                                  cache_diff_k[:, li].reshape(dec_batch, past, BR_W).astype(BF16)], axis=1)
        va_all = jnp.concatenate([va_lat.reshape(dec_batch, dec_seq, BR_W),
                                  cache_diff_v[:, li].reshape(dec_batch, past, BR_W).astype(BF16)], axis=1)
        oa = _lat_attn(qa_lat.reshape(dec_batch, dec_seq, BR_W), ka_all, jnp.swapaxes(va_all, 1, 2), oa, nh=H_A,
                       hps=1, dq=2 * HD_A, dv=2 * HD_A, tq=tq_lat, out_row0=m_ctx, diff=True, lam=lam, g=g_sub,
                       out_scale=1.0 - lam_inits[li], v_transposed=True, name="diff_attn_lat")

        cqn, st_ckv, ckr = _mla_prep(zm, kr, mla_q_norm_g[li].reshape(1, q_lora), mla_kv_norm_g[li].reshape(1, kv_lora),
                                     tabs_b, st(st_ckv), q_lora=q_lora, kv_lora=kv_lora, m_ctx=m_ctx, seg_len=dec_seq)
        qb_ctx = mm(cqn, w_uqp, col0=0, ncols=hb2, out_dtype=BF16, scale=qs_b, name="mla_q_ctx", **ctx)
        qb_lat = mm(cqn, w_uqp, col0=0, ncols=hb2, out_dtype=BF16, scale=qs_b,
                    rope=(tabs_b, ROPE_B // 4, (False, True)), name="mla_q_lat", **lat)
        cache_ckr = jnp.concatenate([cache_mla_ckv[:, li], cache_mla_krope[:, li],
                                     jnp.zeros((dec_batch, past, LANES - ROPE_B), F32)], axis=-1).astype(BF16)
        ckr_lat = jnp.concatenate([ckr[m_ctx:].reshape(dec_batch, dec_seq, kv_lora + LANES), cache_ckr], axis=1)
        ckr_lat = ckr_lat.reshape(dec_batch * nk_lat, kv_lora + LANES)
        tm_kv = _pick(math.gcd(m_ctx, dec_batch * nk_lat), 1024)
        kv_mm = functools.partial(_matmul, row0=0, col0=0, wsel=li, out_dtype=BF16, tm=tm_kv, tn=1024)
        kb_ctx = kv_mm(ckr, wk_aug, rows=m_ctx, ncols=hb2, name="mla_k_ctx")
        vb_ctx = kv_mm(ckr, wv_aug, rows=m_ctx, ncols=H_B * V_B, name="mla_v_ctx")
        kb_lat = kv_mm(ckr_lat, wk_aug, rows=dec_batch * nk_lat, ncols=hb2, name="mla_k_lat")
        vb_lat = kv_mm(ckr_lat, wv_aug, rows=dec_batch * nk_lat, ncols=H_B * V_B, name="mla_v_lat")
        ob = _ctx_attn(qb_ctx.reshape(batch, seq, hb2), kb_ctx.reshape(batch, seq, hb2),
                       vb_ctx.reshape(batch, seq, H_B * V_B), li=li, nh=H_B, dq=2 * LANES, dv=V_B, out_rows=m,
                       name="mla_attn_ctx")
        vbt_lat = jnp.swapaxes(vb_lat.reshape(dec_batch, nk_lat, H_B * V_B), 1, 2)
        ob = _lat_attn(qb_lat.reshape(dec_batch, dec_seq, hb2), kb_lat.reshape(dec_batch, nk_lat, hb2),
                       vbt_lat, ob, nh=H_B, hps=2, dq=2 * LANES, dv=V_B,
                       tq=tq_lat, out_row0=m_ctx, v_transposed=True, name="mla_attn_lat")

        oc = _ctx_attn(b3(qc_ctx), b3(kc_ctx), b3(vc_ctx), li=li, nh=H_C, dq=HD_C, dv=HD_C, out_rows=m,
                       name="na_attn_ctx")
        oc = _na_lat_attn(qc_lat.reshape(dec_batch, dec_seq, BR_W), kvc_lat.reshape(dec_batch, dec_seq, 2 * BR_W),
                          cache_k_c, cache_v_c, na_bias, na, oc, li=li, out_row0=m_ctx)

        merged = _merge(oa, ob, oc, gates, w_branch_b, li)
        y = mm(merged, w_out_b, rows=m, row0=0, col0=0, ncols=d, wsel=li, out_dtype=F32, tn=_pick(d, 1024, 256), name="w_out")

        j = li // 2
        if li % 2 == 0:
            x, hff = rn(x, y, mods, norm_g4, li_post=li, k_gate=2, li_pre=li, k_mod=3, want_x=True, want_h=True)
            tm_f = _pick(m, 512)
            nt = m // tm_f
            yff = _grouped_ffn(hff, ffn_w13b[j][None], ffn_w2b[j][None],
                               jnp.zeros((nt,), jnp.int32), jnp.arange(nt, dtype=jnp.int32),
                               jnp.full((1,), nt, jnp.int32), tm_f)
        else:
            x, hff, logits = rn(x, y, mods, norm_g4, li_post=li, k_gate=2, li_pre=li, k_mod=3, want_x=True, want_h=True,
                                router=router_p[j])
            tm_e = _pick(TOP_K * m, 512)
            row_token, dest, top_w, tile_expert, tile_src, n_used = _moe_plan(logits[:, :N_EXPERTS], tm_e)
            xs = _rows(hff, row_token)
            ys = _grouped_ffn(xs, moe_w13b[j], moe_w2b[j], tile_expert, tile_src, n_used, tm_e)
            yff = top_w[:, 0:1] * _rows(ys, dest[:, 0]) + top_w[:, 1:2] * _rows(ys, dest[:, 1])
        if li + 1 < depth:
            x, hmix = rn(x, yff, mods, norm_g4, li_post=li, k_gate=5, li_pre=li + 1, k_mod=0, want_x=True, want_h=True)
        else:
            y_prompt, y_sample = rn(x, yff, mods, norm_g4, li_post=li, k_gate=5, li_pre=li, k_mod=0, want_x=True,
                                    want_h=False, out_split=True)

    return (y_prompt.reshape(batch, seq, d), y_sample.reshape(dec_batch, dec_seq, d),
            st_dk, st_dv, st_ckv, jnp.stack(st_kr, axis=1), st_nk, st_nv)
```

```python
import functools
import math

import numpy as np
import jax
import jax.numpy as jnp
from jax import lax
from jax.experimental import pallas as pl
from jax.experimental.pallas import tpu as pltpu

GRID_W = 64
RMS_EPS = 1e-6
ROPE_THETA = 10000.0
H_A, HD_A = 4, 128
H_B, NOPE_B, ROPE_B, V_B = 8, 128, 64, 128
H_C, HD_C = 8, 128
NA_KR_MAX, NA_KC = 8, 16
N_BRANCH = 3
N_EXPERTS = 8
TOP_K = 2
BR_W = 1024

LANES = 128
V7X_VMEM_BYTES = 64 * 1024 * 1024
VMEM_CAP_BYTES = V7X_VMEM_BYTES - 8 * 1024 * 1024
NA_ROWS_PER_BLOCK = 4
MASK_VALUE = -1e30
LOG2E = math.log2(math.e)

F32 = jnp.float32
BF16 = jnp.bfloat16
HIGHEST = lax.Precision.HIGHEST


def _pick(n, pref, mult=8):
    t = min(pref, n)
    t -= t % mult
    while t > mult and n % t:
        t -= mult
    assert t > 0 and n % t == 0, (n, pref, mult)
    return t


def _cparams(sems, vmem_est):
    limit = int(min(max(vmem_est, 16 * 1024 * 1024), VMEM_CAP_BYTES))
    return pltpu.CompilerParams(dimension_semantics=sems, vmem_limit_bytes=limit)


def _rms(v):
    return v * lax.rsqrt(jnp.mean(v * v, axis=-1, keepdims=True) + RMS_EPS)


def _rope128(a, cos, sa, sb, shift):
    return a * cos + pltpu.roll(a, LANES - shift, 1) * sa + pltpu.roll(a, shift, 1) * sb


def _ada_body(c_ref, w_ref, b_ref, o_ref):
    c = c_ref[...]
    s = (c * jax.nn.sigmoid(c)).astype(BF16)
    o_ref[...] = jnp.dot(s, w_ref[...].astype(BF16), preferred_element_type=F32) + b_ref[...]


def _ada_mod(cvec, ada_w, ada_b):
    depth, d, n = ada_w.shape
    r = cvec.shape[0]
    tn = _pick(n, 1024, LANES)
    return pl.pallas_call(
        _ada_body,
        grid=(depth, n // tn),
        in_specs=[pl.BlockSpec((r, d), lambda l, j: (0, 0)),
                  pl.BlockSpec((None, d, tn), lambda l, j: (l, 0, j)),
                  pl.BlockSpec((None, 1, tn), lambda l, j: (l, 0, j))],
        out_specs=pl.BlockSpec((None, r, tn), lambda l, j: (l, 0, j)),
        out_shape=jax.ShapeDtypeStruct((depth, r, n), F32),
        compiler_params=_cparams(("arbitrary", "arbitrary"), 3 * d * tn * 4 + d * tn * 2),
        name="ada_mod",
    )(cvec, ada_w, ada_b.reshape(depth, 1, n))


def _cast_shift_body(a_ref, b_ref, o_ref, *, tn):
    half = LANES // 2
    a = pltpu.roll(a_ref[...], tn - half, 1)
    b = pltpu.roll(b_ref[...], half, 1)
    lane = lax.broadcasted_iota(jnp.int32, b.shape, 1)
    o_ref[:, :tn - LANES] = a[:, :tn - LANES].astype(BF16)
    o_ref[:, tn - LANES:] = jnp.where(lane < half, a[:, tn - LANES:], b).astype(BF16)


def _cast_body(w_ref, o_ref):
    o_ref[...] = w_ref[...].astype(BF16)


def _cast_head(w, n):
    depth, k, _ = w.shape
    assert n % LANES == 0
    tk = _pick(k, 256)
    blk = pl.BlockSpec((None, tk, n), lambda l, i: (l, i, 0))
    return pl.pallas_call(
        _cast_body, grid=(depth, k // tk), in_specs=[blk], out_specs=blk,
        out_shape=jax.ShapeDtypeStruct((depth, k, n), BF16),
        compiler_params=_cparams(("arbitrary", "arbitrary"), 2 * tk * n * 6 + tk * n * 4),
        name="cast_head",
    )(w)


def _cast_shift(w, c0, n, tn):
    depth, k, _ = w.shape
    a0 = c0 - LANES // 2
    assert a0 % tn == 0 and n % tn == 0 and tn % LANES == 0
    ab, r = a0 // tn, tn // LANES
    return pl.pallas_call(
        functools.partial(_cast_shift_body, tn=tn), grid=(depth, n // tn),
        in_specs=[pl.BlockSpec((None, k, tn), lambda l, j: (l, 0, ab + j)),
                  pl.BlockSpec((None, k, LANES), lambda l, j: (l, 0, (ab + j + 1) * r))],
        out_specs=pl.BlockSpec((None, k, tn), lambda l, j: (l, 0, j)),
        out_shape=jax.ShapeDtypeStruct((depth, k, n), BF16),
        compiler_params=_cparams(("arbitrary", "arbitrary"), 2 * k * (tn + LANES) * 4 + 2 * k * tn * 2 + 2 * k * tn * 4),
        name="cast_shift",
    )(w, w)


def _resid_norm_body(*refs, n_ctx_tiles, x_split, has_y, want_x, out_split, want_h, has_router):
    it = iter(refs)
    i = pl.program_id(0)
    if x_split:
        xa_ref, xb_ref = next(it), next(it)
        x = jnp.where(i < n_ctx_tiles, xa_ref[...], xb_ref[...])
    else:
        x = next(it)[...]
    if has_y:
        y_ref, gpost_ref, gate_ref = next(it), next(it), next(it)
        x = x + gate_ref[...] * (_rms(y_ref[...].astype(F32)) * gpost_ref[...])
    if want_h:
        gpre_ref, sc_ref, sh_ref = next(it), next(it), next(it)
    if has_router:
        r_ref = next(it)
    if want_x and out_split:
        xa_o, xb_o = next(it), next(it)

        @pl.when(i < n_ctx_tiles)
        def _():
            xa_o[...] = x

        @pl.when(i >= n_ctx_tiles)
        def _():
            xb_o[...] = x
    elif want_x:
        next(it)[...] = x
    if want_h:
        h = _rms(x) * gpre_ref[...]
        h = h * (1.0 + sc_ref[...]) + sh_ref[...]
        next(it)[...] = h.astype(BF16)
        if has_router:
            next(it)[...] = jnp.dot(h, r_ref[...], precision=HIGHEST, preferred_element_type=F32)


def _resid_norm(x, y, mods, norm_g, *, li_post, k_gate, li_pre, k_mod, m_ctx, seg_len,
                want_x, want_h, out_split=False, router=None):
    x_split = isinstance(x, tuple)
    d = x[0].shape[1] if x_split else x.shape[1]
    m = m_ctx + x[1].shape[0] if x_split else x.shape[0]
    tm = _pick(math.gcd(m_ctx, seg_len), 256)
    nct = m_ctx // tm

    def seg(i):
        return jnp.maximum((i * tm - m_ctx) // seg_len + 1, 0)

    row = pl.BlockSpec((tm, d), lambda i: (i, 0))
    row_a = pl.BlockSpec((tm, d), lambda i: (jnp.minimum(i, nct - 1), 0))
    row_b = pl.BlockSpec((tm, d), lambda i: (jnp.maximum(i - nct, 0), 0))
    in_specs, args = ([row_a, row_b], list(x)) if x_split else ([row], [x])
    if y is not None:
        k_post = 1 if k_gate == 2 else 3
        in_specs += [row,
                     pl.BlockSpec((None, None, 1, d), lambda i: (li_post, k_post, 0, 0)),
                     pl.BlockSpec((None, None, None, 1, d), lambda i: (li_post, seg(i), k_gate, 0, 0))]
        args += [y, norm_g, mods]
    if want_h:
        k_norm = 0 if k_mod == 0 else 2
        in_specs += [pl.BlockSpec((None, None, 1, d), lambda i: (li_pre, k_norm, 0, 0)),
                     pl.BlockSpec((None, None, None, 1, d), lambda i: (li_pre, seg(i), k_mod + 1, 0, 0)),
                     pl.BlockSpec((None, None, None, 1, d), lambda i: (li_pre, seg(i), k_mod, 0, 0))]
        args += [norm_g, mods, mods]
    if router is not None:
        in_specs.append(pl.BlockSpec(router.shape, lambda i: (0, 0)))
        args.append(router)
    out_specs, out_shape = [], []
    if want_x and out_split:
        out_specs += [row_a, row_b]
        out_shape += [jax.ShapeDtypeStruct((m_ctx, d), F32), jax.ShapeDtypeStruct((m - m_ctx, d), F32)]
    elif want_x:
        out_specs.append(row)
        out_shape.append(jax.ShapeDtypeStruct((m, d), F32))
    if want_h:
        out_specs.append(row)
        out_shape.append(jax.ShapeDtypeStruct((m, d), BF16))
        if router is not None:
            out_specs.append(pl.BlockSpec((tm, router.shape[1]), lambda i: (i, 0)))
            out_shape.append(jax.ShapeDtypeStruct((m, router.shape[1]), F32))
    body = functools.partial(_resid_norm_body, n_ctx_tiles=nct, x_split=x_split, has_y=y is not None,
                             want_x=want_x, out_split=out_split, want_h=want_h,
                             has_router=router is not None)
    return pl.pallas_call(
        body, grid=(m // tm,), in_specs=in_specs, out_specs=out_specs, out_shape=out_shape,
        compiler_params=_cparams(("arbitrary",), 14 * tm * d * 4 + d * LANES * 8),
        name="resid_norm",
    )(*args)


def _matmul_body(*refs, tn, scale, act, rope_shift, rope_pattern, seq_split, n_heads):
    x_ref, w_ref = refs[0], refs[1]
    o_ref = refs[-1]
    acc = jnp.dot(x_ref[...], w_ref[...], preferred_element_type=F32)
    if scale is not None:
        acc = acc * scale
    if act == "sigmoid":
        acc = jax.nn.sigmoid(acc)
    if seq_split is not None:
        refs[-2][...] = acc.astype(BF16)
        dh = tn // n_heads
        for b in range(acc.shape[0] // seq_split):
            for h in range(n_heads):
                o_ref[b, :, h, :] = acc[b * seq_split:(b + 1) * seq_split, h * dh:(h + 1) * dh]
        return
    if rope_shift is None:
        o_ref[...] = acc.astype(o_ref.dtype)
        return
    cos, sa, sb = refs[2][...], refs[3][...], refs[4][...]
    for g in range(tn // LANES):
        a = acc[:, g * LANES:(g + 1) * LANES]
        if rope_pattern[g % len(rope_pattern)]:
            a = _rope128(a, cos, sa, sb, rope_shift)
        o_ref[:, g * LANES:(g + 1) * LANES] = a.astype(o_ref.dtype)


def _matmul(x, w, *, rows, row0, col0, ncols, wsel, out_dtype, tm, tn, scale=None, act=None,
            rope=None, state=None, name="matmul"):
    k = x.shape[1]
    assert row0 % tm == 0 and rows % tm == 0 and col0 % tn == 0 and ncols % tn == 0, (row0, rows, col0, ncols, tm, tn)
    rb0, cb0 = row0 // tm, col0 // tn
    in_specs = [pl.BlockSpec((tm, k), lambda j, i: (rb0 + i, 0)),
                pl.BlockSpec((None, k, tn), lambda j, i: (wsel, 0, cb0 + j))]
    args = [x, w]
    rope_shift = rope_pattern = None
    if rope is not None:
        tabs, rope_shift, rope_pattern = rope
        nper = tabs[0].shape[0] // tm
        assert tabs[0].shape[0] % tm == 0
        for t in tabs:
            in_specs.append(pl.BlockSpec((tm, LANES), lambda j, i: (i % nper, 0)))
            args.append(t)
    aliases, seq_split, n_heads = {}, None, None
    out_spec = pl.BlockSpec((tm, tn), lambda j, i: (i, j))
    out_shape = jax.ShapeDtypeStruct((rows, ncols), out_dtype)
    if state is not None:
        prev, li, depth, seq_split, n_heads = state
        assert tm % seq_split == 0 and rope is None and tn == ncols and out_dtype == BF16
        dh = ncols // n_heads
        out_spec = [out_spec, pl.BlockSpec((tm // seq_split, None, seq_split, n_heads, dh),
                                           lambda j, i: (i, li, 0, 0, 0))]
        out_shape = [out_shape, jax.ShapeDtypeStruct((rows // seq_split, depth, seq_split, n_heads, dh), F32)]
        if prev is not None:
            in_specs.append(pl.BlockSpec(memory_space=pl.ANY))
            aliases = {len(args): 1}
            args.append(prev)
    osz = jnp.dtype(out_dtype).itemsize
    est = 2 * (tm * k * 2 + k * tn * 2 + tm * tn * osz) + 3 * tm * tn * 4 + 6 * tm * LANES * 4
    body = functools.partial(_matmul_body, tn=tn, scale=scale, act=act, rope_shift=rope_shift,
                             rope_pattern=rope_pattern, seq_split=seq_split, n_heads=n_heads)
    return pl.pallas_call(
        body, grid=(ncols // tn, rows // tm), in_specs=in_specs, out_specs=out_spec, out_shape=out_shape,
        input_output_aliases=aliases,
        compiler_params=_cparams(("arbitrary", "arbitrary"), est),
        name=name,
    )(*args)


def _mla_prep_body(zm_ref, kr_ref, qg_ref, kg_ref, cos_ref, sa_ref, sb_ref, *rest, q_lora, kv_lora,
                   n_ctx_tiles, seq):
    cq_o, ckv_o, ckr_o = rest[-3:]
    z = zm_ref[...]
    cq_o[...] = (_rms(z[:, :q_lora]) * qg_ref[...]).astype(BF16)
    ckv = _rms(z[:, q_lora:]) * kg_ref[...]
    ckr_o[:, :kv_lora] = ckv.astype(BF16)
    i = pl.program_id(0)
    lane = lax.broadcasted_iota(jnp.int32, kr_ref.shape, 1)
    kr = jnp.where(lane < ROPE_B, kr_ref[...], 0.0)

    @pl.when(i < n_ctx_tiles)
    def _():
        ckv_o[...] = ckv.reshape(ckv.shape[0] // seq, seq, kv_lora)
        ckr_o[:, kv_lora:] = kr.astype(BF16)

    @pl.when(i >= n_ctx_tiles)
    def _():
        ckr_o[:, kv_lora:] = _rope128(kr, cos_ref[...], sa_ref[...], sb_ref[...], ROPE_B // 4).astype(BF16)


def _mla_prep(zm, kr, qg, kg, tabs, state, *, q_lora, kv_lora, m_ctx, seg_len):
    prev, li, depth, seq = state[:4]
    m = zm.shape[0]
    tm = _pick(math.gcd(m_ctx, seg_len), 512)
    assert tm % seq == 0
    nct, nper = m_ctx // tm, seg_len // tm
    tab_spec = pl.BlockSpec((tm, LANES), lambda i: (jnp.maximum(i - nct, 0) % nper, 0))
    in_specs = [pl.BlockSpec((tm, q_lora + kv_lora), lambda i: (i, 0)),
                pl.BlockSpec((tm, LANES), lambda i: (i, 0)),
                pl.BlockSpec((1, q_lora), lambda i: (0, 0)),
                pl.BlockSpec((1, kv_lora), lambda i: (0, 0)),
                tab_spec, tab_spec, tab_spec]
    args = [zm, kr, qg, kg, *tabs]
    aliases = {}
    if prev is not None:
        in_specs.append(pl.BlockSpec(memory_space=pl.ANY))
        aliases = {len(args): 1}
        args.append(prev)
    body = functools.partial(_mla_prep_body, q_lora=q_lora, kv_lora=kv_lora, n_ctx_tiles=nct, seq=seq)
    return pl.pallas_call(
        body, grid=(m // tm,), in_specs=in_specs,
        out_specs=[pl.BlockSpec((tm, q_lora), lambda i: (i, 0)),
                   pl.BlockSpec((tm // seq, None, seq, kv_lora), lambda i: (jnp.minimum(i, nct - 1), li, 0, 0)),
                   pl.BlockSpec((tm, kv_lora + LANES), lambda i: (i, 0))],
        out_shape=[jax.ShapeDtypeStruct((m, q_lora), BF16),
                   jax.ShapeDtypeStruct((m_ctx // seq, depth, seq, kv_lora), F32),
                   jax.ShapeDtypeStruct((m, kv_lora + LANES), BF16)],
        input_output_aliases=aliases,
        compiler_params=_cparams(("arbitrary",), 8 * tm * (q_lora + kv_lora + 4 * LANES) * 4),
        name="mla_prep",
    )(*args)


_NT = (((1,), (1,)), ((), ()))


def _softmax_parts(q, k):
    s = lax.dot_general(q, k, _NT, preferred_element_type=F32)
    p = jnp.exp2(s - jnp.max(s, axis=-1, keepdims=True))
    return p, 1.0 / jnp.sum(p, axis=-1, keepdims=True)


def _one_head(q, k, v, *, diff, lam, g, out_scale):
    k = k.astype(BF16)
    v = v.astype(BF16)
    if not diff:
        p, r = _softmax_parts(q, k)
        return jnp.dot(p.astype(BF16), v, preferred_element_type=F32) * r
    p1, r1 = _softmax_parts(q[:, :HD_A], k[:, :HD_A])
    p2, r2 = _softmax_parts(q[:, HD_A:], k[:, HD_A:])
    p = (p1 * r1 - p2 * (lam * r2)).astype(BF16)
    o = jnp.dot(p, v, preferred_element_type=F32)
    return _rms(o) * g * out_scale


def _ctx_attn_body(*refs, nb, nh, dq, dv, diff, out_scale):
    it = iter(refs)
    lam = next(it)[0] if diff else None
    q_ref, k_ref, v_ref = next(it), next(it), next(it)
    g = next(it)[...] if diff else None
    o_ref = refs[-1]
    s = q_ref.shape[1]
    for b in range(nb):
        for h in range(nh):
            o = _one_head(q_ref[b, :, h * dq:(h + 1) * dq], k_ref[b, :, h * dq:(h + 1) * dq],
                          v_ref[b, :, h * dv:(h + 1) * dv], diff=diff, lam=lam, g=g, out_scale=out_scale)
            o_ref[b * s:(b + 1) * s, h * dv:(h + 1) * dv] = o.astype(o_ref.dtype)


def _ctx_attn(q, k, v, *, li, nh, dq, dv, out_rows, diff=False, lam=None, g=None, out_scale=None, name):
    bsz, s, _ = q.shape
    nb = 2 if bsz % 2 == 0 else 1

    def kv_spec(a, width):
        if a.ndim == 4:
            return pl.BlockSpec((nb, None, s, width), lambda i: (i, li, 0, 0))
        return pl.BlockSpec((nb, s, width), lambda i: (i, 0, 0))

    in_specs, args = [], []
    if diff:
        in_specs.append(pl.BlockSpec(memory_space=pltpu.SMEM))
        args.append(lam)
    in_specs += [pl.BlockSpec((nb, s, nh * dq), lambda i: (i, 0, 0)), kv_spec(k, nh * dq), kv_spec(v, nh * dv)]
    args += [q, k, v]
    if diff:
        in_specs.append(pl.BlockSpec((1, dv), lambda i: (0, 0)))
        args.append(g)
    body = functools.partial(_ctx_attn_body, nb=nb, nh=nh, dq=dq, dv=dv, diff=diff, out_scale=out_scale)
    return pl.pallas_call(
        body, grid=(bsz // nb,), in_specs=in_specs,
        out_specs=pl.BlockSpec((nb * s, nh * dv), lambda i: (i, 0)),
        out_shape=jax.ShapeDtypeStruct((out_rows, nh * dv), BF16),
        compiler_params=_cparams(("arbitrary",), 4 * nb * s * nh * (2 * dq + dv) * 4 + 16 * s * s * 4 * nb * nh),
        name=name,
    )(*args)


def _softmax_parts_t(k, q):
    st = lax.dot_general(k, q, _NT, preferred_element_type=F32)
    p = jnp.exp2(st - jnp.max(st, axis=0, keepdims=True))
    return p.astype(BF16), 1.0 / jnp.sum(p, axis=0, keepdims=True)


def _lat_attn_t_body(*refs, hps, dq, dv, diff, out_scale):
    it = iter(refs)
    lam = next(it)[0] if diff else None
    q_ref, k_ref, vt_ref = next(it), next(it), next(it)
    g = next(it)[...] if diff else None
    o_ref = refs[-1]
    if diff:
        q, k, vt = q_ref[...], k_ref[...], vt_ref[...]
        p1, r1 = _softmax_parts_t(k[:, :HD_A], q[:, :HD_A])
        p2, r2 = _softmax_parts_t(k[:, HD_A:], q[:, HD_A:])
        ot = (jnp.dot(vt, p1, preferred_element_type=F32) * r1
              - jnp.dot(vt, p2, preferred_element_type=F32) * (lam * r2))
        o_ref[...] = (_rms(ot.T) * g * out_scale).astype(o_ref.dtype)
        return
    ps = [_softmax_parts_t(k_ref[:, h * dq:(h + 1) * dq], q_ref[:, h * dq:(h + 1) * dq]) for h in range(hps)]
    for h, (p, r) in enumerate(ps):
        ot = jnp.dot(vt_ref[h * dv:(h + 1) * dv, :], p, preferred_element_type=F32) * r
        o_ref[:, h * dv:(h + 1) * dv] = ot.T.astype(o_ref.dtype)


def _lat_attn_body(*refs, hps, dq, dv, diff, out_scale):
    it = iter(refs)
    lam = next(it)[0] if diff else None
    q_ref, k_ref, v_ref = next(it), next(it), next(it)
    g = next(it)[...] if diff else None
    o_ref = refs[-1]
    for h in range(hps):
        o = _one_head(q_ref[:, h * dq:(h + 1) * dq], k_ref[:, h * dq:(h + 1) * dq],
                      v_ref[:, h * dv:(h + 1) * dv], diff=diff, lam=lam, g=g, out_scale=out_scale)
        o_ref[:, h * dv:(h + 1) * dv] = o.astype(o_ref.dtype)


def _lat_attn(q, k, v, prev_out, *, nh, hps, dq, dv, tq, out_row0, diff=False, lam=None, g=None,
              out_scale=None, v_transposed=False, name):
    bsz, n, _ = q.shape
    nk = k.shape[1]
    assert nh % hps == 0 and n % tq == 0 and out_row0 % tq == 0
    in_specs, args = [], []
    if diff:
        in_specs.append(pl.BlockSpec(memory_space=pltpu.SMEM))
        args.append(lam)
    v_spec = (pl.BlockSpec((None, hps * dv, nk), lambda b, h, i: (b, h, 0)) if v_transposed
              else pl.BlockSpec((None, nk, hps * dv), lambda b, h, i: (b, 0, h)))
    in_specs += [pl.BlockSpec((None, tq, hps * dq), lambda b, h, i: (b, i, h)),
                 pl.BlockSpec((None, nk, hps * dq), lambda b, h, i: (b, 0, h)), v_spec]
    args += [q, k, v]
    if diff:
        in_specs.append(pl.BlockSpec((1, dv), lambda b, h, i: (0, 0)))
        args.append(g)
    in_specs.append(pl.BlockSpec(memory_space=pl.ANY))
    aliases = {len(args): 0}
    args.append(prev_out)
    orb0, nqb = out_row0 // tq, n // tq
    n_chain = hps * (2 if diff else 1)
    est = 2 * (tq * hps * dq + nk * hps * (dq + dv)) * 2 + 5 * n_chain * tq * nk * 4
    if v_transposed:
        assert hps == 1 or not diff
        body = functools.partial(_lat_attn_t_body, hps=hps, dq=dq, dv=dv, diff=diff, out_scale=out_scale)
    else:
        body = functools.partial(_lat_attn_body, hps=hps, dq=dq, dv=dv, diff=diff, out_scale=out_scale)
    return pl.pallas_call(
        body, grid=(bsz, nh // hps, nqb), in_specs=in_specs,
        out_specs=pl.BlockSpec((tq, hps * dv), lambda b, h, i: (orb0 + b * nqb + i, h)),
        out_shape=jax.ShapeDtypeStruct(prev_out.shape, prev_out.dtype),
        input_output_aliases=aliases,
        compiler_params=_cparams(("arbitrary", "arbitrary", "arbitrary"), est),
        name=name,
    )(*args)


def _na_geometry(rows):
    rpb = NA_ROWS_PER_BLOCK
    kr = min(NA_KR_MAX, rows)
    span = rpb + kr - 1
    assert rows % rpb == 0 and rows >= span, rows
    nblk = rows // rpb
    r = np.arange(rows)
    row_start = np.clip(r - kr // 2, 0, rows - kr)
    blk_start = np.clip(rpb * np.arange(nblk) - kr // 2, 0, rows - span)
    cols = np.arange(GRID_W)
    col_start = np.clip(cols - NA_KC // 2, 0, GRID_W - NA_KC)
    dc = cols[None, :] - cols[:, None] + (NA_KC - 1)
    ok_c = (cols[None, :] >= col_start[:, None]) & (cols[None, :] < col_start[:, None] + NA_KC)
    sigs, cls_of_blk, dr_l, ok_l = {}, [], [], []
    for b in range(nblk):
        qr = rpb * b + np.arange(rpb)
        key_row = blk_start[b] + np.arange(span)
        dr = key_row[None, :] - qr[:, None] + (NA_KR_MAX - 1)
        ok_r = (key_row[None, :] >= row_start[qr][:, None]) & (key_row[None, :] < row_start[qr][:, None] + kr)
        sig = (dr.tobytes(), ok_r.tobytes())
        if sig not in sigs:
            sigs[sig] = len(sigs)
            dr_l.append(np.clip(dr, 0, 2 * NA_KR_MAX - 2))
            ok = ok_r[:, None, :, None] & ok_c[None, :, None, :]
            ok_l.append(ok.reshape(rpb * GRID_W, span * GRID_W))
        cls_of_blk.append(sigs[sig])
    return dict(span=span, nblk=nblk, blk_start=blk_start.astype(np.int32), cls=np.asarray(cls_of_blk, np.int32),
                dr=np.stack(dr_l), dc=np.clip(dc, 0, 2 * NA_KC - 2), ok=np.stack(ok_l))


def _na_bias(tables, na):
    depth, nh, n_dr, n_dc = tables.shape
    ncls, rpb, span = na["dr"].shape
    oh_r = jax.nn.one_hot(na["dr"].reshape(-1), n_dr, dtype=F32)
    oh_c = jax.nn.one_hot(na["dc"].reshape(-1), n_dc, dtype=F32).T
    t = jnp.einsum("xr,lhrc->lhxc", oh_r, tables.astype(F32), precision=HIGHEST)
    t = jnp.einsum("lhxc,cy->lhxy", t, oh_c, precision=HIGHEST)
    t = t.reshape(depth, nh, ncls, rpb, span, GRID_W, GRID_W).transpose(0, 1, 2, 3, 5, 4, 6)
    t = t.reshape(depth, nh, ncls, rpb * GRID_W, span * GRID_W)
    return jnp.where(na["ok"][None, None], t, MASK_VALUE)


def _na_body(cls_ref, start_ref, q_ref, k_ref, v_ref, kc_ref, vc_ref, bias_ref, prev_ref, o_ref, *, span_tok):
    del cls_ref, prev_ref
    blk = pl.program_id(2)
    start = pl.multiple_of(start_ref[blk] * GRID_W, GRID_W)
    q = q_ref[...]
    kl = k_ref[pl.ds(start, span_tok), :]
    vl = v_ref[pl.ds(start, span_tok), :]
    s_loc = lax.dot_general(q, kl, _NT, preferred_element_type=F32) + bias_ref[...]
    s_ctx = lax.dot_general(q, kc_ref[...].astype(BF16), _NT, preferred_element_type=F32)
    mx = jnp.maximum(jnp.max(s_loc, axis=-1, keepdims=True), jnp.max(s_ctx, axis=-1, keepdims=True))
    p_loc = jnp.exp2(s_loc - mx)
    p_ctx = jnp.exp2(s_ctx - mx)
    r = 1.0 / (jnp.sum(p_loc, axis=-1, keepdims=True) + jnp.sum(p_ctx, axis=-1, keepdims=True))
    o = jnp.dot(p_loc.astype(BF16), vl, preferred_element_type=F32)
    o = o + jnp.dot(p_ctx.astype(BF16), vc_ref[...].astype(BF16), preferred_element_type=F32)
    o_ref[...] = (o * r).astype(o_ref.dtype)


def _na_lat_attn(q, kv, cache_k, cache_v, bias, na, prev_out, *, li, out_row0):
    bsz, n, _ = q.shape
    past = cache_k.shape[2]
    rpb_tok, span_tok, nblk = NA_ROWS_PER_BLOCK * GRID_W, na["span"] * GRID_W, na["nblk"]
    gs = pltpu.PrefetchScalarGridSpec(
        num_scalar_prefetch=2, grid=(bsz, H_C, nblk),
        in_specs=[pl.BlockSpec((None, rpb_tok, HD_C), lambda b, h, i, cl, st: (b, i, h)),
                  pl.BlockSpec((None, n, HD_C), lambda b, h, i, cl, st: (b, 0, h)),
                  pl.BlockSpec((None, n, HD_C), lambda b, h, i, cl, st: (b, 0, H_C + h)),
                  pl.BlockSpec((None, None, past, HD_C), lambda b, h, i, cl, st: (b, li, 0, h)),
                  pl.BlockSpec((None, None, past, HD_C), lambda b, h, i, cl, st: (b, li, 0, h)),
                  pl.BlockSpec((None, None, None, rpb_tok, span_tok), lambda b, h, i, cl, st: (li, h, cl[i], 0, 0)),
                  pl.BlockSpec(memory_space=pl.ANY)],
        out_specs=pl.BlockSpec((rpb_tok, HD_C), lambda b, h, i, cl, st: (out_row0 // rpb_tok + b * nblk + i, h)))
    return pl.pallas_call(
        functools.partial(_na_body, span_tok=span_tok),
        grid_spec=gs, out_shape=jax.ShapeDtypeStruct(prev_out.shape, prev_out.dtype),
        input_output_aliases={8: 0},
        compiler_params=_cparams(("arbitrary", "arbitrary", "arbitrary"),
                                 8 * n * HD_C * 2 + 10 * rpb_tok * span_tok * 4),
        name="na_attn_lat",
    )(jnp.asarray(na["cls"]), jnp.asarray(na["blk_start"]), q, kv, kv, cache_k, cache_v, bias, prev_out)


def _merge_body(a_ref, b_ref, c_ref, ga_ref, gb_ref, gc_ref, wa_ref, wb_ref, wc_ref, o_ref):
    acc = ga_ref[...].astype(F32) * jnp.dot(a_ref[...], wa_ref[...], preferred_element_type=F32)
    acc += gb_ref[...].astype(F32) * jnp.dot(b_ref[...], wb_ref[...], preferred_element_type=F32)
    acc += gc_ref[...].astype(F32) * jnp.dot(c_ref[...], wc_ref[...], preferred_element_type=F32)
    o_ref[...] = acc.astype(o_ref.dtype)


def _merge(oa, ob, oc, gates, wb, li):
    m, kb = oa.shape
    d = wb.shape[-1]
    tm, tn = _pick(m, 1024), _pick(d, 1024, 256)
    nj = d // tn
    br = pl.BlockSpec((tm, kb), lambda j, i: (i, 0))
    gspec = [pl.BlockSpec((tm, tn), functools.partial(lambda j, i, s: (i, s * nj + j), s=s)) for s in range(N_BRANCH)]
    wspec = [pl.BlockSpec((None, None, kb, tn), functools.partial(lambda j, i, s: (li, s, 0, j), s=s)) for s in range(N_BRANCH)]
    est = 2 * (3 * tm * kb * 2 + 3 * tm * tn * 2 + 3 * kb * tn * 2 + tm * tn * 2) + 4 * tm * tn * 4
    return pl.pallas_call(
        _merge_body, grid=(nj, m // tm), in_specs=[br, br, br] + gspec + wspec,
        out_specs=pl.BlockSpec((tm, tn), lambda j, i: (i, j)),
        out_shape=jax.ShapeDtypeStruct((m, d), BF16),
        compiler_params=_cparams(("arbitrary", "arbitrary"), est),
        name="branch_merge",
    )(oa, ob, oc, gates, gates, gates, wb, wb, wb)


def _ffn1_body(te_ref, ts_ref, nu_ref, x_ref, w13_ref, o_ref):
    del te_ref, ts_ref

    @pl.when(pl.program_id(1) < nu_ref[0])
    def _():
        tf = o_ref.shape[1]
        h = jnp.dot(x_ref[...], w13_ref[...], preferred_element_type=F32)
        h1, h3 = h[:, :tf], h[:, tf:]
        o_ref[...] = (h1 * jax.nn.sigmoid(h1) * h3).astype(o_ref.dtype)


def _ffn2_body(te_ref, ts_ref, nu_ref, h_ref, w2_ref, o_ref):
    del te_ref, ts_ref

    @pl.when(pl.program_id(1) < nu_ref[0])
    def _():
        o_ref[...] = jnp.dot(h_ref[...], w2_ref[...], preferred_element_type=F32).astype(o_ref.dtype)


def _ffn_tile(f):
    return _pick(f, 1408, LANES)


def _cast_interleave_body(w1_ref, w3_ref, o_ref):
    tf = w1_ref.shape[1]
    o_ref[:, :tf] = w1_ref[...].astype(BF16)
    o_ref[:, tf:] = w3_ref[...].astype(BF16)


def _cast_interleave(w1, w3):
    nl, ne, d, f = w1.shape
    tf = _ffn_tile(f)
    tk = _pick(d, 1024)
    blk = pl.BlockSpec((None, None, tk, tf), lambda l, e, k, j: (l, e, k, j))
    return pl.pallas_call(
        _cast_interleave_body, grid=(nl, ne, d // tk, f // tf), in_specs=[blk, blk],
        out_specs=pl.BlockSpec((None, None, tk, 2 * tf), lambda l, e, k, j: (l, e, k, j)),
        out_shape=jax.ShapeDtypeStruct((nl, ne, d, 2 * f), BF16),
        compiler_params=_cparams(("arbitrary",) * 4, 2 * (2 * tk * tf * 4 + tk * 2 * tf * 2) + 2 * tk * tf * 4),
        name="cast_interleave",
    )(w1, w3)


def _grouped_ffn(xs, w13, w2, lsel, tile_expert, tile_src, n_used, tm):
    r, d = xs.shape
    f = w2.shape[2]
    nt = r // tm
    tf = _ffn_tile(f)
    tn = _pick(d, 1024, 256)
    gs1 = pltpu.PrefetchScalarGridSpec(
        num_scalar_prefetch=3, grid=(f // tf, nt),
        in_specs=[pl.BlockSpec((tm, d), lambda j, t, te, ts, nu: (ts[t], 0)),
                  pl.BlockSpec((None, None, d, 2 * tf), lambda j, t, te, ts, nu: (lsel, te[t], 0, j))],
        out_specs=pl.BlockSpec((tm, tf), lambda j, t, te, ts, nu: (ts[t], j)))
    hid = pl.pallas_call(
        _ffn1_body, grid_spec=gs1, out_shape=jax.ShapeDtypeStruct((r, f), BF16),
        compiler_params=_cparams(("arbitrary", "arbitrary"),
                                 2 * (tm * d * 2 + 2 * d * tf * 2 + tm * tf * 2) + 4 * tm * tf * 4),
        name="ffn_up",
    )(tile_expert, tile_src, n_used, xs, w13)
    gs2 = pltpu.PrefetchScalarGridSpec(
        num_scalar_prefetch=3, grid=(d // tn, nt),
        in_specs=[pl.BlockSpec((tm, f), lambda j, t, te, ts, nu: (ts[t], 0)),
                  pl.BlockSpec((None, None, f, tn), lambda j, t, te, ts, nu: (lsel, te[t], 0, j))],
        out_specs=pl.BlockSpec((tm, tn), lambda j, t, te, ts, nu: (ts[t], j)))
    return pl.pallas_call(
        _ffn2_body, grid_spec=gs2, out_shape=jax.ShapeDtypeStruct((r, d), F32),
        compiler_params=_cparams(("arbitrary", "arbitrary"),
                                 2 * (tm * f * 2 + f * tn * 2 + tm * tn * 4) + 2 * tm * tn * 4),
        name="ffn_down",
    )(tile_expert, tile_src, n_used, hid, w2)


def _moe_plan(logits, tm):
    m = logits.shape[0]
    top_v, top_i = lax.top_k(logits, TOP_K)
    top_w = jax.nn.softmax(top_v, axis=-1)
    flat_e = top_i.reshape(-1)
    onehot = (flat_e[:, None] == jnp.arange(N_EXPERTS)[None, :]).astype(jnp.int32)
    csum = jnp.cumsum(onehot, axis=0)
    counts = csum[-1]
    rank = jnp.take_along_axis(csum, flat_e[:, None], axis=1)[:, 0] - 1
    padded = ((counts + tm - 1) // tm) * tm
    ends = jnp.cumsum(padded)
    offs = ends - padded
    dest = offs[flat_e] + rank
    n_rows = TOP_K * m + N_EXPERTS * tm
    nt = n_rows // tm
    row_token = jnp.zeros((n_rows,), jnp.int32).at[dest].set(jnp.arange(TOP_K * m, dtype=jnp.int32) // TOP_K)
    n_used = (ends[-1] // tm).astype(jnp.int32)
    tile_src = jnp.minimum(jnp.arange(nt, dtype=jnp.int32), n_used - 1)
    tile_expert = jnp.sum((tile_src[:, None] * tm >= ends[None, :]).astype(jnp.int32), axis=1)
    tile_expert = jnp.minimum(tile_expert, N_EXPERTS - 1).astype(jnp.int32)
    return row_token, dest.reshape(m, TOP_K), top_w, tile_expert, tile_src, n_used.reshape(1)


def _rows(a, idx):
    return a.at[idx].get(mode="promise_in_bounds")


def _rope_tables(n_tok, dim):
    half = dim // 2
    t = jnp.arange(n_tok)
    row = (t // GRID_W).astype(F32)
    col = (t % GRID_W).astype(F32)
    inv = ROPE_THETA ** (-jnp.arange(0, half, 2, dtype=F32) / half)
    ar = row[:, None] * inv[None, :]
    ac = col[:, None] * inv[None, :]
    ang = jnp.concatenate([ar, ar, ac, ac], axis=-1)
    cos, sin = jnp.cos(ang), jnp.sin(ang)
    first = (np.arange(dim) % half) < (dim // 4)
    sa = jnp.where(first[None, :], -sin, 0.0)
    sb = jnp.where(first[None, :], 0.0, sin)
    pad = LANES - dim
    if pad:
        cos = jnp.pad(cos, ((0, 0), (0, pad)), constant_values=1.0)
        sa = jnp.pad(sa, ((0, 0), (0, pad)))
        sb = jnp.pad(sb, ((0, 0), (0, pad)))
    return cos, sa, sb


def kernel(x_prompt, x_sample, cache_diff_k, cache_diff_v, cache_mla_ckv, cache_mla_krope, cache_na_k, cache_na_v, c, c_ctx, norm_g, ada_w, ada_b, w_in, diff_lambda, diff_subln_g, mla_q_norm_g, mla_kv_norm_g, mla_w_uq, mla_w_ukv, na_rel_bias, w_branch, w_out, ffn_w1, ffn_w3, ffn_w2, moe_router, moe_w1, moe_w3, moe_w2):
    batch, seq, d = x_prompt.shape
    dec_batch, dec_seq, _ = x_sample.shape
    depth = norm_g.shape[0]
    past = cache_diff_k.shape[2]
    q_lora, kv_lora = mla_q_norm_g.shape[1], mla_kv_norm_g.shape[1]
    m_ctx, m_lat = batch * seq, dec_batch * dec_seq
    m = m_ctx + m_lat
    nk_lat = dec_seq + past
    assert dec_seq % GRID_W == 0 and kv_lora % LANES == 0 and q_lora % LANES == 0

    a3 = 3 * BR_W
    col_m = a3
    col_kr = a3 + q_lora + kv_lora
    c_na = col_kr + ROPE_B
    n_rest = 3 * BR_W + N_BRANCH * d
    w_head = _cast_head(w_in, col_kr + LANES)
    w_rest = _cast_shift(w_in, c_na, n_rest, 768)
    col_c, col_g = 0, 3 * BR_W
    wq = mla_w_uq.reshape(depth, q_lora, H_B, NOPE_B + ROPE_B)
    w_uqp = jnp.pad(wq, ((0, 0), (0, 0), (0, 0), (0, 2 * LANES - NOPE_B - ROPE_B))).reshape(depth, q_lora, H_B * 2 * LANES).astype(BF16)
    wkv = mla_w_ukv.reshape(depth, kv_lora, H_B, NOPE_B + V_B)
    wk_top = jnp.pad(wkv[..., :NOPE_B], ((0, 0), (0, 0), (0, 0), (0, 2 * LANES - NOPE_B))).reshape(depth, kv_lora, H_B * 2 * LANES)
    eye = np.zeros((LANES, H_B, 2 * LANES), np.float32)
    for r_ in range(ROPE_B):
        eye[r_, :, NOPE_B + r_] = 1.0
    wk_aug = jnp.concatenate([wk_top, jnp.broadcast_to(jnp.asarray(eye.reshape(LANES, -1)), (depth, LANES, H_B * 2 * LANES))], axis=1).astype(BF16)
    wv_aug = jnp.pad(wkv[..., NOPE_B:].reshape(depth, kv_lora, H_B * V_B), ((0, 0), (0, LANES), (0, 0))).astype(BF16)
    w_branch_b = w_branch.astype(BF16)
    w_out_b = w_out.astype(BF16)
    ffn_w13b, ffn_w2b = _cast_interleave(ffn_w1[:, None], ffn_w3[:, None]), ffn_w2.astype(BF16)[:, None]
    moe_w13b, moe_w2b = _cast_interleave(moe_w1, moe_w3), moe_w2.astype(BF16)
    router_p = jnp.pad(moe_router, ((0, 0), (0, 0), (0, LANES - N_EXPERTS)))
    norm_g4 = norm_g.reshape(depth, 4, 1, d)

    n_seg = 1 + dec_batch
    r_pad = -(-n_seg // 8) * 8
    cvec = jnp.zeros((r_pad, d), F32).at[0].set(c_ctx).at[1:n_seg].set(c)
    mods = _ada_mod(cvec, ada_w, ada_b).reshape(depth, r_pad, 6, 1, d)

    tabs_a = _rope_tables(dec_seq, HD_A)
    tabs_b = _rope_tables(dec_seq, ROPE_B)
    na = _na_geometry(dec_seq // GRID_W)
    na_bias = _na_bias(na_rel_bias, na)
    lam_inits = [0.8 - 0.6 * math.exp(-0.3 * li) for li in range(depth)]
    lp = diff_lambda.astype(F32)
    lams = jnp.exp(jnp.sum(lp[:, 0] * lp[:, 1], axis=-1)) - jnp.exp(jnp.sum(lp[:, 2] * lp[:, 3], axis=-1)) + jnp.asarray(lam_inits, F32)

    tm_big = _pick(math.gcd(m_ctx, dec_seq), 1024)
    assert tm_big % seq == 0
    tq_lat = _pick(dec_seq, 512)
    rn = functools.partial(_resid_norm, m_ctx=m_ctx, seg_len=dec_seq)
    mm = functools.partial(_matmul, tm=tm_big)
    qs_a, qs_b, qs_c = HD_A ** -0.5 * LOG2E, (NOPE_B + ROPE_B) ** -0.5 * LOG2E, HD_C ** -0.5 * LOG2E
    hb2 = H_B * 2 * LANES
    cache_k_c = cache_na_k.reshape(dec_batch, depth, past, H_C * HD_C)
    cache_v_c = cache_na_v.reshape(dec_batch, depth, past, H_C * HD_C)

    x = (x_prompt.reshape(m_ctx, d), x_sample.reshape(m_lat, d))
    (hmix,) = rn(x, None, mods, norm_g4, li_post=0, k_gate=2, li_pre=0, k_mod=0, want_x=False, want_h=True)
    st_dk = st_dv = st_ckv = st_nk = st_nv = None
    st_kr = []

    for li in range(depth):
        def st(prev, n_heads=None):
            return (prev, li, depth, seq, n_heads)

        def b3(a):
            return a.reshape(batch, seq, a.shape[-1])

        ctx = dict(rows=m_ctx, row0=0, wsel=li, tn=1024)
        lat = dict(rows=m_lat, row0=m_ctx, wsel=li, tn=1024)
        qa_ctx = mm(hmix, w_head, col0=0, ncols=BR_W, out_dtype=BF16, scale=qs_a, name="in_qa_ctx", **ctx)
        ka_ctx, st_dk = mm(hmix, w_head, col0=BR_W, ncols=BR_W, out_dtype=BF16, state=st(st_dk, H_A), name="in_ka_ctx", **ctx)
        va_ctx, st_dv = mm(hmix, w_head, col0=2 * BR_W, ncols=BR_W, out_dtype=BF16, state=st(st_dv, H_A), name="in_va_ctx", **ctx)
        qa_lat = mm(hmix, w_head, col0=0, ncols=BR_W, out_dtype=BF16, scale=qs_a,
                    rope=(tabs_a, HD_A // 4, (True,)), name="in_qa_lat", **lat)
        ka_lat = mm(hmix, w_head, col0=BR_W, ncols=BR_W, out_dtype=BF16,
                    rope=(tabs_a, HD_A // 4, (True,)), name="in_ka_lat", **lat)
        va_lat = mm(hmix, w_head, col0=2 * BR_W, ncols=BR_W, out_dtype=BF16, name="in_va_lat", **lat)
        qc_ctx = mm(hmix, w_rest, col0=col_c, ncols=BR_W, out_dtype=BF16, scale=qs_c, name="in_qc_ctx", **ctx)
        kc_ctx, st_nk = mm(hmix, w_rest, col0=col_c + BR_W, ncols=BR_W, out_dtype=BF16, state=st(st_nk, H_C), name="in_kc_ctx", **ctx)
        vc_ctx, st_nv = mm(hmix, w_rest, col0=col_c + 2 * BR_W, ncols=BR_W, out_dtype=BF16, state=st(st_nv, H_C), name="in_vc_ctx", **ctx)
        qc_lat = mm(hmix, w_rest, col0=col_c, ncols=BR_W, out_dtype=BF16, scale=qs_c, name="in_qc_lat", **lat)
        kvc_lat = mm(hmix, w_rest, col0=col_c + BR_W, ncols=2 * BR_W, out_dtype=BF16, name="in_kvc_lat", **lat)
        gates = mm(hmix, w_rest, rows=m, row0=0, col0=col_g, ncols=N_BRANCH * d, wsel=li, out_dtype=BF16, tn=768,
                   act="sigmoid", name="in_gates")
        zm = mm(hmix, w_head, rows=m, row0=0, col0=col_m, ncols=q_lora + kv_lora, wsel=li, out_dtype=F32,
                tn=q_lora + kv_lora, name="in_mla")
        kr = mm(hmix, w_head, rows=m, row0=0, col0=col_kr, ncols=LANES, wsel=li, out_dtype=F32, tn=LANES, name="in_krope")
        st_kr.append(kr[:m_ctx, :ROPE_B].reshape(batch, seq, ROPE_B))

        g_sub = diff_subln_g[li].reshape(1, 2 * HD_A)
        lam = lams[li].reshape(1)
        oa = _ctx_attn(b3(qa_ctx), b3(ka_ctx), b3(va_ctx), li=li, nh=H_A, dq=2 * HD_A, dv=2 * HD_A,
                       out_rows=m, diff=True, lam=lam, g=g_sub, out_scale=1.0 - lam_inits[li], name="diff_attn_ctx")
        ka_all = jnp.concatenate([ka_lat.reshape(dec_batch, dec_seq, BR_W),
                                  cache_diff_k[:, li].reshape(dec_batch, past, BR_W).astype(BF16)], axis=1)
        va_all = jnp.concatenate([va_lat.reshape(dec_batch, dec_seq, BR_W),
                                  cache_diff_v[:, li].reshape(dec_batch, past, BR_W).astype(BF16)], axis=1)
        oa = _lat_attn(qa_lat.reshape(dec_batch, dec_seq, BR_W), ka_all, jnp.swapaxes(va_all, 1, 2), oa, nh=H_A,
                       hps=1, dq=2 * HD_A, dv=2 * HD_A, tq=tq_lat, out_row0=m_ctx, diff=True, lam=lam, g=g_sub,
                       out_scale=1.0 - lam_inits[li], v_transposed=True, name="diff_attn_lat")

        cqn, st_ckv, ckr = _mla_prep(zm, kr, mla_q_norm_g[li].reshape(1, q_lora), mla_kv_norm_g[li].reshape(1, kv_lora),
                                     tabs_b, st(st_ckv), q_lora=q_lora, kv_lora=kv_lora, m_ctx=m_ctx, seg_len=dec_seq)
        qb_ctx = mm(cqn, w_uqp, col0=0, ncols=hb2, out_dtype=BF16, scale=qs_b, name="mla_q_ctx", **ctx)
        qb_lat = mm(cqn, w_uqp, col0=0, ncols=hb2, out_dtype=BF16, scale=qs_b,
                    rope=(tabs_b, ROPE_B // 4, (False, True)), name="mla_q_lat", **lat)
        cache_ckr = jnp.concatenate([cache_mla_ckv[:, li], cache_mla_krope[:, li],
                                     jnp.zeros((dec_batch, past, LANES - ROPE_B), F32)], axis=-1).astype(BF16)
        ckr_lat = jnp.concatenate([ckr[m_ctx:].reshape(dec_batch, dec_seq, kv_lora + LANES), cache_ckr], axis=1)
        ckr_lat = ckr_lat.reshape(dec_batch * nk_lat, kv_lora + LANES)
        tm_kv = _pick(math.gcd(m_ctx, dec_batch * nk_lat), 1024)
        kv_mm = functools.partial(_matmul, row0=0, col0=0, wsel=li, out_dtype=BF16, tm=tm_kv, tn=1024)
        kb_ctx = kv_mm(ckr, wk_aug, rows=m_ctx, ncols=hb2, name="mla_k_ctx")
        vb_ctx = kv_mm(ckr, wv_aug, rows=m_ctx, ncols=H_B * V_B, name="mla_v_ctx")
        kb_lat = kv_mm(ckr_lat, wk_aug, rows=dec_batch * nk_lat, ncols=hb2, name="mla_k_lat")
        vb_lat = kv_mm(ckr_lat, wv_aug, rows=dec_batch * nk_lat, ncols=H_B * V_B, name="mla_v_lat")
        ob = _ctx_attn(qb_ctx.reshape(batch, seq, hb2), kb_ctx.reshape(batch, seq, hb2),
                       vb_ctx.reshape(batch, seq, H_B * V_B), li=li, nh=H_B, dq=2 * LANES, dv=V_B, out_rows=m,
                       name="mla_attn_ctx")
        vbt_lat = jnp.swapaxes(vb_lat.reshape(dec_batch, nk_lat, H_B * V_B), 1, 2)
        ob = _lat_attn(qb_lat.reshape(dec_batch, dec_seq, hb2), kb_lat.reshape(dec_batch, nk_lat, hb2),
                       vbt_lat, ob, nh=H_B, hps=2, dq=2 * LANES, dv=V_B,
                       tq=tq_lat, out_row0=m_ctx, v_transposed=True, name="mla_attn_lat")

        oc = _ctx_attn(b3(qc_ctx), b3(kc_ctx), b3(vc_ctx), li=li, nh=H_C, dq=HD_C, dv=HD_C, out_rows=m,
                       name="na_attn_ctx")
        oc = _na_lat_attn(qc_lat.reshape(dec_batch, dec_seq, BR_W), kvc_lat.reshape(dec_batch, dec_seq, 2 * BR_W),
                          cache_k_c, cache_v_c, na_bias, na, oc, li=li, out_row0=m_ctx)

        merged = _merge(oa, ob, oc, gates, w_branch_b, li)
        y = mm(merged, w_out_b, rows=m, row0=0, col0=0, ncols=d, wsel=li, out_dtype=F32, tn=_pick(d, 1024, 256), name="w_out")

        j = li // 2
        if li % 2 == 0:
            x, hff = rn(x, y, mods, norm_g4, li_post=li, k_gate=2, li_pre=li, k_mod=3, want_x=True, want_h=True)
            tm_f = _pick(m, 512)
            nt = m // tm_f
            yff = _grouped_ffn(hff, ffn_w13b, ffn_w2b, j,
                               jnp.zeros((nt,), jnp.int32), jnp.arange(nt, dtype=jnp.int32),
                               jnp.full((1,), nt, jnp.int32), tm_f)
        else:
            x, hff, logits = rn(x, y, mods, norm_g4, li_post=li, k_gate=2, li_pre=li, k_mod=3, want_x=True, want_h=True,
                                router=router_p[j])
            tm_e = _pick(TOP_K * m, 512)
            row_token, dest, top_w, tile_expert, tile_src, n_used = _moe_plan(logits[:, :N_EXPERTS], tm_e)
            xs = _rows(hff, row_token)
            ys = _grouped_ffn(xs, moe_w13b, moe_w2b, j, tile_expert, tile_src, n_used, tm_e)
            yff = top_w[:, 0:1] * _rows(ys, dest[:, 0]) + top_w[:, 1:2] * _rows(ys, dest[:, 1])
        if li + 1 < depth:
            x, hmix = rn(x, yff, mods, norm_g4, li_post=li, k_gate=5, li_pre=li + 1, k_mod=0, want_x=True, want_h=True)
        else:
            y_prompt, y_sample = rn(x, yff, mods, norm_g4, li_post=li, k_gate=5, li_pre=li, k_mod=0, want_x=True,
                                    want_h=False, out_split=True)

    return (y_prompt.reshape(batch, seq, d), y_sample.reshape(dec_batch, dec_seq, d),
            st_dk, st_dv, st_ckv, jnp.stack(st_kr, axis=1), st_nk, st_nv)
```

```python
import functools
import math

import numpy as np
import jax
import jax.numpy as jnp
from jax import lax
from jax.experimental import pallas as pl
from jax.experimental.pallas import tpu as pltpu

GRID_W = 64
RMS_EPS = 1e-6
ROPE_THETA = 10000.0
H_A, HD_A = 4, 128
H_B, NOPE_B, ROPE_B, V_B = 8, 128, 64, 128
H_C, HD_C = 8, 128
NA_KR_MAX, NA_KC = 8, 16
N_BRANCH = 3
N_EXPERTS = 8
TOP_K = 2
BR_W = 1024

LANES = 128
V7X_VMEM_BYTES = 64 * 1024 * 1024
VMEM_CAP_BYTES = V7X_VMEM_BYTES - 8 * 1024 * 1024
NA_ROWS_PER_BLOCK = 4
NA_HEADS_PER_STEP = 4
MASK_VALUE = -1e30
LOG2E = math.log2(math.e)

F32 = jnp.float32
BF16 = jnp.bfloat16
HIGHEST = lax.Precision.HIGHEST


def _pick(n, pref, mult=8):
    t = min(pref, n)
    t -= t % mult
    while t > mult and n % t:
        t -= mult
    assert t > 0 and n % t == 0, (n, pref, mult)
    return t


def _cparams(sems, vmem_est):
    limit = int(min(max(vmem_est, 16 * 1024 * 1024), VMEM_CAP_BYTES))
    return pltpu.CompilerParams(dimension_semantics=sems, vmem_limit_bytes=limit)


def _rms(v):
    return v * lax.rsqrt(jnp.mean(v * v, axis=-1, keepdims=True) + RMS_EPS)


def _rope128(a, cos, sa, sb, shift):
    return a * cos + pltpu.roll(a, LANES - shift, 1) * sa + pltpu.roll(a, shift, 1) * sb


def _ada_body(c_ref, w_ref, b_ref, o_ref):
    c = c_ref[...]
    s = (c * jax.nn.sigmoid(c)).astype(BF16)
    o_ref[...] = jnp.dot(s, w_ref[...].astype(BF16), preferred_element_type=F32) + b_ref[...]


def _ada_mod(cvec, ada_w, ada_b):
    depth, d, n = ada_w.shape
    r = cvec.shape[0]
    tn = _pick(n, 1024, LANES)
    return pl.pallas_call(
        _ada_body,
        grid=(depth, n // tn),
        in_specs=[pl.BlockSpec((r, d), lambda l, j: (0, 0)),
                  pl.BlockSpec((None, d, tn), lambda l, j: (l, 0, j)),
                  pl.BlockSpec((None, 1, tn), lambda l, j: (l, 0, j))],
        out_specs=pl.BlockSpec((None, r, tn), lambda l, j: (l, 0, j)),
        out_shape=jax.ShapeDtypeStruct((depth, r, n), F32),
        compiler_params=_cparams(("arbitrary", "arbitrary"), 3 * d * tn * 4 + d * tn * 2),
        name="ada_mod",
    )(cvec, ada_w, ada_b.reshape(depth, 1, n))


def _cast_shift_body(a_ref, b_ref, o_ref, *, tn):
    half = LANES // 2
    a = pltpu.roll(a_ref[...], tn - half, 1)
    b = pltpu.roll(b_ref[...], half, 1)
    lane = lax.broadcasted_iota(jnp.int32, b.shape, 1)
    o_ref[:, :tn - LANES] = a[:, :tn - LANES].astype(BF16)
    o_ref[:, tn - LANES:] = jnp.where(lane < half, a[:, tn - LANES:], b).astype(BF16)


def _cast_body(w_ref, o_ref):
    o_ref[...] = w_ref[...].astype(BF16)


def _cast_head(w, n):
    depth, k, _ = w.shape
    assert n % LANES == 0
    tk = _pick(k, 256)
    blk = pl.BlockSpec((None, tk, n), lambda l, i: (l, i, 0))
    return pl.pallas_call(
        _cast_body, grid=(depth, k // tk), in_specs=[blk], out_specs=blk,
        out_shape=jax.ShapeDtypeStruct((depth, k, n), BF16),
        compiler_params=_cparams(("arbitrary", "arbitrary"), 2 * tk * n * 6 + tk * n * 4),
        name="cast_head",
    )(w)


def _cast_shift(w, c0, n, tn):
    depth, k, _ = w.shape
    a0 = c0 - LANES // 2
    assert a0 % tn == 0 and n % tn == 0 and tn % LANES == 0
    ab, r = a0 // tn, tn // LANES
    return pl.pallas_call(
        functools.partial(_cast_shift_body, tn=tn), grid=(depth, n // tn),
        in_specs=[pl.BlockSpec((None, k, tn), lambda l, j: (l, 0, ab + j)),
                  pl.BlockSpec((None, k, LANES), lambda l, j: (l, 0, (ab + j + 1) * r))],
        out_specs=pl.BlockSpec((None, k, tn), lambda l, j: (l, 0, j)),
        out_shape=jax.ShapeDtypeStruct((depth, k, n), BF16),
        compiler_params=_cparams(("arbitrary", "arbitrary"), 2 * k * (tn + LANES) * 4 + 2 * k * tn * 2 + 2 * k * tn * 4),
        name="cast_shift",
    )(w, w)


def _resid_norm_body(*refs, n_ctx_tiles, x_split, has_y, want_x, out_split, want_h, has_router):
    it = iter(refs)
    i = pl.program_id(0)
    if x_split:
        xa_ref, xb_ref = next(it), next(it)
        x = jnp.where(i < n_ctx_tiles, xa_ref[...], xb_ref[...])
    else:
        x = next(it)[...]
    if has_y == "pair":
        ya_ref, yb_ref, w_ref = next(it), next(it), next(it)
        w = w_ref[...]
        y = w[:, 0:1] * ya_ref[...].astype(F32) + w[:, 1:2] * yb_ref[...].astype(F32)
    elif has_y:
        y = next(it)[...].astype(F32)
    if has_y:
        gpost_ref, gate_ref = next(it), next(it)
        x = x + gate_ref[...] * (_rms(y) * gpost_ref[...])
    if want_h:
        gpre_ref, sc_ref, sh_ref = next(it), next(it), next(it)
    if has_router:
        r_ref = next(it)
    if want_x and out_split:
        xa_o, xb_o = next(it), next(it)

        @pl.when(i < n_ctx_tiles)
        def _():
            xa_o[...] = x

        @pl.when(i >= n_ctx_tiles)
        def _():
            xb_o[...] = x
    elif want_x:
        next(it)[...] = x
    if want_h:
        h = _rms(x) * gpre_ref[...]
        h = h * (1.0 + sc_ref[...]) + sh_ref[...]
        next(it)[...] = h.astype(BF16)
        if has_router:
            next(it)[...] = jnp.dot(h, r_ref[...], precision=HIGHEST, preferred_element_type=F32)


def _resid_norm(x, y, mods, norm_g, *, li_post, k_gate, li_pre, k_mod, m_ctx, seg_len,
                want_x, want_h, out_split=False, router=None):
    x_split = isinstance(x, tuple)
    d = x[0].shape[1] if x_split else x.shape[1]
    m = m_ctx + x[1].shape[0] if x_split else x.shape[0]
    tm = _pick(math.gcd(m_ctx, seg_len), 256)
    nct = m_ctx // tm

    def seg(i):
        return jnp.maximum((i * tm - m_ctx) // seg_len + 1, 0)

    row = pl.BlockSpec((tm, d), lambda i: (i, 0))
    row_a = pl.BlockSpec((tm, d), lambda i: (jnp.minimum(i, nct - 1), 0))
    row_b = pl.BlockSpec((tm, d), lambda i: (jnp.maximum(i - nct, 0), 0))
    in_specs, args = ([row_a, row_b], list(x)) if x_split else ([row], [x])
    has_y = False
    if y is not None:
        k_post = 1 if k_gate == 2 else 3
        if isinstance(y, tuple):
            has_y = "pair"
            in_specs += [row, row, pl.BlockSpec((tm, y[2].shape[1]), lambda i: (i, 0))]
            args += list(y)
        else:
            has_y = True
            in_specs.append(row)
            args.append(y)
        in_specs += [pl.BlockSpec((None, None, 1, d), lambda i: (li_post, k_post, 0, 0)),
                     pl.BlockSpec((None, None, None, 1, d), lambda i: (li_post, seg(i), k_gate, 0, 0))]
        args += [norm_g, mods]
    if want_h:
        k_norm = 0 if k_mod == 0 else 2
        in_specs += [pl.BlockSpec((None, None, 1, d), lambda i: (li_pre, k_norm, 0, 0)),
                     pl.BlockSpec((None, None, None, 1, d), lambda i: (li_pre, seg(i), k_mod + 1, 0, 0)),
                     pl.BlockSpec((None, None, None, 1, d), lambda i: (li_pre, seg(i), k_mod, 0, 0))]
        args += [norm_g, mods, mods]
    if router is not None:
        in_specs.append(pl.BlockSpec(router.shape, lambda i: (0, 0)))
        args.append(router)
    out_specs, out_shape = [], []
    if want_x and out_split:
        out_specs += [row_a, row_b]
        out_shape += [jax.ShapeDtypeStruct((m_ctx, d), F32), jax.ShapeDtypeStruct((m - m_ctx, d), F32)]
    elif want_x:
        out_specs.append(row)
        out_shape.append(jax.ShapeDtypeStruct((m, d), F32))
    if want_h:
        out_specs.append(row)
        out_shape.append(jax.ShapeDtypeStruct((m, d), BF16))
        if router is not None:
            out_specs.append(pl.BlockSpec((tm, router.shape[1]), lambda i: (i, 0)))
            out_shape.append(jax.ShapeDtypeStruct((m, router.shape[1]), F32))
    body = functools.partial(_resid_norm_body, n_ctx_tiles=nct, x_split=x_split, has_y=has_y,
                             want_x=want_x, out_split=out_split, want_h=want_h,
                             has_router=router is not None)
    return pl.pallas_call(
        body, grid=(m // tm,), in_specs=in_specs, out_specs=out_specs, out_shape=out_shape,
        compiler_params=_cparams(("arbitrary",), 14 * tm * d * 4 + d * LANES * 8),
        name="resid_norm",
    )(*args)


def _matmul_body(*refs, tn, scale, act, rope_shift, rope_pattern, seq_split, n_heads):
    x_ref, w_ref = refs[0], refs[1]
    o_ref = refs[-1]
    acc = jnp.dot(x_ref[...], w_ref[...], preferred_element_type=F32)
    if scale is not None:
        acc = acc * scale
    if act == "sigmoid":
        acc = jax.nn.sigmoid(acc)
    if seq_split is not None:
        refs[-2][...] = acc.astype(BF16)
        dh = tn // n_heads
        for b in range(acc.shape[0] // seq_split):
            for h in range(n_heads):
                o_ref[b, :, h, :] = acc[b * seq_split:(b + 1) * seq_split, h * dh:(h + 1) * dh]
        return
    if rope_shift is None:
        o_ref[...] = acc.astype(o_ref.dtype)
        return
    cos, sa, sb = refs[2][...], refs[3][...], refs[4][...]
    for g in range(tn // LANES):
        a = acc[:, g * LANES:(g + 1) * LANES]
        if rope_pattern[g % len(rope_pattern)]:
            a = _rope128(a, cos, sa, sb, rope_shift)
        o_ref[:, g * LANES:(g + 1) * LANES] = a.astype(o_ref.dtype)


def _matmul(x, w, *, rows, row0, col0, ncols, wsel, out_dtype, tm, tn, scale=None, act=None,
            rope=None, state=None, name="matmul"):
    k = x.shape[1]
    assert row0 % tm == 0 and rows % tm == 0 and col0 % tn == 0 and ncols % tn == 0, (row0, rows, col0, ncols, tm, tn)
    rb0, cb0 = row0 // tm, col0 // tn
    in_specs = [pl.BlockSpec((tm, k), lambda j, i: (rb0 + i, 0)),
                pl.BlockSpec((None, k, tn), lambda j, i: (wsel, 0, cb0 + j))]
    args = [x, w]
    rope_shift = rope_pattern = None
    if rope is not None:
        tabs, rope_shift, rope_pattern = rope
        nper = tabs[0].shape[0] // tm
        assert tabs[0].shape[0] % tm == 0
        for t in tabs:
            in_specs.append(pl.BlockSpec((tm, LANES), lambda j, i: (i % nper, 0)))
            args.append(t)
    aliases, seq_split, n_heads = {}, None, None
    out_spec = pl.BlockSpec((tm, tn), lambda j, i: (i, j))
    out_shape = jax.ShapeDtypeStruct((rows, ncols), out_dtype)
    if state is not None:
        prev, li, depth, seq_split, n_heads = state
        assert tm % seq_split == 0 and rope is None and tn == ncols and out_dtype == BF16
        dh = ncols // n_heads
        out_spec = [out_spec, pl.BlockSpec((tm // seq_split, None, seq_split, n_heads, dh),
                                           lambda j, i: (i, li, 0, 0, 0))]
        out_shape = [out_shape, jax.ShapeDtypeStruct((rows // seq_split, depth, seq_split, n_heads, dh), F32)]
        if prev is not None:
            in_specs.append(pl.BlockSpec(memory_space=pl.ANY))
            aliases = {len(args): 1}
            args.append(prev)
    osz = jnp.dtype(out_dtype).itemsize
    est = 2 * (tm * k * 2 + k * tn * 2 + tm * tn * osz) + 3 * tm * tn * 4 + 6 * tm * LANES * 4
    body = functools.partial(_matmul_body, tn=tn, scale=scale, act=act, rope_shift=rope_shift,
                             rope_pattern=rope_pattern, seq_split=seq_split, n_heads=n_heads)
    return pl.pallas_call(
        body, grid=(ncols // tn, rows // tm), in_specs=in_specs, out_specs=out_spec, out_shape=out_shape,
        input_output_aliases=aliases,
        compiler_params=_cparams(("arbitrary", "arbitrary"), est),
        name=name,
    )(*args)


def _mla_prep_body(zm_ref, kr_ref, qg_ref, kg_ref, cos_ref, sa_ref, sb_ref, *rest, q_lora, kv_lora,
                   n_ctx_tiles, seq):
    cq_o, ckv_o, ckr_o = rest[-3:]
    z = zm_ref[...]
    cq_o[...] = (_rms(z[:, :q_lora]) * qg_ref[...]).astype(BF16)
    ckv = _rms(z[:, q_lora:]) * kg_ref[...]
    ckr_o[:, :kv_lora] = ckv.astype(BF16)
    i = pl.program_id(0)
    lane = lax.broadcasted_iota(jnp.int32, kr_ref.shape, 1)
    kr = jnp.where(lane < ROPE_B, kr_ref[...], 0.0)

    @pl.when(i < n_ctx_tiles)
    def _():
        ckv_o[...] = ckv.reshape(ckv.shape[0] // seq, seq, kv_lora)
        ckr_o[:, kv_lora:] = kr.astype(BF16)

    @pl.when(i >= n_ctx_tiles)
    def _():
        ckr_o[:, kv_lora:] = _rope128(kr, cos_ref[...], sa_ref[...], sb_ref[...], ROPE_B // 4).astype(BF16)


def _mla_prep(zm, kr, qg, kg, tabs, state, *, q_lora, kv_lora, m_ctx, seg_len):
    prev, li, depth, seq = state[:4]
    m = zm.shape[0]
    tm = _pick(math.gcd(m_ctx, seg_len), 512)
    assert tm % seq == 0
    nct, nper = m_ctx // tm, seg_len // tm
    tab_spec = pl.BlockSpec((tm, LANES), lambda i: (jnp.maximum(i - nct, 0) % nper, 0))
    in_specs = [pl.BlockSpec((tm, q_lora + kv_lora), lambda i: (i, 0)),
                pl.BlockSpec((tm, LANES), lambda i: (i, 0)),
                pl.BlockSpec((1, q_lora), lambda i: (0, 0)),
                pl.BlockSpec((1, kv_lora), lambda i: (0, 0)),
                tab_spec, tab_spec, tab_spec]
    args = [zm, kr, qg, kg, *tabs]
    aliases = {}
    if prev is not None:
        in_specs.append(pl.BlockSpec(memory_space=pl.ANY))
        aliases = {len(args): 1}
        args.append(prev)
    body = functools.partial(_mla_prep_body, q_lora=q_lora, kv_lora=kv_lora, n_ctx_tiles=nct, seq=seq)
    return pl.pallas_call(
        body, grid=(m // tm,), in_specs=in_specs,
        out_specs=[pl.BlockSpec((tm, q_lora), lambda i: (i, 0)),
                   pl.BlockSpec((tm // seq, None, seq, kv_lora), lambda i: (jnp.minimum(i, nct - 1), li, 0, 0)),
                   pl.BlockSpec((tm, kv_lora + LANES), lambda i: (i, 0))],
        out_shape=[jax.ShapeDtypeStruct((m, q_lora), BF16),
                   jax.ShapeDtypeStruct((m_ctx // seq, depth, seq, kv_lora), F32),
                   jax.ShapeDtypeStruct((m, kv_lora + LANES), BF16)],
        input_output_aliases=aliases,
        compiler_params=_cparams(("arbitrary",), 8 * tm * (q_lora + kv_lora + 4 * LANES) * 4),
        name="mla_prep",
    )(*args)


_NT = (((1,), (1,)), ((), ()))


def _softmax_parts(q, k):
    s = lax.dot_general(q, k, _NT, preferred_element_type=F32)
    p = jnp.exp2(s - jnp.max(s, axis=-1, keepdims=True))
    return p, 1.0 / jnp.sum(p, axis=-1, keepdims=True)


def _one_head(q, k, v, *, diff, lam, g, out_scale):
    k = k.astype(BF16)
    v = v.astype(BF16)
    if not diff:
        p, r = _softmax_parts(q, k)
        return jnp.dot(p.astype(BF16), v, preferred_element_type=F32) * r
    p1, r1 = _softmax_parts(q[:, :HD_A], k[:, :HD_A])
    p2, r2 = _softmax_parts(q[:, HD_A:], k[:, HD_A:])
    p = (p1 * r1 - p2 * (lam * r2)).astype(BF16)
    o = jnp.dot(p, v, preferred_element_type=F32)
    return _rms(o) * g * out_scale


def _ctx_attn_body(*refs, nb, nh, dq, dv, diff, out_scale):
    it = iter(refs)
    lam = next(it)[0] if diff else None
    q_ref, k_ref, v_ref = next(it), next(it), next(it)
    g = next(it)[...] if diff else None
    o_ref = refs[-1]
    s = q_ref.shape[1]
    for b in range(nb):
        for h in range(nh):
            o = _one_head(q_ref[b, :, h * dq:(h + 1) * dq], k_ref[b, :, h * dq:(h + 1) * dq],
                          v_ref[b, :, h * dv:(h + 1) * dv], diff=diff, lam=lam, g=g, out_scale=out_scale)
            o_ref[b * s:(b + 1) * s, h * dv:(h + 1) * dv] = o.astype(o_ref.dtype)


def _ctx_attn(q, k, v, *, li, nh, dq, dv, out_rows, diff=False, lam=None, g=None, out_scale=None, name):
    bsz, s, _ = q.shape
    nb = 2 if bsz % 2 == 0 else 1

    def kv_spec(a, width):
        if a.ndim == 4:
            return pl.BlockSpec((nb, None, s, width), lambda i: (i, li, 0, 0))
        return pl.BlockSpec((nb, s, width), lambda i: (i, 0, 0))

    in_specs, args = [], []
    if diff:
        in_specs.append(pl.BlockSpec(memory_space=pltpu.SMEM))
        args.append(lam)
    in_specs += [pl.BlockSpec((nb, s, nh * dq), lambda i: (i, 0, 0)), kv_spec(k, nh * dq), kv_spec(v, nh * dv)]
    args += [q, k, v]
    if diff:
        in_specs.append(pl.BlockSpec((1, dv), lambda i: (0, 0)))
        args.append(g)
    body = functools.partial(_ctx_attn_body, nb=nb, nh=nh, dq=dq, dv=dv, diff=diff, out_scale=out_scale)
    return pl.pallas_call(
        body, grid=(bsz // nb,), in_specs=in_specs,
        out_specs=pl.BlockSpec((nb * s, nh * dv), lambda i: (i, 0)),
        out_shape=jax.ShapeDtypeStruct((out_rows, nh * dv), BF16),
        compiler_params=_cparams(("arbitrary",), 4 * nb * s * nh * (2 * dq + dv) * 4 + 16 * s * s * 4 * nb * nh),
        name=name,
    )(*args)


def _softmax_parts_t(k, q):
    st = lax.dot_general(k, q, _NT, preferred_element_type=F32)
    p = jnp.exp2(st - jnp.max(st, axis=0, keepdims=True))
    return p.astype(BF16), 1.0 / jnp.sum(p, axis=0, keepdims=True)


def _lat_attn_t_body(*refs, hps, dq, dv, diff, out_scale):
    it = iter(refs)
    lam = next(it)[0] if diff else None
    q_ref, k_ref, vt_ref = next(it), next(it), next(it)
    g = next(it)[...] if diff else None
    o_ref = refs[-1]
    if diff:
        q, k, vt = q_ref[...], k_ref[...], vt_ref[...]
        p1, r1 = _softmax_parts_t(k[:, :HD_A], q[:, :HD_A])
        p2, r2 = _softmax_parts_t(k[:, HD_A:], q[:, HD_A:])
        ot = (jnp.dot(vt, p1, preferred_element_type=F32) * r1
              - jnp.dot(vt, p2, preferred_element_type=F32) * (lam * r2))
        o_ref[...] = (_rms(ot.T) * g * out_scale).astype(o_ref.dtype)
        return
    ps = [_softmax_parts_t(k_ref[:, h * dq:(h + 1) * dq], q_ref[:, h * dq:(h + 1) * dq]) for h in range(hps)]
    for h, (p, r) in enumerate(ps):
        ot = jnp.dot(vt_ref[h * dv:(h + 1) * dv, :], p, preferred_element_type=F32) * r
        o_ref[:, h * dv:(h + 1) * dv] = ot.T.astype(o_ref.dtype)


def _lat_attn_body(*refs, hps, dq, dv, diff, out_scale):
    it = iter(refs)
    lam = next(it)[0] if diff else None
    q_ref, k_ref, v_ref = next(it), next(it), next(it)
    g = next(it)[...] if diff else None
    o_ref = refs[-1]
    for h in range(hps):
        o = _one_head(q_ref[:, h * dq:(h + 1) * dq], k_ref[:, h * dq:(h + 1) * dq],
                      v_ref[:, h * dv:(h + 1) * dv], diff=diff, lam=lam, g=g, out_scale=out_scale)
        o_ref[:, h * dv:(h + 1) * dv] = o.astype(o_ref.dtype)


def _lat_attn(q, k, v, prev_out, *, nh, hps, dq, dv, tq, out_row0, diff=False, lam=None, g=None,
              out_scale=None, v_transposed=False, name):
    bsz, n, _ = q.shape
    nk = k.shape[1]
    assert nh % hps == 0 and n % tq == 0 and out_row0 % tq == 0
    in_specs, args = [], []
    if diff:
        in_specs.append(pl.BlockSpec(memory_space=pltpu.SMEM))
        args.append(lam)
    v_spec = (pl.BlockSpec((None, hps * dv, nk), lambda b, h, i: (b, h, 0)) if v_transposed
              else pl.BlockSpec((None, nk, hps * dv), lambda b, h, i: (b, 0, h)))
    in_specs += [pl.BlockSpec((None, tq, hps * dq), lambda b, h, i: (b, i, h)),
                 pl.BlockSpec((None, nk, hps * dq), lambda b, h, i: (b, 0, h)), v_spec]
    args += [q, k, v]
    if diff:
        in_specs.append(pl.BlockSpec((1, dv), lambda b, h, i: (0, 0)))
        args.append(g)
    in_specs.append(pl.BlockSpec(memory_space=pl.ANY))
    aliases = {len(args): 0}
    args.append(prev_out)
    orb0, nqb = out_row0 // tq, n // tq
    n_chain = hps * (2 if diff else 1)
    est = 2 * (tq * hps * dq + nk * hps * (dq + dv)) * 2 + 5 * n_chain * tq * nk * 4
    if v_transposed:
        assert hps == 1 or not diff
        body = functools.partial(_lat_attn_t_body, hps=hps, dq=dq, dv=dv, diff=diff, out_scale=out_scale)
    else:
        body = functools.partial(_lat_attn_body, hps=hps, dq=dq, dv=dv, diff=diff, out_scale=out_scale)
    return pl.pallas_call(
        body, grid=(bsz, nh // hps, nqb), in_specs=in_specs,
        out_specs=pl.BlockSpec((tq, hps * dv), lambda b, h, i: (orb0 + b * nqb + i, h)),
        out_shape=jax.ShapeDtypeStruct(prev_out.shape, prev_out.dtype),
        input_output_aliases=aliases,
        compiler_params=_cparams(("arbitrary", "arbitrary", "arbitrary"), est),
        name=name,
    )(*args)


def _na_geometry(rows):
    rpb = NA_ROWS_PER_BLOCK
    kr = min(NA_KR_MAX, rows)
    span = rpb + kr - 1
    assert rows % rpb == 0 and rows >= span, rows
    nblk = rows // rpb
    r = np.arange(rows)
    row_start = np.clip(r - kr // 2, 0, rows - kr)
    blk_start = np.clip(rpb * np.arange(nblk) - kr // 2, 0, rows - span)
    cols = np.arange(GRID_W)
    col_start = np.clip(cols - NA_KC // 2, 0, GRID_W - NA_KC)
    dc = cols[None, :] - cols[:, None] + (NA_KC - 1)
    ok_c = (cols[None, :] >= col_start[:, None]) & (cols[None, :] < col_start[:, None] + NA_KC)
    sigs, cls_of_blk, dr_l, ok_l = {}, [], [], []
    for b in range(nblk):
        qr = rpb * b + np.arange(rpb)
        key_row = blk_start[b] + np.arange(span)
        dr = key_row[None, :] - qr[:, None] + (NA_KR_MAX - 1)
        ok_r = (key_row[None, :] >= row_start[qr][:, None]) & (key_row[None, :] < row_start[qr][:, None] + kr)
        sig = (dr.tobytes(), ok_r.tobytes())
        if sig not in sigs:
            sigs[sig] = len(sigs)
            dr_l.append(np.clip(dr, 0, 2 * NA_KR_MAX - 2))
            ok = ok_r[:, None, :, None] & ok_c[None, :, None, :]
            ok_l.append(ok.reshape(rpb * GRID_W, span * GRID_W))
        cls_of_blk.append(sigs[sig])
    return dict(span=span, nblk=nblk, blk_start=blk_start.astype(np.int32), cls=np.asarray(cls_of_blk, np.int32),
                dr=np.stack(dr_l), dc=np.clip(dc, 0, 2 * NA_KC - 2), ok=np.stack(ok_l))


def _na_bias(tables, na):
    depth, nh, n_dr, n_dc = tables.shape
    ncls, rpb, span = na["dr"].shape
    oh_r = jax.nn.one_hot(na["dr"].reshape(-1), n_dr, dtype=F32)
    oh_c = jax.nn.one_hot(na["dc"].reshape(-1), n_dc, dtype=F32).T
    t = jnp.einsum("xr,lhrc->lhxc", oh_r, tables.astype(F32), precision=HIGHEST)
    t = jnp.einsum("lhxc,cy->lhxy", t, oh_c, precision=HIGHEST)
    t = t.reshape(depth, nh, ncls, rpb, span, GRID_W, GRID_W).transpose(0, 1, 2, 3, 5, 4, 6)
    t = t.reshape(depth, nh, ncls, rpb * GRID_W, span * GRID_W)
    return jnp.where(na["ok"][None, None], t * LOG2E, MASK_VALUE)


def _na_body(cls_ref, start_ref, q_ref, k_ref, v_ref, kc_ref, vc_ref, bias_ref, prev_ref, o_ref, *, span_tok, hps):
    del cls_ref, prev_ref
    blk = pl.program_id(2)
    start = pl.multiple_of(start_ref[blk] * GRID_W, GRID_W)
    scores = []
    for h in range(hps):
        c = slice(h * HD_C, (h + 1) * HD_C)
        q = q_ref[:, c]
        s_loc = lax.dot_general(q, k_ref[pl.ds(start, span_tok), c], _NT, preferred_element_type=F32) + bias_ref[h]
        s_ctx = lax.dot_general(q, kc_ref[:, c].astype(BF16), _NT, preferred_element_type=F32)
        scores.append((s_loc, s_ctx))
    for h, (s_loc, s_ctx) in enumerate(scores):
        c = slice(h * HD_C, (h + 1) * HD_C)
        mx = jnp.maximum(jnp.max(s_loc, axis=-1, keepdims=True), jnp.max(s_ctx, axis=-1, keepdims=True))
        p_loc = jnp.exp2(s_loc - mx)
        p_ctx = jnp.exp2(s_ctx - mx)
        r = 1.0 / (jnp.sum(p_loc, axis=-1, keepdims=True) + jnp.sum(p_ctx, axis=-1, keepdims=True))
        o = jnp.dot(p_loc.astype(BF16), v_ref[pl.ds(start, span_tok), c], preferred_element_type=F32)
        o = o + jnp.dot(p_ctx.astype(BF16), vc_ref[:, c].astype(BF16), preferred_element_type=F32)
        o_ref[:, c] = (o * r).astype(o_ref.dtype)


def _na_lat_attn(q, kv, cache_k, cache_v, bias, na, prev_out, *, li, out_row0):
    bsz, n, _ = q.shape
    past = cache_k.shape[2]
    rpb_tok, span_tok, nblk = NA_ROWS_PER_BLOCK * GRID_W, na["span"] * GRID_W, na["nblk"]
    hps = NA_HEADS_PER_STEP
    w = hps * HD_C
    ng = H_C // hps
    gs = pltpu.PrefetchScalarGridSpec(
        num_scalar_prefetch=2, grid=(bsz, ng, nblk),
        in_specs=[pl.BlockSpec((None, rpb_tok, w), lambda b, h, i, cl, st: (b, i, h)),
                  pl.BlockSpec((None, n, w), lambda b, h, i, cl, st: (b, 0, h)),
                  pl.BlockSpec((None, n, w), lambda b, h, i, cl, st: (b, 0, ng + h)),
                  pl.BlockSpec((None, None, past, w), lambda b, h, i, cl, st: (b, li, 0, h)),
                  pl.BlockSpec((None, None, past, w), lambda b, h, i, cl, st: (b, li, 0, h)),
                  pl.BlockSpec((None, hps, None, rpb_tok, span_tok), lambda b, h, i, cl, st: (li, h, cl[i], 0, 0)),
                  pl.BlockSpec(memory_space=pl.ANY)],
        out_specs=pl.BlockSpec((rpb_tok, w), lambda b, h, i, cl, st: (out_row0 // rpb_tok + b * nblk + i, h)))
    return pl.pallas_call(
        functools.partial(_na_body, span_tok=span_tok, hps=hps),
        grid_spec=gs, out_shape=jax.ShapeDtypeStruct(prev_out.shape, prev_out.dtype),
        input_output_aliases={8: 0},
        compiler_params=_cparams(("arbitrary", "arbitrary", "arbitrary"),
                                 8 * n * w * 2 + hps * 12 * rpb_tok * span_tok * 4),
        name="na_attn_lat",
    )(jnp.asarray(na["cls"]), jnp.asarray(na["blk_start"]), q, kv, kv, cache_k, cache_v, bias, prev_out)


def _merge_body(a_ref, b_ref, c_ref, ga_ref, gb_ref, gc_ref, wa_ref, wb_ref, wc_ref, o_ref):
    acc = ga_ref[...].astype(F32) * jnp.dot(a_ref[...], wa_ref[...], preferred_element_type=F32)
    acc += gb_ref[...].astype(F32) * jnp.dot(b_ref[...], wb_ref[...], preferred_element_type=F32)
    acc += gc_ref[...].astype(F32) * jnp.dot(c_ref[...], wc_ref[...], preferred_element_type=F32)
    o_ref[...] = acc.astype(o_ref.dtype)


def _merge(oa, ob, oc, gates, wb, li):
    m, kb = oa.shape
    d = wb.shape[-1]
    tm, tn = _pick(m, 1024), _pick(d, 1024, 256)
    nj = d // tn
    br = pl.BlockSpec((tm, kb), lambda j, i: (i, 0))
    gspec = [pl.BlockSpec((tm, tn), functools.partial(lambda j, i, s: (i, s * nj + j), s=s)) for s in range(N_BRANCH)]
    wspec = [pl.BlockSpec((None, None, kb, tn), functools.partial(lambda j, i, s: (li, s, 0, j), s=s)) for s in range(N_BRANCH)]
    est = 2 * (3 * tm * kb * 2 + 3 * tm * tn * 2 + 3 * kb * tn * 2 + tm * tn * 2) + 4 * tm * tn * 4
    return pl.pallas_call(
        _merge_body, grid=(nj, m // tm), in_specs=[br, br, br] + gspec + wspec,
        out_specs=pl.BlockSpec((tm, tn), lambda j, i: (i, j)),
        out_shape=jax.ShapeDtypeStruct((m, d), BF16),
        compiler_params=_cparams(("arbitrary", "arbitrary"), est),
        name="branch_merge",
    )(oa, ob, oc, gates, gates, gates, wb, wb, wb)


def _ffn1_body(te_ref, ts_ref, nu_ref, x_ref, w13_ref, o_ref):
    del te_ref, ts_ref

    @pl.when(pl.program_id(1) < nu_ref[0])
    def _():
        tf = o_ref.shape[1]
        h = jnp.dot(x_ref[...], w13_ref[...], preferred_element_type=F32)
        h1, h3 = h[:, :tf], h[:, tf:]
        o_ref[...] = (h1 * jax.nn.sigmoid(h1) * h3).astype(o_ref.dtype)


def _ffn2_body(te_ref, ts_ref, nu_ref, h_ref, w2_ref, o_ref):
    del te_ref, ts_ref

    @pl.when(pl.program_id(1) < nu_ref[0])
    def _():
        o_ref[...] = jnp.dot(h_ref[...], w2_ref[...], preferred_element_type=F32).astype(o_ref.dtype)


def _ffn_tile(f):
    return _pick(f, 1408, LANES)


def _cast_interleave_body(w1_ref, w3_ref, o_ref):
    tf = w1_ref.shape[1]
    o_ref[:, :tf] = w1_ref[...].astype(BF16)
    o_ref[:, tf:] = w3_ref[...].astype(BF16)


def _cast_interleave(w1, w3):
    nl, ne, d, f = w1.shape
    tf = _ffn_tile(f)
    tk = _pick(d, 1024)
    blk = pl.BlockSpec((None, None, tk, tf), lambda l, e, k, j: (l, e, k, j))
    return pl.pallas_call(
        _cast_interleave_body, grid=(nl, ne, d // tk, f // tf), in_specs=[blk, blk],
        out_specs=pl.BlockSpec((None, None, tk, 2 * tf), lambda l, e, k, j: (l, e, k, j)),
        out_shape=jax.ShapeDtypeStruct((nl, ne, d, 2 * f), BF16),
        compiler_params=_cparams(("arbitrary",) * 4, 2 * (2 * tk * tf * 4 + tk * 2 * tf * 2) + 2 * tk * tf * 4),
        name="cast_interleave",
    )(w1, w3)


def _grouped_ffn(xs, w13, w2, lsel, tile_expert, tile_src, n_used, tm, out_dtype=F32):
    r, d = xs.shape
    f = w2.shape[2]
    nt = r // tm
    tf = _ffn_tile(f)
    tn = _pick(d, 1024, 256)
    gs1 = pltpu.PrefetchScalarGridSpec(
        num_scalar_prefetch=3, grid=(f // tf, nt),
        in_specs=[pl.BlockSpec((tm, d), lambda j, t, te, ts, nu: (ts[t], 0)),
                  pl.BlockSpec((None, None, d, 2 * tf), lambda j, t, te, ts, nu: (lsel, te[t], 0, j))],
        out_specs=pl.BlockSpec((tm, tf), lambda j, t, te, ts, nu: (ts[t], j)))
    hid = pl.pallas_call(
        _ffn1_body, grid_spec=gs1, out_shape=jax.ShapeDtypeStruct((r, f), BF16),
        compiler_params=_cparams(("arbitrary", "arbitrary"),
                                 2 * (tm * d * 2 + 2 * d * tf * 2 + tm * tf * 2) + 4 * tm * tf * 4),
        name="ffn_up",
    )(tile_expert, tile_src, n_used, xs, w13)
    gs2 = pltpu.PrefetchScalarGridSpec(
        num_scalar_prefetch=3, grid=(d // tn, nt),
        in_specs=[pl.BlockSpec((tm, f), lambda j, t, te, ts, nu: (ts[t], 0)),
                  pl.BlockSpec((None, None, f, tn), lambda j, t, te, ts, nu: (lsel, te[t], 0, j))],
        out_specs=pl.BlockSpec((tm, tn), lambda j, t, te, ts, nu: (ts[t], j)))
    return pl.pallas_call(
        _ffn2_body, grid_spec=gs2, out_shape=jax.ShapeDtypeStruct((r, d), out_dtype),
        compiler_params=_cparams(("arbitrary", "arbitrary"),
                                 2 * (tm * f * 2 + f * tn * 2 + tm * tn * 4) + 2 * tm * tn * 4),
        name="ffn_down",
    )(tile_expert, tile_src, n_used, hid, w2)


def _moe_plan(logits, tm):
    m = logits.shape[0]
    top_v, top_i = lax.top_k(logits, TOP_K)
    top_w = jax.nn.softmax(top_v, axis=-1)
    flat_e = top_i.reshape(-1)
    onehot = (flat_e[:, None] == jnp.arange(N_EXPERTS)[None, :]).astype(jnp.int32)
    csum = jnp.cumsum(onehot, axis=0)
    counts = csum[-1]
    rank = jnp.take_along_axis(csum, flat_e[:, None], axis=1)[:, 0] - 1
    padded = ((counts + tm - 1) // tm) * tm
    ends = jnp.cumsum(padded)
    offs = ends - padded
    dest = offs[flat_e] + rank
    n_rows = TOP_K * m + N_EXPERTS * tm
    nt = n_rows // tm
    row_token = jnp.zeros((n_rows,), jnp.int32).at[dest].set(jnp.arange(TOP_K * m, dtype=jnp.int32) // TOP_K)
    n_used = (ends[-1] // tm).astype(jnp.int32)
    tile_src = jnp.minimum(jnp.arange(nt, dtype=jnp.int32), n_used - 1)
    tile_expert = jnp.sum((tile_src[:, None] * tm >= ends[None, :]).astype(jnp.int32), axis=1)
    tile_expert = jnp.minimum(tile_expert, N_EXPERTS - 1).astype(jnp.int32)
    return row_token, dest.reshape(m, TOP_K), top_w, tile_expert, tile_src, n_used.reshape(1)


def _rows(a, idx):
    return a.at[idx].get(mode="promise_in_bounds")


def _rope_tables(n_tok, dim):
    half = dim // 2
    t = jnp.arange(n_tok)
    row = (t // GRID_W).astype(F32)
    col = (t % GRID_W).astype(F32)
    inv = ROPE_THETA ** (-jnp.arange(0, half, 2, dtype=F32) / half)
    ar = row[:, None] * inv[None, :]
    ac = col[:, None] * inv[None, :]
    ang = jnp.concatenate([ar, ar, ac, ac], axis=-1)
    cos, sin = jnp.cos(ang), jnp.sin(ang)
    first = (np.arange(dim) % half) < (dim // 4)
    sa = jnp.where(first[None, :], -sin, 0.0)
    sb = jnp.where(first[None, :], 0.0, sin)
    pad = LANES - dim
    if pad:
        cos = jnp.pad(cos, ((0, 0), (0, pad)), constant_values=1.0)
        sa = jnp.pad(sa, ((0, 0), (0, pad)))
        sb = jnp.pad(sb, ((0, 0), (0, pad)))
    return cos, sa, sb


def kernel(x_prompt, x_sample, cache_diff_k, cache_diff_v, cache_mla_ckv, cache_mla_krope, cache_na_k, cache_na_v, c, c_ctx, norm_g, ada_w, ada_b, w_in, diff_lambda, diff_subln_g, mla_q_norm_g, mla_kv_norm_g, mla_w_uq, mla_w_ukv, na_rel_bias, w_branch, w_out, ffn_w1, ffn_w3, ffn_w2, moe_router, moe_w1, moe_w3, moe_w2):
    batch, seq, d = x_prompt.shape
    dec_batch, dec_seq, _ = x_sample.shape
    depth = norm_g.shape[0]
    past = cache_diff_k.shape[2]
    q_lora, kv_lora = mla_q_norm_g.shape[1], mla_kv_norm_g.shape[1]
    m_ctx, m_lat = batch * seq, dec_batch * dec_seq
    m = m_ctx + m_lat
    nk_lat = dec_seq + past
    assert dec_seq % GRID_W == 0 and kv_lora % LANES == 0 and q_lora % LANES == 0

    a3 = 3 * BR_W
    col_m = a3
    col_kr = a3 + q_lora + kv_lora
    c_na = col_kr + ROPE_B
    n_rest = 3 * BR_W + N_BRANCH * d
    w_head = _cast_head(w_in, col_kr + LANES)
    w_rest = _cast_shift(w_in, c_na, n_rest, 768)
    col_c, col_g = 0, 3 * BR_W
    wq = mla_w_uq.reshape(depth, q_lora, H_B, NOPE_B + ROPE_B)
    w_uqp = jnp.pad(wq, ((0, 0), (0, 0), (0, 0), (0, 2 * LANES - NOPE_B - ROPE_B))).reshape(depth, q_lora, H_B * 2 * LANES).astype(BF16)
    wkv = mla_w_ukv.reshape(depth, kv_lora, H_B, NOPE_B + V_B)
    wk_top = jnp.pad(wkv[..., :NOPE_B], ((0, 0), (0, 0), (0, 0), (0, 2 * LANES - NOPE_B))).reshape(depth, kv_lora, H_B * 2 * LANES)
    eye = np.zeros((LANES, H_B, 2 * LANES), np.float32)
    for r_ in range(ROPE_B):
        eye[r_, :, NOPE_B + r_] = 1.0
    wk_aug = jnp.concatenate([wk_top, jnp.broadcast_to(jnp.asarray(eye.reshape(LANES, -1)), (depth, LANES, H_B * 2 * LANES))], axis=1).astype(BF16)
    wv_aug = jnp.pad(wkv[..., NOPE_B:].reshape(depth, kv_lora, H_B * V_B), ((0, 0), (0, LANES), (0, 0))).astype(BF16)
    w_branch_b = w_branch.astype(BF16)
    w_out_b = w_out.astype(BF16)
    ffn_w13b, ffn_w2b = _cast_interleave(ffn_w1[:, None], ffn_w3[:, None]), ffn_w2.astype(BF16)[:, None]
    if depth > 1:
        moe_w13b, moe_w2b = _cast_interleave(moe_w1, moe_w3), moe_w2.astype(BF16)
    router_p = jnp.pad(moe_router, ((0, 0), (0, 0), (0, LANES - N_EXPERTS)))
    norm_g4 = norm_g.reshape(depth, 4, 1, d)

    n_seg = 1 + dec_batch
    r_pad = -(-n_seg // 8) * 8
    cvec = jnp.zeros((r_pad, d), F32).at[0].set(c_ctx).at[1:n_seg].set(c)
    mods = _ada_mod(cvec, ada_w, ada_b).reshape(depth, r_pad, 6, 1, d)

    tabs_a = _rope_tables(dec_seq, HD_A)
    tabs_b = _rope_tables(dec_seq, ROPE_B)
    na = _na_geometry(dec_seq // GRID_W)
    na_bias = _na_bias(na_rel_bias, na)
    lam_inits = [0.8 - 0.6 * math.exp(-0.3 * li) for li in range(depth)]
    lp = diff_lambda.astype(F32)
    lams = jnp.exp(jnp.sum(lp[:, 0] * lp[:, 1], axis=-1)) - jnp.exp(jnp.sum(lp[:, 2] * lp[:, 3], axis=-1)) + jnp.asarray(lam_inits, F32)

    tm_big = _pick(math.gcd(m_ctx, dec_seq), 1024)
    assert tm_big % seq == 0
    tq_lat = _pick(dec_seq, 512)
    rn = functools.partial(_resid_norm, m_ctx=m_ctx, seg_len=dec_seq)
    mm = functools.partial(_matmul, tm=tm_big)
    qs_a, qs_b, qs_c = HD_A ** -0.5 * LOG2E, (NOPE_B + ROPE_B) ** -0.5 * LOG2E, HD_C ** -0.5 * LOG2E
    hb2 = H_B * 2 * LANES
    cache_k_c = cache_na_k.reshape(dec_batch, depth, past, H_C * HD_C)
    cache_v_c = cache_na_v.reshape(dec_batch, depth, past, H_C * HD_C)

    x = (x_prompt.reshape(m_ctx, d), x_sample.reshape(m_lat, d))
    (hmix,) = rn(x, None, mods, norm_g4, li_post=0, k_gate=2, li_pre=0, k_mod=0, want_x=False, want_h=True)
    st_dk = st_dv = st_ckv = st_nk = st_nv = None
    st_kr = []

    for li in range(depth):
        def st(prev, n_heads=None):
            return (prev, li, depth, seq, n_heads)

        def b3(a):
            return a.reshape(batch, seq, a.shape[-1])

        ctx = dict(rows=m_ctx, row0=0, wsel=li, tn=1024)
        lat = dict(rows=m_lat, row0=m_ctx, wsel=li, tn=1024)
        qa_ctx = mm(hmix, w_head, col0=0, ncols=BR_W, out_dtype=BF16, scale=qs_a, name="in_qa_ctx", **ctx)
        ka_ctx, st_dk = mm(hmix, w_head, col0=BR_W, ncols=BR_W, out_dtype=BF16, state=st(st_dk, H_A), name="in_ka_ctx", **ctx)
        va_ctx, st_dv = mm(hmix, w_head, col0=2 * BR_W, ncols=BR_W, out_dtype=BF16, state=st(st_dv, H_A), name="in_va_ctx", **ctx)
        qa_lat = mm(hmix, w_head, col0=0, ncols=BR_W, out_dtype=BF16, scale=qs_a,
                    rope=(tabs_a, HD_A // 4, (True,)), name="in_qa_lat", **lat)
        ka_lat = mm(hmix, w_head, col0=BR_W, ncols=BR_W, out_dtype=BF16,
                    rope=(tabs_a, HD_A // 4, (True,)), name="in_ka_lat", **lat)
        va_lat = mm(hmix, w_head, col0=2 * BR_W, ncols=BR_W, out_dtype=BF16, name="in_va_lat", **lat)
        qc_ctx = mm(hmix, w_rest, col0=col_c, ncols=BR_W, out_dtype=BF16, scale=qs_c, name="in_qc_ctx", **ctx)
        kc_ctx, st_nk = mm(hmix, w_rest, col0=col_c + BR_W, ncols=BR_W, out_dtype=BF16, state=st(st_nk, H_C), name="in_kc_ctx", **ctx)
        vc_ctx, st_nv = mm(hmix, w_rest, col0=col_c + 2 * BR_W, ncols=BR_W, out_dtype=BF16, state=st(st_nv, H_C), name="in_vc_ctx", **ctx)
        qc_lat = mm(hmix, w_rest, col0=col_c, ncols=BR_W, out_dtype=BF16, scale=qs_c, name="in_qc_lat", **lat)
        kvc_lat = mm(hmix, w_rest, col0=col_c + BR_W, ncols=2 * BR_W, out_dtype=BF16, name="in_kvc_lat", **lat)
        gates = mm(hmix, w_rest, rows=m, row0=0, col0=col_g, ncols=N_BRANCH * d, wsel=li, out_dtype=BF16, tn=768,
                   act="sigmoid", name="in_gates")
        zm = mm(hmix, w_head, rows=m, row0=0, col0=col_m, ncols=q_lora + kv_lora, wsel=li, out_dtype=F32,
                tn=q_lora + kv_lora, name="in_mla")
        kr = mm(hmix, w_head, rows=m, row0=0, col0=col_kr, ncols=LANES, wsel=li, out_dtype=F32, tn=LANES, name="in_krope")
        st_kr.append(kr[:m_ctx, :ROPE_B].reshape(batch, seq, ROPE_B))

        g_sub = diff_subln_g[li].reshape(1, 2 * HD_A)
        lam = lams[li].reshape(1)
        oa = _ctx_attn(b3(qa_ctx), b3(ka_ctx), b3(va_ctx), li=li, nh=H_A, dq=2 * HD_A, dv=2 * HD_A,
                       out_rows=m, diff=True, lam=lam, g=g_sub, out_scale=1.0 - lam_inits[li], name="diff_attn_ctx")
        ka_all = jnp.concatenate([ka_lat.reshape(dec_batch, dec_seq, BR_W),
                                  cache_diff_k[:, li].reshape(dec_batch, past, BR_W).astype(BF16)], axis=1)
        va_all = jnp.concatenate([va_lat.reshape(dec_batch, dec_seq, BR_W),
                                  cache_diff_v[:, li].reshape(dec_batch, past, BR_W).astype(BF16)], axis=1)
        oa = _lat_attn(qa_lat.reshape(dec_batch, dec_seq, BR_W), ka_all, jnp.swapaxes(va_all, 1, 2), oa, nh=H_A,
                       hps=1, dq=2 * HD_A, dv=2 * HD_A, tq=tq_lat, out_row0=m_ctx, diff=True, lam=lam, g=g_sub,
                       out_scale=1.0 - lam_inits[li], v_transposed=True, name="diff_attn_lat")

        cqn, st_ckv, ckr = _mla_prep(zm, kr, mla_q_norm_g[li].reshape(1, q_lora), mla_kv_norm_g[li].reshape(1, kv_lora),
                                     tabs_b, st(st_ckv), q_lora=q_lora, kv_lora=kv_lora, m_ctx=m_ctx, seg_len=dec_seq)
        qb_ctx = mm(cqn, w_uqp, col0=0, ncols=hb2, out_dtype=BF16, scale=qs_b, name="mla_q_ctx", **ctx)
        qb_lat = mm(cqn, w_uqp, col0=0, ncols=hb2, out_dtype=BF16, scale=qs_b,
                    rope=(tabs_b, ROPE_B // 4, (False, True)), name="mla_q_lat", **lat)
        cache_ckr = jnp.concatenate([cache_mla_ckv[:, li], cache_mla_krope[:, li],
                                     jnp.zeros((dec_batch, past, LANES - ROPE_B), F32)], axis=-1).astype(BF16)
        ckr_lat = jnp.concatenate([ckr[m_ctx:].reshape(dec_batch, dec_seq, kv_lora + LANES), cache_ckr], axis=1)
        ckr_lat = ckr_lat.reshape(dec_batch * nk_lat, kv_lora + LANES)
        tm_kv = _pick(math.gcd(m_ctx, dec_batch * nk_lat), 1024)
        kv_mm = functools.partial(_matmul, row0=0, col0=0, wsel=li, out_dtype=BF16, tm=tm_kv, tn=1024)
        kb_ctx = kv_mm(ckr, wk_aug, rows=m_ctx, ncols=hb2, name="mla_k_ctx")
        vb_ctx = kv_mm(ckr, wv_aug, rows=m_ctx, ncols=H_B * V_B, name="mla_v_ctx")
        kb_lat = kv_mm(ckr_lat, wk_aug, rows=dec_batch * nk_lat, ncols=hb2, name="mla_k_lat")
        vb_lat = kv_mm(ckr_lat, wv_aug, rows=dec_batch * nk_lat, ncols=H_B * V_B, name="mla_v_lat")
        ob = _ctx_attn(qb_ctx.reshape(batch, seq, hb2), kb_ctx.reshape(batch, seq, hb2),
                       vb_ctx.reshape(batch, seq, H_B * V_B), li=li, nh=H_B, dq=2 * LANES, dv=V_B, out_rows=m,
                       name="mla_attn_ctx")
        vbt_lat = jnp.swapaxes(vb_lat.reshape(dec_batch, nk_lat, H_B * V_B), 1, 2)
        ob = _lat_attn(qb_lat.reshape(dec_batch, dec_seq, hb2), kb_lat.reshape(dec_batch, nk_lat, hb2),
                       vbt_lat, ob, nh=H_B, hps=2, dq=2 * LANES, dv=V_B,
                       tq=tq_lat, out_row0=m_ctx, v_transposed=True, name="mla_attn_lat")

        oc = _ctx_attn(b3(qc_ctx), b3(kc_ctx), b3(vc_ctx), li=li, nh=H_C, dq=HD_C, dv=HD_C, out_rows=m,
                       name="na_attn_ctx")
        oc = _na_lat_attn(qc_lat.reshape(dec_batch, dec_seq, BR_W), kvc_lat.reshape(dec_batch, dec_seq, 2 * BR_W),
                          cache_k_c, cache_v_c, na_bias, na, oc, li=li, out_row0=m_ctx)

        merged = _merge(oa, ob, oc, gates, w_branch_b, li)
        y = mm(merged, w_out_b, rows=m, row0=0, col0=0, ncols=d, wsel=li, out_dtype=F32, tn=_pick(d, 1024, 256), name="w_out")

        j = li // 2
        if li % 2 == 0:
            x, hff = rn(x, y, mods, norm_g4, li_post=li, k_gate=2, li_pre=li, k_mod=3, want_x=True, want_h=True)
            tm_f = _pick(m, 512)
            nt = m // tm_f
            yff = _grouped_ffn(hff, ffn_w13b, ffn_w2b, j,
                               jnp.zeros((nt,), jnp.int32), jnp.arange(nt, dtype=jnp.int32),
                               jnp.full((1,), nt, jnp.int32), tm_f)
        else:
            x, hff, logits = rn(x, y, mods, norm_g4, li_post=li, k_gate=2, li_pre=li, k_mod=3, want_x=True, want_h=True,
                                router=router_p[j])
            tm_e = _pick(TOP_K * m, 512)
            row_token, dest, top_w, tile_expert, tile_src, n_used = _moe_plan(logits[:, :N_EXPERTS], tm_e)
            xs = _rows(hff, row_token)
            ys = _grouped_ffn(xs, moe_w13b, moe_w2b, j, tile_expert, tile_src, n_used, tm_e, out_dtype=BF16)
            yff = (_rows(ys, dest[:, 0]), _rows(ys, dest[:, 1]), top_w)
        if li + 1 < depth:
            x, hmix = rn(x, yff, mods, norm_g4, li_post=li, k_gate=5, li_pre=li + 1, k_mod=0, want_x=True, want_h=True)
        else:
            y_prompt, y_sample = rn(x, yff, mods, norm_g4, li_post=li, k_gate=5, li_pre=li, k_mod=0, want_x=True,
                                    want_h=False, out_split=True)

    return (y_prompt.reshape(batch, seq, d), y_sample.reshape(dec_batch, dec_seq, d),
            st_dk, st_dv, st_ckv, jnp.stack(st_kr, axis=1), st_nk, st_nv)
```

```python
import functools
import math

import numpy as np
import jax
import jax.numpy as jnp
from jax import lax
from jax.experimental import pallas as pl
from jax.experimental.pallas import tpu as pltpu

GRID_W = 64
RMS_EPS = 1e-6
ROPE_THETA = 10000.0
H_A, HD_A = 4, 128
H_B, NOPE_B, ROPE_B, V_B = 8, 128, 64, 128
H_C, HD_C = 8, 128
NA_KR_MAX, NA_KC = 8, 16
N_BRANCH = 3
N_EXPERTS = 8
TOP_K = 2
BR_W = 1024

LANES = 128
V7X_VMEM_BYTES = 64 * 1024 * 1024
VMEM_CAP_BYTES = V7X_VMEM_BYTES - 8 * 1024 * 1024
NA_ROWS_PER_BLOCK = 4
NA_HEADS_PER_STEP = 4
MASK_VALUE = -1e30
LOG2E = math.log2(math.e)

F32 = jnp.float32
BF16 = jnp.bfloat16
HIGHEST = lax.Precision.HIGHEST


def _pick(n, pref, mult=8):
    t = min(pref, n)
    t -= t % mult
    while t > mult and n % t:
        t -= mult
    assert t > 0 and n % t == 0, (n, pref, mult)
    return t


def _cparams(sems, vmem_est):
    limit = int(min(max(vmem_est, 16 * 1024 * 1024), VMEM_CAP_BYTES))
    return pltpu.CompilerParams(dimension_semantics=sems, vmem_limit_bytes=limit)


def _rms(v):
    return v * lax.rsqrt(jnp.mean(v * v, axis=-1, keepdims=True) + RMS_EPS)


def _rope128(a, cos, sa, sb, shift):
    return a * cos + pltpu.roll(a, LANES - shift, 1) * sa + pltpu.roll(a, shift, 1) * sb


def _ada_body(c_ref, w_ref, b_ref, o_ref):
    c = c_ref[...]
    s = (c * jax.nn.sigmoid(c)).astype(BF16)
    o_ref[...] = jnp.dot(s, w_ref[...].astype(BF16), preferred_element_type=F32) + b_ref[...]


def _ada_mod(cvec, ada_w, ada_b):
    depth, d, n = ada_w.shape
    r = cvec.shape[0]
    tn = _pick(n, 1024, LANES)
    return pl.pallas_call(
        _ada_body,
        grid=(depth, n // tn),
        in_specs=[pl.BlockSpec((r, d), lambda l, j: (0, 0)),
                  pl.BlockSpec((None, d, tn), lambda l, j: (l, 0, j)),
                  pl.BlockSpec((None, 1, tn), lambda l, j: (l, 0, j))],
        out_specs=pl.BlockSpec((None, r, tn), lambda l, j: (l, 0, j)),
        out_shape=jax.ShapeDtypeStruct((depth, r, n), F32),
        compiler_params=_cparams(("arbitrary", "arbitrary"), 3 * d * tn * 4 + d * tn * 2),
        name="ada_mod",
    )(cvec, ada_w, ada_b.reshape(depth, 1, n))


def _cast_shift_body(a_ref, b_ref, o_ref, *, tn):
    half = LANES // 2
    a = pltpu.roll(a_ref[...], tn - half, 1)
    b = pltpu.roll(b_ref[...], half, 1)
    lane = lax.broadcasted_iota(jnp.int32, b.shape, 1)
    o_ref[:, :tn - LANES] = a[:, :tn - LANES].astype(BF16)
    o_ref[:, tn - LANES:] = jnp.where(lane < half, a[:, tn - LANES:], b).astype(BF16)


def _cast_body(w_ref, o_ref):
    o_ref[...] = w_ref[...].astype(BF16)


def _cast_head(w, n):
    depth, k, _ = w.shape
    assert n % LANES == 0
    tk = _pick(k, 256)
    blk = pl.BlockSpec((None, tk, n), lambda l, i: (l, i, 0))
    return pl.pallas_call(
        _cast_body, grid=(depth, k // tk), in_specs=[blk], out_specs=blk,
        out_shape=jax.ShapeDtypeStruct((depth, k, n), BF16),
        compiler_params=_cparams(("arbitrary", "arbitrary"), 2 * tk * n * 6 + tk * n * 4),
        name="cast_head",
    )(w)


def _cast_shift(w, c0, n, tn):
    depth, k, _ = w.shape
    a0 = c0 - LANES // 2
    assert a0 % tn == 0 and n % tn == 0 and tn % LANES == 0
    ab, r = a0 // tn, tn // LANES
    return pl.pallas_call(
        functools.partial(_cast_shift_body, tn=tn), grid=(depth, n // tn),
        in_specs=[pl.BlockSpec((None, k, tn), lambda l, j: (l, 0, ab + j)),
                  pl.BlockSpec((None, k, LANES), lambda l, j: (l, 0, (ab + j + 1) * r))],
        out_specs=pl.BlockSpec((None, k, tn), lambda l, j: (l, 0, j)),
        out_shape=jax.ShapeDtypeStruct((depth, k, n), BF16),
        compiler_params=_cparams(("arbitrary", "arbitrary"), 2 * k * (tn + LANES) * 4 + 2 * k * tn * 2 + 2 * k * tn * 4),
        name="cast_shift",
    )(w, w)


def _resid_norm_body(*refs, n_ctx_tiles, x_split, has_y, want_x, out_split, want_h, has_router):
    it = iter(refs)
    i = pl.program_id(0)
    if x_split:
        xa_ref, xb_ref = next(it), next(it)
        x = jnp.where(i < n_ctx_tiles, xa_ref[...], xb_ref[...])
    else:
        x = next(it)[...]
    if has_y == "pair":
        ya_ref, yb_ref, w_ref = next(it), next(it), next(it)
        w = w_ref[...]
        y = w[:, 0:1] * ya_ref[...].astype(F32) + w[:, 1:2] * yb_ref[...].astype(F32)
    elif has_y == "proj":
        a_ref, w_ref = next(it), next(it)
        y = jnp.dot(a_ref[...], w_ref[...], preferred_element_type=F32)
    elif has_y:
        y = next(it)[...].astype(F32)
    if has_y:
        gpost_ref, gate_ref = next(it), next(it)
        x = x + gate_ref[...] * (_rms(y) * gpost_ref[...])
    if want_h:
        gpre_ref, sc_ref, sh_ref = next(it), next(it), next(it)
    if has_router:
        r_ref = next(it)
    if want_x and out_split:
        xa_o, xb_o = next(it), next(it)

        @pl.when(i < n_ctx_tiles)
        def _():
            xa_o[...] = x

        @pl.when(i >= n_ctx_tiles)
        def _():
            xb_o[...] = x
    elif want_x:
        next(it)[...] = x
    if want_h:
        h = _rms(x) * gpre_ref[...]
        h = h * (1.0 + sc_ref[...]) + sh_ref[...]
        next(it)[...] = h.astype(BF16)
        if has_router:
            next(it)[...] = jnp.dot(h, r_ref[...], precision=HIGHEST, preferred_element_type=F32)


def _resid_norm(x, y, mods, norm_g, *, li_post, k_gate, li_pre, k_mod, m_ctx, seg_len,
                want_x, want_h, out_split=False, router=None):
    x_split = isinstance(x, tuple)
    d = x[0].shape[1] if x_split else x.shape[1]
    m = m_ctx + x[1].shape[0] if x_split else x.shape[0]
    tm = _pick(math.gcd(m_ctx, seg_len), 256)
    nct = m_ctx // tm

    def seg(i):
        return jnp.maximum((i * tm - m_ctx) // seg_len + 1, 0)

    row = pl.BlockSpec((tm, d), lambda i: (i, 0))
    row_a = pl.BlockSpec((tm, d), lambda i: (jnp.minimum(i, nct - 1), 0))
    row_b = pl.BlockSpec((tm, d), lambda i: (jnp.maximum(i - nct, 0), 0))
    in_specs, args = ([row_a, row_b], list(x)) if x_split else ([row], [x])
    has_y, proj_bytes = False, 0
    if y is not None:
        k_post = 1 if k_gate == 2 else 3
        if isinstance(y, tuple) and len(y) == 2:
            has_y = "proj"
            a, w = y
            in_specs += [pl.BlockSpec((tm, a.shape[1]), lambda i: (i, 0)),
                         pl.BlockSpec((None, w.shape[1], d), lambda i: (li_post, 0, 0))]
            args += [a, w]
            proj_bytes = 2 * w.shape[1] * d * 2 + 2 * tm * a.shape[1] * 2
        elif isinstance(y, tuple):
            has_y = "pair"
            in_specs += [row, row, pl.BlockSpec((tm, y[2].shape[1]), lambda i: (i, 0))]
            args += list(y)
        else:
            has_y = True
            in_specs.append(row)
            args.append(y)
        in_specs += [pl.BlockSpec((None, None, 1, d), lambda i: (li_post, k_post, 0, 0)),
                     pl.BlockSpec((None, None, None, 1, d), lambda i: (li_post, seg(i), k_gate, 0, 0))]
        args += [norm_g, mods]
    if want_h:
        k_norm = 0 if k_mod == 0 else 2
        in_specs += [pl.BlockSpec((None, None, 1, d), lambda i: (li_pre, k_norm, 0, 0)),
                     pl.BlockSpec((None, None, None, 1, d), lambda i: (li_pre, seg(i), k_mod + 1, 0, 0)),
                     pl.BlockSpec((None, None, None, 1, d), lambda i: (li_pre, seg(i), k_mod, 0, 0))]
        args += [norm_g, mods, mods]
    if router is not None:
        in_specs.append(pl.BlockSpec(router.shape, lambda i: (0, 0)))
        args.append(router)
    out_specs, out_shape = [], []
    if want_x and out_split:
        out_specs += [row_a, row_b]
        out_shape += [jax.ShapeDtypeStruct((m_ctx, d), F32), jax.ShapeDtypeStruct((m - m_ctx, d), F32)]
    elif want_x:
        out_specs.append(row)
        out_shape.append(jax.ShapeDtypeStruct((m, d), F32))
    if want_h:
        out_specs.append(row)
        out_shape.append(jax.ShapeDtypeStruct((m, d), BF16))
        if router is not None:
            out_specs.append(pl.BlockSpec((tm, router.shape[1]), lambda i: (i, 0)))
            out_shape.append(jax.ShapeDtypeStruct((m, router.shape[1]), F32))
    body = functools.partial(_resid_norm_body, n_ctx_tiles=nct, x_split=x_split, has_y=has_y,
                             want_x=want_x, out_split=out_split, want_h=want_h,
                             has_router=router is not None)
    return pl.pallas_call(
        body, grid=(m // tm,), in_specs=in_specs, out_specs=out_specs, out_shape=out_shape,
        compiler_params=_cparams(("arbitrary",), 14 * tm * d * 4 + d * LANES * 8 + proj_bytes),
        name="resid_norm",
    )(*args)


def _matmul_body(*refs, tn, scale, act, rope_shift, rope_pattern, seq_split, n_heads):
    x_ref, w_ref = refs[0], refs[1]
    o_ref = refs[-1]
    acc = jnp.dot(x_ref[...], w_ref[...], preferred_element_type=F32)
    if scale is not None:
        acc = acc * scale
    if act == "sigmoid":
        acc = jax.nn.sigmoid(acc)
    if seq_split is not None:
        refs[-2][...] = acc.astype(BF16)
        dh = tn // n_heads
        for b in range(acc.shape[0] // seq_split):
            for h in range(n_heads):
                o_ref[b, :, h, :] = acc[b * seq_split:(b + 1) * seq_split, h * dh:(h + 1) * dh]
        return
    if rope_shift is None:
        o_ref[...] = acc.astype(o_ref.dtype)
        return
    cos, sa, sb = refs[2][...], refs[3][...], refs[4][...]
    for g in range(tn // LANES):
        a = acc[:, g * LANES:(g + 1) * LANES]
        if rope_pattern[g % len(rope_pattern)]:
            a = _rope128(a, cos, sa, sb, rope_shift)
        o_ref[:, g * LANES:(g + 1) * LANES] = a.astype(o_ref.dtype)


def _matmul(x, w, *, rows, row0, col0, ncols, wsel, out_dtype, tm, tn, scale=None, act=None,
            rope=None, state=None, name="matmul"):
    k = x.shape[1]
    assert row0 % tm == 0 and rows % tm == 0 and col0 % tn == 0 and ncols % tn == 0, (row0, rows, col0, ncols, tm, tn)
    rb0, cb0 = row0 // tm, col0 // tn
    in_specs = [pl.BlockSpec((tm, k), lambda j, i: (rb0 + i, 0)),
                pl.BlockSpec((None, k, tn), lambda j, i: (wsel, 0, cb0 + j))]
    args = [x, w]
    rope_shift = rope_pattern = None
    if rope is not None:
        tabs, rope_shift, rope_pattern = rope
        nper = tabs[0].shape[0] // tm
        assert tabs[0].shape[0] % tm == 0
        for t in tabs:
            in_specs.append(pl.BlockSpec((tm, LANES), lambda j, i: (i % nper, 0)))
            args.append(t)
    aliases, seq_split, n_heads = {}, None, None
    out_spec = pl.BlockSpec((tm, tn), lambda j, i: (i, j))
    out_shape = jax.ShapeDtypeStruct((rows, ncols), out_dtype)
    if state is not None:
        prev, li, depth, seq_split, n_heads = state
        assert tm % seq_split == 0 and rope is None and tn == ncols and out_dtype == BF16
        dh = ncols // n_heads
        out_spec = [out_spec, pl.BlockSpec((tm // seq_split, None, seq_split, n_heads, dh),
                                           lambda j, i: (i, li, 0, 0, 0))]
        out_shape = [out_shape, jax.ShapeDtypeStruct((rows // seq_split, depth, seq_split, n_heads, dh), F32)]
        if prev is not None:
            in_specs.append(pl.BlockSpec(memory_space=pl.ANY))
            aliases = {len(args): 1}
            args.append(prev)
    osz = jnp.dtype(out_dtype).itemsize
    est = 2 * (tm * k * 2 + k * tn * 2 + tm * tn * osz) + 3 * tm * tn * 4 + 6 * tm * LANES * 4
    body = functools.partial(_matmul_body, tn=tn, scale=scale, act=act, rope_shift=rope_shift,
                             rope_pattern=rope_pattern, seq_split=seq_split, n_heads=n_heads)
    return pl.pallas_call(
        body, grid=(ncols // tn, rows // tm), in_specs=in_specs, out_specs=out_spec, out_shape=out_shape,
        input_output_aliases=aliases,
        compiler_params=_cparams(("arbitrary", "arbitrary"), est),
        name=name,
    )(*args)


def _mla_prep_body(zm_ref, kr_ref, qg_ref, kg_ref, cos_ref, sa_ref, sb_ref, *rest, q_lora, kv_lora,
                   n_ctx_tiles, seq):
    cq_o, ckv_o, ckr_o = rest[-3:]
    z = zm_ref[...]
    cq_o[...] = (_rms(z[:, :q_lora]) * qg_ref[...]).astype(BF16)
    ckv = _rms(z[:, q_lora:]) * kg_ref[...]
    ckr_o[:, :kv_lora] = ckv.astype(BF16)
    i = pl.program_id(0)
    lane = lax.broadcasted_iota(jnp.int32, kr_ref.shape, 1)
    kr = jnp.where(lane < ROPE_B, kr_ref[...], 0.0)

    @pl.when(i < n_ctx_tiles)
    def _():
        ckv_o[...] = ckv.reshape(ckv.shape[0] // seq, seq, kv_lora)
        ckr_o[:, kv_lora:] = kr.astype(BF16)

    @pl.when(i >= n_ctx_tiles)
    def _():
        ckr_o[:, kv_lora:] = _rope128(kr, cos_ref[...], sa_ref[...], sb_ref[...], ROPE_B // 4).astype(BF16)


def _mla_prep(zm, kr, qg, kg, tabs, state, *, q_lora, kv_lora, m_ctx, seg_len):
    prev, li, depth, seq = state[:4]
    m = zm.shape[0]
    tm = _pick(math.gcd(m_ctx, seg_len), 512)
    assert tm % seq == 0
    nct, nper = m_ctx // tm, seg_len // tm
    tab_spec = pl.BlockSpec((tm, LANES), lambda i: (jnp.maximum(i - nct, 0) % nper, 0))
    in_specs = [pl.BlockSpec((tm, q_lora + kv_lora), lambda i: (i, 0)),
                pl.BlockSpec((tm, LANES), lambda i: (i, 0)),
                pl.BlockSpec((1, q_lora), lambda i: (0, 0)),
                pl.BlockSpec((1, kv_lora), lambda i: (0, 0)),
                tab_spec, tab_spec, tab_spec]
    args = [zm, kr, qg, kg, *tabs]
    aliases = {}
    if prev is not None:
        in_specs.append(pl.BlockSpec(memory_space=pl.ANY))
        aliases = {len(args): 1}
        args.append(prev)
    body = functools.partial(_mla_prep_body, q_lora=q_lora, kv_lora=kv_lora, n_ctx_tiles=nct, seq=seq)
    return pl.pallas_call(
        body, grid=(m // tm,), in_specs=in_specs,
        out_specs=[pl.BlockSpec((tm, q_lora), lambda i: (i, 0)),
                   pl.BlockSpec((tm // seq, None, seq, kv_lora), lambda i: (jnp.minimum(i, nct - 1), li, 0, 0)),
                   pl.BlockSpec((tm, kv_lora + LANES), lambda i: (i, 0))],
        out_shape=[jax.ShapeDtypeStruct((m, q_lora), BF16),
                   jax.ShapeDtypeStruct((m_ctx // seq, depth, seq, kv_lora), F32),
                   jax.ShapeDtypeStruct((m, kv_lora + LANES), BF16)],
        input_output_aliases=aliases,
        compiler_params=_cparams(("arbitrary",), 8 * tm * (q_lora + kv_lora + 4 * LANES) * 4),
        name="mla_prep",
    )(*args)


_NT = (((1,), (1,)), ((), ()))


def _softmax_parts(q, k):
    s = lax.dot_general(q, k, _NT, preferred_element_type=F32)
    p = jnp.exp2(s - jnp.max(s, axis=-1, keepdims=True))
    return p, 1.0 / jnp.sum(p, axis=-1, keepdims=True)


def _one_head(q, k, v, *, diff, lam, g, out_scale):
    k = k.astype(BF16)
    v = v.astype(BF16)
    if not diff:
        p, r = _softmax_parts(q, k)
        return jnp.dot(p.astype(BF16), v, preferred_element_type=F32) * r
    p1, r1 = _softmax_parts(q[:, :HD_A], k[:, :HD_A])
    p2, r2 = _softmax_parts(q[:, HD_A:], k[:, HD_A:])
    p = (p1 * r1 - p2 * (lam * r2)).astype(BF16)
    o = jnp.dot(p, v, preferred_element_type=F32)
    return _rms(o) * g * out_scale


def _ctx_attn_body(*refs, nb, nh, dq, dv, diff, out_scale):
    it = iter(refs)
    lam = next(it)[0] if diff else None
    q_ref, k_ref, v_ref = next(it), next(it), next(it)
    g = next(it)[...] if diff else None
    o_ref = refs[-1]
    s = q_ref.shape[1]
    for b in range(nb):
        for h in range(nh):
            o = _one_head(q_ref[b, :, h * dq:(h + 1) * dq], k_ref[b, :, h * dq:(h + 1) * dq],
                          v_ref[b, :, h * dv:(h + 1) * dv], diff=diff, lam=lam, g=g, out_scale=out_scale)
            o_ref[b * s:(b + 1) * s, h * dv:(h + 1) * dv] = o.astype(o_ref.dtype)


def _ctx_attn(q, k, v, *, li, nh, dq, dv, out_rows, diff=False, lam=None, g=None, out_scale=None, name):
    bsz, s, _ = q.shape
    nb = 2 if bsz % 2 == 0 else 1

    def kv_spec(a, width):
        if a.ndim == 4:
            return pl.BlockSpec((nb, None, s, width), lambda i: (i, li, 0, 0))
        return pl.BlockSpec((nb, s, width), lambda i: (i, 0, 0))

    in_specs, args = [], []
    if diff:
        in_specs.append(pl.BlockSpec(memory_space=pltpu.SMEM))
        args.append(lam)
    in_specs += [pl.BlockSpec((nb, s, nh * dq), lambda i: (i, 0, 0)), kv_spec(k, nh * dq), kv_spec(v, nh * dv)]
    args += [q, k, v]
    if diff:
        in_specs.append(pl.BlockSpec((1, dv), lambda i: (0, 0)))
        args.append(g)
    body = functools.partial(_ctx_attn_body, nb=nb, nh=nh, dq=dq, dv=dv, diff=diff, out_scale=out_scale)
    return pl.pallas_call(
        body, grid=(bsz // nb,), in_specs=in_specs,
        out_specs=pl.BlockSpec((nb * s, nh * dv), lambda i: (i, 0)),
        out_shape=jax.ShapeDtypeStruct((out_rows, nh * dv), BF16),
        compiler_params=_cparams(("arbitrary",), 4 * nb * s * nh * (2 * dq + dv) * 4 + 16 * s * s * 4 * nb * nh),
        name=name,
    )(*args)


def _softmax_parts_t(k, q):
    st = lax.dot_general(k, q, _NT, preferred_element_type=F32)
    p = jnp.exp2(st - jnp.max(st, axis=0, keepdims=True))
    return p.astype(BF16), 1.0 / jnp.sum(p, axis=0, keepdims=True)


def _lat_attn_t_body(*refs, hps, dq, dv, diff, out_scale):
    it = iter(refs)
    lam = next(it)[0] if diff else None
    q_ref, k_ref, vt_ref = next(it), next(it), next(it)
    g = next(it)[...] if diff else None
    o_ref = refs[-1]
    if diff:
        q, k, vt = q_ref[...], k_ref[...], vt_ref[...]
        p1, r1 = _softmax_parts_t(k[:, :HD_A], q[:, :HD_A])
        p2, r2 = _softmax_parts_t(k[:, HD_A:], q[:, HD_A:])
        ot = (jnp.dot(vt, p1, preferred_element_type=F32) * r1
              - jnp.dot(vt, p2, preferred_element_type=F32) * (lam * r2))
        o_ref[...] = (_rms(ot.T) * g * out_scale).astype(o_ref.dtype)
        return
    ps = [_softmax_parts_t(k_ref[:, h * dq:(h + 1) * dq], q_ref[:, h * dq:(h + 1) * dq]) for h in range(hps)]
    for h, (p, r) in enumerate(ps):
        ot = jnp.dot(vt_ref[h * dv:(h + 1) * dv, :], p, preferred_element_type=F32) * r
        o_ref[:, h * dv:(h + 1) * dv] = ot.T.astype(o_ref.dtype)


def _lat_attn_body(*refs, hps, dq, dv, diff, out_scale):
    it = iter(refs)
    lam = next(it)[0] if diff else None
    q_ref, k_ref, v_ref = next(it), next(it), next(it)
    g = next(it)[...] if diff else None
    o_ref = refs[-1]
    for h in range(hps):
        o = _one_head(q_ref[:, h * dq:(h + 1) * dq], k_ref[:, h * dq:(h + 1) * dq],
                      v_ref[:, h * dv:(h + 1) * dv], diff=diff, lam=lam, g=g, out_scale=out_scale)
        o_ref[:, h * dv:(h + 1) * dv] = o.astype(o_ref.dtype)


def _lat_attn(q, k, v, *, nh, hps, dq, dv, tq, diff=False, lam=None, g=None,
              out_scale=None, v_transposed=False, name):
    bsz, n, _ = q.shape
    nk = k.shape[1]
    assert nh % hps == 0 and n % tq == 0
    in_specs, args = [], []
    if diff:
        in_specs.append(pl.BlockSpec(memory_space=pltpu.SMEM))
        args.append(lam)
    v_spec = (pl.BlockSpec((None, hps * dv, nk), lambda b, h, i: (b, h, 0)) if v_transposed
              else pl.BlockSpec((None, nk, hps * dv), lambda b, h, i: (b, 0, h)))
    in_specs += [pl.BlockSpec((None, tq, hps * dq), lambda b, h, i: (b, i, h)),
                 pl.BlockSpec((None, nk, hps * dq), lambda b, h, i: (b, 0, h)), v_spec]
    args += [q, k, v]
    if diff:
        in_specs.append(pl.BlockSpec((1, dv), lambda b, h, i: (0, 0)))
        args.append(g)
    nqb = n // tq
    n_chain = hps * (2 if diff else 1)
    est = 2 * (tq * hps * dq + nk * hps * (dq + dv)) * 2 + 5 * n_chain * tq * nk * 4
    if v_transposed:
        assert hps == 1 or not diff
        body = functools.partial(_lat_attn_t_body, hps=hps, dq=dq, dv=dv, diff=diff, out_scale=out_scale)
    else:
        body = functools.partial(_lat_attn_body, hps=hps, dq=dq, dv=dv, diff=diff, out_scale=out_scale)
    return pl.pallas_call(
        body, grid=(bsz, nh // hps, nqb), in_specs=in_specs,
        out_specs=pl.BlockSpec((tq, hps * dv), lambda b, h, i: (b * nqb + i, h)),
        out_shape=jax.ShapeDtypeStruct((bsz * n, nh * dv), BF16),
        compiler_params=_cparams(("arbitrary", "arbitrary", "arbitrary"), est),
        name=name,
    )(*args)


def _na_geometry(rows):
    rpb = NA_ROWS_PER_BLOCK
    kr = min(NA_KR_MAX, rows)
    span = rpb + kr - 1
    assert rows % rpb == 0 and rows >= span, rows
    nblk = rows // rpb
    r = np.arange(rows)
    row_start = np.clip(r - kr // 2, 0, rows - kr)
    blk_start = np.clip(rpb * np.arange(nblk) - kr // 2, 0, rows - span)
    cols = np.arange(GRID_W)
    col_start = np.clip(cols - NA_KC // 2, 0, GRID_W - NA_KC)
    dc = cols[None, :] - cols[:, None] + (NA_KC - 1)
    ok_c = (cols[None, :] >= col_start[:, None]) & (cols[None, :] < col_start[:, None] + NA_KC)
    sigs, cls_of_blk, dr_l, ok_l = {}, [], [], []
    for b in range(nblk):
        qr = rpb * b + np.arange(rpb)
        key_row = blk_start[b] + np.arange(span)
        dr = key_row[None, :] - qr[:, None] + (NA_KR_MAX - 1)
        ok_r = (key_row[None, :] >= row_start[qr][:, None]) & (key_row[None, :] < row_start[qr][:, None] + kr)
        sig = (dr.tobytes(), ok_r.tobytes())
        if sig not in sigs:
            sigs[sig] = len(sigs)
            dr_l.append(np.clip(dr, 0, 2 * NA_KR_MAX - 2))
            ok = ok_r[:, None, :, None] & ok_c[None, :, None, :]
            ok_l.append(ok.reshape(rpb * GRID_W, span * GRID_W))
        cls_of_blk.append(sigs[sig])
    return dict(span=span, nblk=nblk, blk_start=blk_start.astype(np.int32), cls=np.asarray(cls_of_blk, np.int32),
                dr=np.stack(dr_l), dc=np.clip(dc, 0, 2 * NA_KC - 2), ok=np.stack(ok_l))


def _na_bias(tables, na):
    depth, nh, n_dr, n_dc = tables.shape
    ncls, rpb, span = na["dr"].shape
    oh_r = jax.nn.one_hot(na["dr"].reshape(-1), n_dr, dtype=F32)
    oh_c = jax.nn.one_hot(na["dc"].reshape(-1), n_dc, dtype=F32).T
    t = jnp.einsum("xr,lhrc->lhxc", oh_r, tables.astype(F32), precision=HIGHEST)
    t = jnp.einsum("lhxc,cy->lhxy", t, oh_c, precision=HIGHEST)
    t = t.reshape(depth, nh, ncls, rpb, span, GRID_W, GRID_W).transpose(0, 1, 2, 3, 5, 4, 6)
    t = t.reshape(depth, nh, ncls, rpb * GRID_W, span * GRID_W)
    return jnp.where(na["ok"][None, None], t * LOG2E, MASK_VALUE)


def _na_body(cls_ref, start_ref, q_ref, k_ref, v_ref, kc_ref, vc_ref, bias_ref, o_ref, *, span_tok, hps):
    del cls_ref
    blk = pl.program_id(2)
    start = pl.multiple_of(start_ref[blk] * GRID_W, GRID_W)
    scores = []
    for h in range(hps):
        c = slice(h * HD_C, (h + 1) * HD_C)
        q = q_ref[:, c]
        s_loc = lax.dot_general(q, k_ref[pl.ds(start, span_tok), c], _NT, preferred_element_type=F32) + bias_ref[h]
        s_ctx = lax.dot_general(q, kc_ref[:, c].astype(BF16), _NT, preferred_element_type=F32)
        scores.append((s_loc, s_ctx))
    for h, (s_loc, s_ctx) in enumerate(scores):
        c = slice(h * HD_C, (h + 1) * HD_C)
        mx = jnp.maximum(jnp.max(s_loc, axis=-1, keepdims=True), jnp.max(s_ctx, axis=-1, keepdims=True))
        p_loc = jnp.exp2(s_loc - mx)
        p_ctx = jnp.exp2(s_ctx - mx)
        r = 1.0 / (jnp.sum(p_loc, axis=-1, keepdims=True) + jnp.sum(p_ctx, axis=-1, keepdims=True))
        o = jnp.dot(p_loc.astype(BF16), v_ref[pl.ds(start, span_tok), c], preferred_element_type=F32)
        o = o + jnp.dot(p_ctx.astype(BF16), vc_ref[:, c].astype(BF16), preferred_element_type=F32)
        o_ref[:, c] = (o * r).astype(o_ref.dtype)


def _na_lat_attn(q, kv, cache_k, cache_v, bias, na, *, li):
    bsz, n, _ = q.shape
    past = cache_k.shape[2]
    rpb_tok, span_tok, nblk = NA_ROWS_PER_BLOCK * GRID_W, na["span"] * GRID_W, na["nblk"]
    hps = NA_HEADS_PER_STEP
    w = hps * HD_C
    ng = H_C // hps
    gs = pltpu.PrefetchScalarGridSpec(
        num_scalar_prefetch=2, grid=(bsz, ng, nblk),
        in_specs=[pl.BlockSpec((None, rpb_tok, w), lambda b, h, i, cl, st: (b, i, h)),
                  pl.BlockSpec((None, n, w), lambda b, h, i, cl, st: (b, 0, h)),
                  pl.BlockSpec((None, n, w), lambda b, h, i, cl, st: (b, 0, ng + h)),
                  pl.BlockSpec((None, None, past, w), lambda b, h, i, cl, st: (b, li, 0, h)),
                  pl.BlockSpec((None, None, past, w), lambda b, h, i, cl, st: (b, li, 0, h)),
                  pl.BlockSpec((None, hps, None, rpb_tok, span_tok), lambda b, h, i, cl, st: (li, h, cl[i], 0, 0))],
        out_specs=pl.BlockSpec((rpb_tok, w), lambda b, h, i, cl, st: (b * nblk + i, h)))
    return pl.pallas_call(
        functools.partial(_na_body, span_tok=span_tok, hps=hps),
        grid_spec=gs, out_shape=jax.ShapeDtypeStruct((bsz * n, H_C * HD_C), BF16),
        compiler_params=_cparams(("arbitrary", "arbitrary", "arbitrary"),
                                 8 * n * w * 2 + hps * 12 * rpb_tok * span_tok * 4),
        name="na_attn_lat",
    )(jnp.asarray(na["cls"]), jnp.asarray(na["blk_start"]), q, kv, kv, cache_k, cache_v, bias)


def _merge_body(*refs, n_ctx_tiles):
    br, (ga, gb, gc), (wa, wb, wc), o_ref = refs[:6], refs[6:9], refs[9:12], refs[12]
    is_ctx = pl.program_id(1) < n_ctx_tiles
    acc = None
    for s, (g_ref, w_ref) in enumerate(((ga, wa), (gb, wb), (gc, wc))):
        a = jnp.where(is_ctx, br[2 * s][...], br[2 * s + 1][...])
        t = g_ref[...].astype(F32) * jnp.dot(a, w_ref[...], preferred_element_type=F32)
        acc = t if acc is None else acc + t
    o_ref[...] = acc.astype(o_ref.dtype)


def _merge(branches, gates, wb, li):
    m_ctx, kb = branches[0][0].shape
    m = m_ctx + branches[0][1].shape[0]
    d = wb.shape[-1]
    tm, tn = _pick(math.gcd(m_ctx, m - m_ctx), 512), _pick(d, 1024, 256)
    nj, nct = d // tn, m_ctx // tm
    br_ctx = pl.BlockSpec((tm, kb), lambda j, i: (jnp.minimum(i, nct - 1), 0))
    br_lat = pl.BlockSpec((tm, kb), lambda j, i: (jnp.maximum(i - nct, 0), 0))
    gspec = [pl.BlockSpec((tm, tn), functools.partial(lambda j, i, s: (i, s * nj + j), s=s)) for s in range(N_BRANCH)]
    wspec = [pl.BlockSpec((None, None, kb, tn), functools.partial(lambda j, i, s: (li, s, 0, j), s=s)) for s in range(N_BRANCH)]
    est = 2 * (6 * tm * kb * 2 + 3 * tm * tn * 2 + 3 * kb * tn * 2 + tm * tn * 2) + 4 * tm * tn * 4
    return pl.pallas_call(
        functools.partial(_merge_body, n_ctx_tiles=nct), grid=(nj, m // tm),
        in_specs=[br_ctx, br_lat] * N_BRANCH + gspec + wspec,
        out_specs=pl.BlockSpec((tm, tn), lambda j, i: (i, j)),
        out_shape=jax.ShapeDtypeStruct((m, d), BF16),
        compiler_params=_cparams(("arbitrary", "arbitrary"), est),
        name="branch_merge",
    )(*[a for pair in branches for a in pair], gates, gates, gates, wb, wb, wb)


def _ffn1_body(te_ref, ts_ref, nu_ref, x_ref, w13_ref, o_ref):
    del te_ref, ts_ref

    @pl.when(pl.program_id(1) < nu_ref[0])
    def _():
        tf = o_ref.shape[1]
        h = jnp.dot(x_ref[...], w13_ref[...], preferred_element_type=F32)
        h1, h3 = h[:, :tf], h[:, tf:]
        o_ref[...] = (h1 * jax.nn.sigmoid(h1) * h3).astype(o_ref.dtype)


def _ffn2_body(te_ref, ts_ref, nu_ref, h_ref, w2_ref, o_ref):
    del te_ref, ts_ref

    @pl.when(pl.program_id(1) < nu_ref[0])
    def _():
        o_ref[...] = jnp.dot(h_ref[...], w2_ref[...], preferred_element_type=F32).astype(o_ref.dtype)


def _ffn_tile(f):
    return _pick(f, 1408, LANES)


def _cast_interleave_body(w1_ref, w3_ref, o_ref):
    tf = w1_ref.shape[1]
    o_ref[:, :tf] = w1_ref[...].astype(BF16)
    o_ref[:, tf:] = w3_ref[...].astype(BF16)


def _cast_interleave(w1, w3):
    nl, ne, d, f = w1.shape
    tf = _ffn_tile(f)
    tk = _pick(d, 1024)
    blk = pl.BlockSpec((None, None, tk, tf), lambda l, e, k, j: (l, e, k, j))
    return pl.pallas_call(
        _cast_interleave_body, grid=(nl, ne, d // tk, f // tf), in_specs=[blk, blk],
        out_specs=pl.BlockSpec((None, None, tk, 2 * tf), lambda l, e, k, j: (l, e, k, j)),
        out_shape=jax.ShapeDtypeStruct((nl, ne, d, 2 * f), BF16),
        compiler_params=_cparams(("arbitrary",) * 4, 2 * (2 * tk * tf * 4 + tk * 2 * tf * 2) + 2 * tk * tf * 4),
        name="cast_interleave",
    )(w1, w3)


def _grouped_ffn(xs, w13, w2, lsel, tile_expert, tile_src, n_used, tm, out_dtype=F32):
    r, d = xs.shape
    f = w2.shape[2]
    nt = r // tm
    tf = _ffn_tile(f)
    tn = _pick(d, 1024, 256)
    gs1 = pltpu.PrefetchScalarGridSpec(
        num_scalar_prefetch=3, grid=(f // tf, nt),
        in_specs=[pl.BlockSpec((tm, d), lambda j, t, te, ts, nu: (ts[t], 0)),
                  pl.BlockSpec((None, None, d, 2 * tf), lambda j, t, te, ts, nu: (lsel, te[t], 0, j))],
        out_specs=pl.BlockSpec((tm, tf), lambda j, t, te, ts, nu: (ts[t], j)))
    hid = pl.pallas_call(
        _ffn1_body, grid_spec=gs1, out_shape=jax.ShapeDtypeStruct((r, f), BF16),
        compiler_params=_cparams(("arbitrary", "arbitrary"),
                                 2 * (tm * d * 2 + 2 * d * tf * 2 + tm * tf * 2) + 4 * tm * tf * 4),
        name="ffn_up",
    )(tile_expert, tile_src, n_used, xs, w13)
    gs2 = pltpu.PrefetchScalarGridSpec(
        num_scalar_prefetch=3, grid=(d // tn, nt),
        in_specs=[pl.BlockSpec((tm, f), lambda j, t, te, ts, nu: (ts[t], 0)),
                  pl.BlockSpec((None, None, f, tn), lambda j, t, te, ts, nu: (lsel, te[t], 0, j))],
        out_specs=pl.BlockSpec((tm, tn), lambda j, t, te, ts, nu: (ts[t], j)))
    return pl.pallas_call(
        _ffn2_body, grid_spec=gs2, out_shape=jax.ShapeDtypeStruct((r, d), out_dtype),
        compiler_params=_cparams(("arbitrary", "arbitrary"),
                                 2 * (tm * f * 2 + f * tn * 2 + tm * tn * 4) + 2 * tm * tn * 4),
        name="ffn_down",
    )(tile_expert, tile_src, n_used, hid, w2)


def _moe_plan(logits, tm):
    m = logits.shape[0]
    top_v, top_i = lax.top_k(logits, TOP_K)
    top_w = jax.nn.softmax(top_v, axis=-1)
    flat_e = top_i.reshape(-1)
    onehot = (flat_e[:, None] == jnp.arange(N_EXPERTS)[None, :]).astype(jnp.int32)
    csum = jnp.cumsum(onehot, axis=0)
    counts = csum[-1]
    rank = jnp.take_along_axis(csum, flat_e[:, None], axis=1)[:, 0] - 1
    padded = ((counts + tm - 1) // tm) * tm
    ends = jnp.cumsum(padded)
    offs = ends - padded
    dest = offs[flat_e] + rank
    n_rows = TOP_K * m + N_EXPERTS * tm
    nt = n_rows // tm
    row_token = jnp.zeros((n_rows,), jnp.int32).at[dest].set(jnp.arange(TOP_K * m, dtype=jnp.int32) // TOP_K)
    n_used = (ends[-1] // tm).astype(jnp.int32)
    tile_src = jnp.minimum(jnp.arange(nt, dtype=jnp.int32), n_used - 1)
    tile_expert = jnp.sum((tile_src[:, None] * tm >= ends[None, :]).astype(jnp.int32), axis=1)
    tile_expert = jnp.minimum(tile_expert, N_EXPERTS - 1).astype(jnp.int32)
    return row_token, dest.reshape(m, TOP_K), top_w, tile_expert, tile_src, n_used.reshape(1)


def _rows(a, idx):
    return a.at[idx].get(mode="promise_in_bounds")


def _rope_tables(n_tok, dim):
    half = dim // 2
    t = jnp.arange(n_tok)
    row = (t // GRID_W).astype(F32)
    col = (t % GRID_W).astype(F32)
    inv = ROPE_THETA ** (-jnp.arange(0, half, 2, dtype=F32) / half)
    ar = row[:, None] * inv[None, :]
    ac = col[:, None] * inv[None, :]
    ang = jnp.concatenate([ar, ar, ac, ac], axis=-1)
    cos, sin = jnp.cos(ang), jnp.sin(ang)
    first = (np.arange(dim) % half) < (dim // 4)
    sa = jnp.where(first[None, :], -sin, 0.0)
    sb = jnp.where(first[None, :], 0.0, sin)
    pad = LANES - dim
    if pad:
        cos = jnp.pad(cos, ((0, 0), (0, pad)), constant_values=1.0)
        sa = jnp.pad(sa, ((0, 0), (0, pad)))
        sb = jnp.pad(sb, ((0, 0), (0, pad)))
    return cos, sa, sb


def kernel(x_prompt, x_sample, cache_diff_k, cache_diff_v, cache_mla_ckv, cache_mla_krope, cache_na_k, cache_na_v, c, c_ctx, norm_g, ada_w, ada_b, w_in, diff_lambda, diff_subln_g, mla_q_norm_g, mla_kv_norm_g, mla_w_uq, mla_w_ukv, na_rel_bias, w_branch, w_out, ffn_w1, ffn_w3, ffn_w2, moe_router, moe_w1, moe_w3, moe_w2):
    batch, seq, d = x_prompt.shape
    dec_batch, dec_seq, _ = x_sample.shape
    depth = norm_g.shape[0]
    past = cache_diff_k.shape[2]
    q_lora, kv_lora = mla_q_norm_g.shape[1], mla_kv_norm_g.shape[1]
    m_ctx, m_lat = batch * seq, dec_batch * dec_seq
    m = m_ctx + m_lat
    nk_lat = dec_seq + past
    assert dec_seq % GRID_W == 0 and kv_lora % LANES == 0 and q_lora % LANES == 0

    a3 = 3 * BR_W
    col_m = a3
    col_kr = a3 + q_lora + kv_lora
    c_na = col_kr + ROPE_B
    n_rest = 3 * BR_W + N_BRANCH * d
    w_head = _cast_head(w_in, col_kr + LANES)
    w_rest = _cast_shift(w_in, c_na, n_rest, 768)
    col_c, col_g = 0, 3 * BR_W
    wq = mla_w_uq.reshape(depth, q_lora, H_B, NOPE_B + ROPE_B)
    w_uqp = jnp.pad(wq, ((0, 0), (0, 0), (0, 0), (0, 2 * LANES - NOPE_B - ROPE_B))).reshape(depth, q_lora, H_B * 2 * LANES).astype(BF16)
    wkv = mla_w_ukv.reshape(depth, kv_lora, H_B, NOPE_B + V_B)
    wk_top = jnp.pad(wkv[..., :NOPE_B], ((0, 0), (0, 0), (0, 0), (0, 2 * LANES - NOPE_B))).reshape(depth, kv_lora, H_B * 2 * LANES)
    eye = np.zeros((LANES, H_B, 2 * LANES), np.float32)
    for r_ in range(ROPE_B):
        eye[r_, :, NOPE_B + r_] = 1.0
    wk_aug = jnp.concatenate([wk_top, jnp.broadcast_to(jnp.asarray(eye.reshape(LANES, -1)), (depth, LANES, H_B * 2 * LANES))], axis=1).astype(BF16)
    wv_aug = jnp.pad(wkv[..., NOPE_B:].reshape(depth, kv_lora, H_B * V_B), ((0, 0), (0, LANES), (0, 0))).astype(BF16)
    w_branch_b = w_branch.astype(BF16)
    w_out_b = w_out.astype(BF16)
    ffn_w13b, ffn_w2b = _cast_interleave(ffn_w1[:, None], ffn_w3[:, None]), ffn_w2.astype(BF16)[:, None]
    if depth > 1:
        moe_w13b, moe_w2b = _cast_interleave(moe_w1, moe_w3), moe_w2.astype(BF16)
    router_p = jnp.pad(moe_router, ((0, 0), (0, 0), (0, LANES - N_EXPERTS)))
    norm_g4 = norm_g.reshape(depth, 4, 1, d)

    n_seg = 1 + dec_batch
    r_pad = -(-n_seg // 8) * 8
    cvec = jnp.zeros((r_pad, d), F32).at[0].set(c_ctx).at[1:n_seg].set(c)
    mods = _ada_mod(cvec, ada_w, ada_b).reshape(depth, r_pad, 6, 1, d)

    tabs_a = _rope_tables(dec_seq, HD_A)
    tabs_b = _rope_tables(dec_seq, ROPE_B)
    na = _na_geometry(dec_seq // GRID_W)
    na_bias = _na_bias(na_rel_bias, na)
    lam_inits = [0.8 - 0.6 * math.exp(-0.3 * li) for li in range(depth)]
    lp = diff_lambda.astype(F32)
    lams = jnp.exp(jnp.sum(lp[:, 0] * lp[:, 1], axis=-1)) - jnp.exp(jnp.sum(lp[:, 2] * lp[:, 3], axis=-1)) + jnp.asarray(lam_inits, F32)

    tm_big = _pick(math.gcd(m_ctx, dec_seq), 1024)
    assert tm_big % seq == 0
    tq_lat = _pick(dec_seq, 512)
    rn = functools.partial(_resid_norm, m_ctx=m_ctx, seg_len=dec_seq)
    mm = functools.partial(_matmul, tm=tm_big)
    qs_a, qs_b, qs_c = HD_A ** -0.5 * LOG2E, (NOPE_B + ROPE_B) ** -0.5 * LOG2E, HD_C ** -0.5 * LOG2E
    hb2 = H_B * 2 * LANES
    cache_k_c = cache_na_k.reshape(dec_batch, depth, past, H_C * HD_C)
    cache_v_c = cache_na_v.reshape(dec_batch, depth, past, H_C * HD_C)

    x = (x_prompt.reshape(m_ctx, d), x_sample.reshape(m_lat, d))
    (hmix,) = rn(x, None, mods, norm_g4, li_post=0, k_gate=2, li_pre=0, k_mod=0, want_x=False, want_h=True)
    st_dk = jnp.zeros((batch, depth, seq, H_A, 2 * HD_A), F32)
    st_dv = jnp.zeros((batch, depth, seq, H_A, 2 * HD_A), F32)
    st_nk = jnp.zeros((batch, depth, seq, H_C, HD_C), F32)
    st_nv = jnp.zeros((batch, depth, seq, H_C, HD_C), F32)
    st_ckv = jnp.zeros((batch, depth, seq, kv_lora), F32)
    st_kr = []

    for li in range(depth):
        def st(prev, n_heads=None):
            return (prev, li, depth, seq, n_heads)

        def b3(a):
            return a.reshape(batch, seq, a.shape[-1])

        ctx = dict(rows=m_ctx, row0=0, wsel=li, tn=1024)
        lat = dict(rows=m_lat, row0=m_ctx, wsel=li, tn=1024)
        qa_ctx = mm(hmix, w_head, col0=0, ncols=BR_W, out_dtype=BF16, scale=qs_a, name="in_qa_ctx", **ctx)
        ka_ctx, st_dk = mm(hmix, w_head, col0=BR_W, ncols=BR_W, out_dtype=BF16, state=st(st_dk, H_A), name="in_ka_ctx", **ctx)
        va_ctx, st_dv = mm(hmix, w_head, col0=2 * BR_W, ncols=BR_W, out_dtype=BF16, state=st(st_dv, H_A), name="in_va_ctx", **ctx)
        qa_lat = mm(hmix, w_head, col0=0, ncols=BR_W, out_dtype=BF16, scale=qs_a,
                    rope=(tabs_a, HD_A // 4, (True,)), name="in_qa_lat", **lat)
        ka_lat = mm(hmix, w_head, col0=BR_W, ncols=BR_W, out_dtype=BF16,
                    rope=(tabs_a, HD_A // 4, (True,)), name="in_ka_lat", **lat)
        va_lat = mm(hmix, w_head, col0=2 * BR_W, ncols=BR_W, out_dtype=BF16, name="in_va_lat", **lat)
        qc_ctx = mm(hmix, w_rest, col0=col_c, ncols=BR_W, out_dtype=BF16, scale=qs_c, name="in_qc_ctx", **ctx)
        kc_ctx, st_nk = mm(hmix, w_rest, col0=col_c + BR_W, ncols=BR_W, out_dtype=BF16, state=st(st_nk, H_C), name="in_kc_ctx", **ctx)
        vc_ctx, st_nv = mm(hmix, w_rest, col0=col_c + 2 * BR_W, ncols=BR_W, out_dtype=BF16, state=st(st_nv, H_C), name="in_vc_ctx", **ctx)
        qc_lat = mm(hmix, w_rest, col0=col_c, ncols=BR_W, out_dtype=BF16, scale=qs_c, name="in_qc_lat", **lat)
        kvc_lat = mm(hmix, w_rest, col0=col_c + BR_W, ncols=2 * BR_W, out_dtype=BF16, name="in_kvc_lat", **lat)
        gates = mm(hmix, w_rest, rows=m, row0=0, col0=col_g, ncols=N_BRANCH * d, wsel=li, out_dtype=BF16, tn=768,
                   act="sigmoid", name="in_gates")
        zm = mm(hmix, w_head, rows=m, row0=0, col0=col_m, ncols=q_lora + kv_lora, wsel=li, out_dtype=F32,
                tn=q_lora + kv_lora, name="in_mla")
        kr = mm(hmix, w_head, rows=m, row0=0, col0=col_kr, ncols=LANES, wsel=li, out_dtype=F32, tn=LANES, name="in_krope")
        st_kr.append(kr[:m_ctx, :ROPE_B].reshape(batch, seq, ROPE_B))

        g_sub = diff_subln_g[li].reshape(1, 2 * HD_A)
        lam = lams[li].reshape(1)
        oa_ctx = _ctx_attn(b3(qa_ctx), b3(ka_ctx), b3(va_ctx), li=li, nh=H_A, dq=2 * HD_A, dv=2 * HD_A,
                           out_rows=m_ctx, diff=True, lam=lam, g=g_sub, out_scale=1.0 - lam_inits[li], name="diff_attn_ctx")
        ka_all = jnp.concatenate([ka_lat.reshape(dec_batch, dec_seq, BR_W),
                                  cache_diff_k[:, li].reshape(dec_batch, past, BR_W).astype(BF16)], axis=1)
        va_all = jnp.concatenate([va_lat.reshape(dec_batch, dec_seq, BR_W),
                                  cache_diff_v[:, li].reshape(dec_batch, past, BR_W).astype(BF16)], axis=1)
        oa_lat = _lat_attn(qa_lat.reshape(dec_batch, dec_seq, BR_W), ka_all, jnp.swapaxes(va_all, 1, 2), nh=H_A,
                           hps=1, dq=2 * HD_A, dv=2 * HD_A, tq=tq_lat, diff=True, lam=lam, g=g_sub,
                           out_scale=1.0 - lam_inits[li], v_transposed=True, name="diff_attn_lat")

        cqn, st_ckv, ckr = _mla_prep(zm, kr, mla_q_norm_g[li].reshape(1, q_lora), mla_kv_norm_g[li].reshape(1, kv_lora),
                                     tabs_b, st(st_ckv), q_lora=q_lora, kv_lora=kv_lora, m_ctx=m_ctx, seg_len=dec_seq)
        qb_ctx = mm(cqn, w_uqp, col0=0, ncols=hb2, out_dtype=BF16, scale=qs_b, name="mla_q_ctx", **ctx)
        qb_lat = mm(cqn, w_uqp, col0=0, ncols=hb2, out_dtype=BF16, scale=qs_b,
                    rope=(tabs_b, ROPE_B // 4, (False, True)), name="mla_q_lat", **lat)
        cache_ckr = jnp.concatenate([cache_mla_ckv[:, li], cache_mla_krope[:, li],
                                     jnp.zeros((dec_batch, past, LANES - ROPE_B), F32)], axis=-1).astype(BF16)
        ckr_lat = jnp.concatenate([ckr[m_ctx:].reshape(dec_batch, dec_seq, kv_lora + LANES), cache_ckr], axis=1)
        ckr_lat = ckr_lat.reshape(dec_batch * nk_lat, kv_lora + LANES)
        tm_kv = _pick(math.gcd(m_ctx, dec_batch * nk_lat), 1024)
        kv_mm = functools.partial(_matmul, row0=0, col0=0, wsel=li, out_dtype=BF16, tm=tm_kv, tn=1024)
        kb_ctx = kv_mm(ckr, wk_aug, rows=m_ctx, ncols=hb2, name="mla_k_ctx")
        vb_ctx = kv_mm(ckr, wv_aug, rows=m_ctx, ncols=H_B * V_B, name="mla_v_ctx")
        kb_lat = kv_mm(ckr_lat, wk_aug, rows=dec_batch * nk_lat, ncols=hb2, name="mla_k_lat")
        vb_lat = kv_mm(ckr_lat, wv_aug, rows=dec_batch * nk_lat, ncols=H_B * V_B, name="mla_v_lat")
        ob_ctx = _ctx_attn(qb_ctx.reshape(batch, seq, hb2), kb_ctx.reshape(batch, seq, hb2),
                           vb_ctx.reshape(batch, seq, H_B * V_B), li=li, nh=H_B, dq=2 * LANES, dv=V_B, out_rows=m_ctx,
                           name="mla_attn_ctx")
        vbt_lat = jnp.swapaxes(vb_lat.reshape(dec_batch, nk_lat, H_B * V_B), 1, 2)
        ob_lat = _lat_attn(qb_lat.reshape(dec_batch, dec_seq, hb2), kb_lat.reshape(dec_batch, nk_lat, hb2),
                           vbt_lat, nh=H_B, hps=2, dq=2 * LANES, dv=V_B,
                           tq=tq_lat, v_transposed=True, name="mla_attn_lat")

        oc_ctx = _ctx_attn(b3(qc_ctx), b3(kc_ctx), b3(vc_ctx), li=li, nh=H_C, dq=HD_C, dv=HD_C, out_rows=m_ctx,
                           name="na_attn_ctx")
        oc_lat = _na_lat_attn(qc_lat.reshape(dec_batch, dec_seq, BR_W), kvc_lat.reshape(dec_batch, dec_seq, 2 * BR_W),
                              cache_k_c, cache_v_c, na_bias, na, li=li)

        merged = _merge(((oa_ctx, oa_lat), (ob_ctx, ob_lat), (oc_ctx, oc_lat)), gates, w_branch_b, li)
        if li % 2 == 0:
            y = (merged, w_out_b)
        else:
            y = mm(merged, w_out_b, rows=m, row0=0, col0=0, ncols=d, wsel=li, out_dtype=F32,
                   tn=_pick(d, 1024, 256), name="w_out")

        j = li // 2
        if li % 2 == 0:
            x, hff = rn(x, y, mods, norm_g4, li_post=li, k_gate=2, li_pre=li, k_mod=3, want_x=True, want_h=True)
            tm_f = _pick(m, 512)
            nt = m // tm_f
            yff = _grouped_ffn(hff, ffn_w13b, ffn_w2b, j,
                               jnp.zeros((nt,), jnp.int32), jnp.arange(nt, dtype=jnp.int32),
                               jnp.full((1,), nt, jnp.int32), tm_f)
        else:
            x, hff, logits = rn(x, y, mods, norm_g4, li_post=li, k_gate=2, li_pre=li, k_mod=3, want_x=True, want_h=True,
                                router=router_p[j])
            tm_e = _pick(TOP_K * m, 512)
            row_token, dest, top_w, tile_expert, tile_src, n_used = _moe_plan(logits[:, :N_EXPERTS], tm_e)
            xs = _rows(hff, row_token)
            ys = _grouped_ffn(xs, moe_w13b, moe_w2b, j, tile_expert, tile_src, n_used, tm_e, out_dtype=BF16)
            yff = (_rows(ys, dest[:, 0]), _rows(ys, dest[:, 1]), top_w)
        if li + 1 < depth:
            x, hmix = rn(x, yff, mods, norm_g4, li_post=li, k_gate=5, li_pre=li + 1, k_mod=0, want_x=True, want_h=True)
        else:
            y_prompt, y_sample = rn(x, yff, mods, norm_g4, li_post=li, k_gate=5, li_pre=li, k_mod=0, want_x=True,
                                    want_h=False, out_split=True)

    return (y_prompt.reshape(batch, seq, d), y_sample.reshape(dec_batch, dec_seq, d),
            st_dk, st_dv, st_ckv, jnp.stack(st_kr, axis=1), st_nk, st_nv)
```

```python
import functools
import math

import numpy as np
import jax
import jax.numpy as jnp
from jax import lax
from jax.experimental import pallas as pl
from jax.experimental.pallas import tpu as pltpu

GRID_W = 64
RMS_EPS = 1e-6
ROPE_THETA = 10000.0
H_A, HD_A = 4, 128
H_B, NOPE_B, ROPE_B, V_B = 8, 128, 64, 128
H_C, HD_C = 8, 128
NA_KR_MAX, NA_KC = 8, 16
N_BRANCH = 3
N_EXPERTS = 8
TOP_K = 2
BR_W = 1024

LANES = 128
V7X_VMEM_BYTES = 64 * 1024 * 1024
VMEM_CAP_BYTES = V7X_VMEM_BYTES - 8 * 1024 * 1024
NA_ROWS_PER_BLOCK = 4
NA_HEADS_PER_STEP = 4
MASK_VALUE = -1e30
LOG2E = math.log2(math.e)

F32 = jnp.float32
BF16 = jnp.bfloat16
HIGHEST = lax.Precision.HIGHEST


def _pick(n, pref, mult=8):
    t = min(pref, n)
    t -= t % mult
    while t > mult and n % t:
        t -= mult
    assert t > 0 and n % t == 0, (n, pref, mult)
    return t


def _cparams(sems, vmem_est):
    limit = int(min(max(vmem_est, 16 * 1024 * 1024), VMEM_CAP_BYTES))
    return pltpu.CompilerParams(dimension_semantics=sems, vmem_limit_bytes=limit)


def _rms(v):
    return v * lax.rsqrt(jnp.mean(v * v, axis=-1, keepdims=True) + RMS_EPS)


def _rope128(a, cos, sa, sb, shift):
    return a * cos + pltpu.roll(a, LANES - shift, 1) * sa + pltpu.roll(a, shift, 1) * sb


def _ada_body(c_ref, w_ref, b_ref, o_ref):
    c = c_ref[...]
    s = (c * jax.nn.sigmoid(c)).astype(BF16)
    o_ref[...] = jnp.dot(s, w_ref[...].astype(BF16), preferred_element_type=F32) + b_ref[...]


def _ada_mod(cvec, ada_w, ada_b):
    depth, d, n = ada_w.shape
    r = cvec.shape[0]
    tn = _pick(n, 1024, LANES)
    return pl.pallas_call(
        _ada_body,
        grid=(depth, n // tn),
        in_specs=[pl.BlockSpec((r, d), lambda l, j: (0, 0)),
                  pl.BlockSpec((None, d, tn), lambda l, j: (l, 0, j)),
                  pl.BlockSpec((None, 1, tn), lambda l, j: (l, 0, j))],
        out_specs=pl.BlockSpec((None, r, tn), lambda l, j: (l, 0, j)),
        out_shape=jax.ShapeDtypeStruct((depth, r, n), F32),
        compiler_params=_cparams(("arbitrary", "arbitrary"), 3 * d * tn * 4 + d * tn * 2),
        name="ada_mod",
    )(cvec, ada_w, ada_b.reshape(depth, 1, n))


def _cast_shift_body(a_ref, b_ref, o_ref, *, tn):
    half = LANES // 2
    a = pltpu.roll(a_ref[...], tn - half, 1)
    b = pltpu.roll(b_ref[...], half, 1)
    lane = lax.broadcasted_iota(jnp.int32, b.shape, 1)
    o_ref[:, :tn - LANES] = a[:, :tn - LANES].astype(BF16)
    o_ref[:, tn - LANES:] = jnp.where(lane < half, a[:, tn - LANES:], b).astype(BF16)


def _cast_body(w_ref, o_ref):
    o_ref[...] = w_ref[...].astype(BF16)


def _cast_head(w, n):
    depth, k, _ = w.shape
    assert n % LANES == 0
    tk = _pick(k, 256)
    blk = pl.BlockSpec((None, tk, n), lambda l, i: (l, i, 0))
    return pl.pallas_call(
        _cast_body, grid=(depth, k // tk), in_specs=[blk], out_specs=blk,
        out_shape=jax.ShapeDtypeStruct((depth, k, n), BF16),
        compiler_params=_cparams(("arbitrary", "arbitrary"), 2 * tk * n * 6 + tk * n * 4),
        name="cast_head",
    )(w)


def _cast_shift(w, c0, n, tn):
    depth, k, _ = w.shape
    a0 = c0 - LANES // 2
    assert a0 % tn == 0 and n % tn == 0 and tn % LANES == 0
    ab, r = a0 // tn, tn // LANES
    return pl.pallas_call(
        functools.partial(_cast_shift_body, tn=tn), grid=(depth, n // tn),
        in_specs=[pl.BlockSpec((None, k, tn), lambda l, j: (l, 0, ab + j)),
                  pl.BlockSpec((None, k, LANES), lambda l, j: (l, 0, (ab + j + 1) * r))],
        out_specs=pl.BlockSpec((None, k, tn), lambda l, j: (l, 0, j)),
        out_shape=jax.ShapeDtypeStruct((depth, k, n), BF16),
        compiler_params=_cparams(("arbitrary", "arbitrary"), 2 * k * (tn + LANES) * 4 + 2 * k * tn * 2 + 2 * k * tn * 4),
        name="cast_shift",
    )(w, w)


def _resid_norm_body(*refs, n_ctx_tiles, x_split, has_y, want_x, out_split, want_h, has_router):
    it = iter(refs)
    i = pl.program_id(0)
    if x_split:
        xa_ref, xb_ref = next(it), next(it)
        x = jnp.where(i < n_ctx_tiles, xa_ref[...], xb_ref[...])
    else:
        x = next(it)[...]
    if has_y == "pair":
        ya_ref, yb_ref, w_ref = next(it), next(it), next(it)
        w = w_ref[...]
        y = w[:, 0:1] * ya_ref[...].astype(F32) + w[:, 1:2] * yb_ref[...].astype(F32)
    elif has_y == "proj":
        a_ref, w_ref = next(it), next(it)
        y = jnp.dot(a_ref[...], w_ref[...], preferred_element_type=F32)
    elif has_y:
        y = next(it)[...].astype(F32)
    if has_y:
        gpost_ref, gate_ref = next(it), next(it)
        x = x + gate_ref[...] * (_rms(y) * gpost_ref[...])
    if want_h:
        gpre_ref, sc_ref, sh_ref = next(it), next(it), next(it)
    if has_router:
        r_ref = next(it)
    if want_x and out_split:
        xa_o, xb_o = next(it), next(it)

        @pl.when(i < n_ctx_tiles)
        def _():
            xa_o[...] = x

        @pl.when(i >= n_ctx_tiles)
        def _():
            xb_o[...] = x
    elif want_x:
        next(it)[...] = x
    if want_h:
        h = _rms(x) * gpre_ref[...]
        h = h * (1.0 + sc_ref[...]) + sh_ref[...]
        next(it)[...] = h.astype(BF16)
        if has_router:
            next(it)[...] = jnp.dot(h, r_ref[...], precision=HIGHEST, preferred_element_type=F32)


def _resid_norm(x, y, mods, norm_g, *, li_post, k_gate, li_pre, k_mod, m_ctx, seg_len,
                want_x, want_h, out_split=False, router=None):
    x_split = isinstance(x, tuple)
    d = x[0].shape[1] if x_split else x.shape[1]
    m = m_ctx + x[1].shape[0] if x_split else x.shape[0]
    tm = _pick(math.gcd(m_ctx, seg_len), 256)
    nct = m_ctx // tm

    def seg(i):
        return jnp.maximum((i * tm - m_ctx) // seg_len + 1, 0)

    row = pl.BlockSpec((tm, d), lambda i: (i, 0))
    row_a = pl.BlockSpec((tm, d), lambda i: (jnp.minimum(i, nct - 1), 0))
    row_b = pl.BlockSpec((tm, d), lambda i: (jnp.maximum(i - nct, 0), 0))
    in_specs, args = ([row_a, row_b], list(x)) if x_split else ([row], [x])
    has_y, proj_bytes = False, 0
    if y is not None:
        k_post = 1 if k_gate == 2 else 3
        if isinstance(y, tuple) and len(y) == 2:
            has_y = "proj"
            a, w = y
            in_specs += [pl.BlockSpec((tm, a.shape[1]), lambda i: (i, 0)),
                         pl.BlockSpec((None, w.shape[1], d), lambda i: (li_post, 0, 0))]
            args += [a, w]
            proj_bytes = 2 * w.shape[1] * d * 2 + 2 * tm * a.shape[1] * 2
        elif isinstance(y, tuple):
            has_y = "pair"
            in_specs += [row, row, pl.BlockSpec((tm, y[2].shape[1]), lambda i: (i, 0))]
            args += list(y)
        else:
            has_y = True
            in_specs.append(row)
            args.append(y)
        in_specs += [pl.BlockSpec((None, None, 1, d), lambda i: (li_post, k_post, 0, 0)),
                     pl.BlockSpec((None, None, None, 1, d), lambda i: (li_post, seg(i), k_gate, 0, 0))]
        args += [norm_g, mods]
    if want_h:
        k_norm = 0 if k_mod == 0 else 2
        in_specs += [pl.BlockSpec((None, None, 1, d), lambda i: (li_pre, k_norm, 0, 0)),
                     pl.BlockSpec((None, None, None, 1, d), lambda i: (li_pre, seg(i), k_mod + 1, 0, 0)),
                     pl.BlockSpec((None, None, None, 1, d), lambda i: (li_pre, seg(i), k_mod, 0, 0))]
        args += [norm_g, mods, mods]
    if router is not None:
        in_specs.append(pl.BlockSpec(router.shape, lambda i: (0, 0)))
        args.append(router)
    out_specs, out_shape = [], []
    if want_x and out_split:
        out_specs += [row_a, row_b]
        out_shape += [jax.ShapeDtypeStruct((m_ctx, d), F32), jax.ShapeDtypeStruct((m - m_ctx, d), F32)]
    elif want_x:
        out_specs.append(row)
        out_shape.append(jax.ShapeDtypeStruct((m, d), F32))
    if want_h:
        out_specs.append(row)
        out_shape.append(jax.ShapeDtypeStruct((m, d), BF16))
        if router is not None:
            out_specs.append(pl.BlockSpec((tm, router.shape[1]), lambda i: (i, 0)))
            out_shape.append(jax.ShapeDtypeStruct((m, router.shape[1]), F32))
    body = functools.partial(_resid_norm_body, n_ctx_tiles=nct, x_split=x_split, has_y=has_y,
                             want_x=want_x, out_split=out_split, want_h=want_h,
                             has_router=router is not None)
    return pl.pallas_call(
        body, grid=(m // tm,), in_specs=in_specs, out_specs=out_specs, out_shape=out_shape,
        compiler_params=_cparams(("arbitrary",), 14 * tm * d * 4 + d * LANES * 8 + proj_bytes),
        name="resid_norm",
    )(*args)


def _matmul_body(*refs, tn, scale, act, rope_shift, rope_pattern, seq_split, n_heads):
    x_ref, w_ref = refs[0], refs[1]
    o_ref = refs[-1]
    acc = jnp.dot(x_ref[...], w_ref[...], preferred_element_type=F32)
    if scale is not None:
        acc = acc * scale
    if act == "sigmoid":
        acc = jax.nn.sigmoid(acc)
    if seq_split is not None:
        refs[-2][...] = acc.astype(BF16)
        dh = tn // n_heads
        for b in range(acc.shape[0] // seq_split):
            for h in range(n_heads):
                o_ref[b, :, h, :] = acc[b * seq_split:(b + 1) * seq_split, h * dh:(h + 1) * dh]
        return
    if rope_shift is None:
        o_ref[...] = acc.astype(o_ref.dtype)
        return
    cos, sa, sb = refs[2][...], refs[3][...], refs[4][...]
    for g in range(tn // LANES):
        a = acc[:, g * LANES:(g + 1) * LANES]
        if rope_pattern[g % len(rope_pattern)]:
            a = _rope128(a, cos, sa, sb, rope_shift)
        o_ref[:, g * LANES:(g + 1) * LANES] = a.astype(o_ref.dtype)


def _matmul(x, w, *, rows, row0, col0, ncols, wsel, out_dtype, tm, tn, scale=None, act=None,
            rope=None, state=None, name="matmul"):
    k = x.shape[1]
    assert row0 % tm == 0 and rows % tm == 0 and col0 % tn == 0 and ncols % tn == 0, (row0, rows, col0, ncols, tm, tn)
    rb0, cb0 = row0 // tm, col0 // tn
    in_specs = [pl.BlockSpec((tm, k), lambda j, i: (rb0 + i, 0)),
                pl.BlockSpec((None, k, tn), lambda j, i: (wsel, 0, cb0 + j))]
    args = [x, w]
    rope_shift = rope_pattern = None
    if rope is not None:
        tabs, rope_shift, rope_pattern = rope
        nper = tabs[0].shape[0] // tm
        assert tabs[0].shape[0] % tm == 0
        for t in tabs:
            in_specs.append(pl.BlockSpec((tm, LANES), lambda j, i: (i % nper, 0)))
            args.append(t)
    aliases, seq_split, n_heads = {}, None, None
    out_spec = pl.BlockSpec((tm, tn), lambda j, i: (i, j))
    out_shape = jax.ShapeDtypeStruct((rows, ncols), out_dtype)
    if state is not None:
        prev, li, depth, seq_split, n_heads = state
        assert tm % seq_split == 0 and rope is None and tn == ncols and out_dtype == BF16
        dh = ncols // n_heads
        out_spec = [out_spec, pl.BlockSpec((tm // seq_split, None, seq_split, n_heads, dh),
                                           lambda j, i: (i, li, 0, 0, 0))]
        out_shape = [out_shape, jax.ShapeDtypeStruct((rows // seq_split, depth, seq_split, n_heads, dh), F32)]
        if prev is not None:
            in_specs.append(pl.BlockSpec(memory_space=pl.ANY))
            aliases = {len(args): 1}
            args.append(prev)
    osz = jnp.dtype(out_dtype).itemsize
    est = 2 * (tm * k * 2 + k * tn * 2 + tm * tn * osz) + 3 * tm * tn * 4 + 6 * tm * LANES * 4
    body = functools.partial(_matmul_body, tn=tn, scale=scale, act=act, rope_shift=rope_shift,
                             rope_pattern=rope_pattern, seq_split=seq_split, n_heads=n_heads)
    return pl.pallas_call(
        body, grid=(ncols // tn, rows // tm), in_specs=in_specs, out_specs=out_spec, out_shape=out_shape,
        input_output_aliases=aliases,
        compiler_params=_cparams(("arbitrary", "arbitrary"), est),
        name=name,
    )(*args)


def _mla_prep_body(zm_ref, kr_ref, qg_ref, kg_ref, cos_ref, sa_ref, sb_ref, *rest, q_lora, kv_lora,
                   n_ctx_tiles, seq):
    cq_o, ckv_o, ckr_o = rest[-3:]
    z = zm_ref[...]
    cq_o[...] = (_rms(z[:, :q_lora]) * qg_ref[...]).astype(BF16)
    ckv = _rms(z[:, q_lora:]) * kg_ref[...]
    ckr_o[:, :kv_lora] = ckv.astype(BF16)
    i = pl.program_id(0)
    lane = lax.broadcasted_iota(jnp.int32, kr_ref.shape, 1)
    kr = jnp.where(lane < ROPE_B, kr_ref[...], 0.0)

    @pl.when(i < n_ctx_tiles)
    def _():
        ckv_o[...] = ckv.reshape(ckv.shape[0] // seq, seq, kv_lora)
        ckr_o[:, kv_lora:] = kr.astype(BF16)

    @pl.when(i >= n_ctx_tiles)
    def _():
        ckr_o[:, kv_lora:] = _rope128(kr, cos_ref[...], sa_ref[...], sb_ref[...], ROPE_B // 4).astype(BF16)


def _mla_prep(zm, kr, qg, kg, tabs, state, *, q_lora, kv_lora, m_ctx, seg_len):
    prev, li, depth, seq = state[:4]
    m = zm.shape[0]
    tm = _pick(math.gcd(m_ctx, seg_len), 512)
    assert tm % seq == 0
    nct, nper = m_ctx // tm, seg_len // tm
    tab_spec = pl.BlockSpec((tm, LANES), lambda i: (jnp.maximum(i - nct, 0) % nper, 0))
    in_specs = [pl.BlockSpec((tm, q_lora + kv_lora), lambda i: (i, 0)),
                pl.BlockSpec((tm, LANES), lambda i: (i, 0)),
                pl.BlockSpec((1, q_lora), lambda i: (0, 0)),
                pl.BlockSpec((1, kv_lora), lambda i: (0, 0)),
                tab_spec, tab_spec, tab_spec]
    args = [zm, kr, qg, kg, *tabs]
    aliases = {}
    if prev is not None:
        in_specs.append(pl.BlockSpec(memory_space=pl.ANY))
        aliases = {len(args): 1}
        args.append(prev)
    body = functools.partial(_mla_prep_body, q_lora=q_lora, kv_lora=kv_lora, n_ctx_tiles=nct, seq=seq)
    return pl.pallas_call(
        body, grid=(m // tm,), in_specs=in_specs,
        out_specs=[pl.BlockSpec((tm, q_lora), lambda i: (i, 0)),
                   pl.BlockSpec((tm // seq, None, seq, kv_lora), lambda i: (jnp.minimum(i, nct - 1), li, 0, 0)),
                   pl.BlockSpec((tm, kv_lora + LANES), lambda i: (i, 0))],
        out_shape=[jax.ShapeDtypeStruct((m, q_lora), BF16),
                   jax.ShapeDtypeStruct((m_ctx // seq, depth, seq, kv_lora), F32),
                   jax.ShapeDtypeStruct((m, kv_lora + LANES), BF16)],
        input_output_aliases=aliases,
        compiler_params=_cparams(("arbitrary",), 8 * tm * (q_lora + kv_lora + 4 * LANES) * 4),
        name="mla_prep",
    )(*args)


_NT = (((1,), (1,)), ((), ()))


def _softmax_parts(q, k):
    s = lax.dot_general(q, k, _NT, preferred_element_type=F32)
    p = jnp.exp2(s - jnp.max(s, axis=-1, keepdims=True))
    return p, 1.0 / jnp.sum(p, axis=-1, keepdims=True)


def _one_head(q, k, v, *, diff, lam, g, out_scale):
    k = k.astype(BF16)
    v = v.astype(BF16)
    if not diff:
        p, r = _softmax_parts(q, k)
        return jnp.dot(p.astype(BF16), v, preferred_element_type=F32) * r
    p1, r1 = _softmax_parts(q[:, :HD_A], k[:, :HD_A])
    p2, r2 = _softmax_parts(q[:, HD_A:], k[:, HD_A:])
    p = (p1 * r1 - p2 * (lam * r2)).astype(BF16)
    o = jnp.dot(p, v, preferred_element_type=F32)
    return _rms(o) * g * out_scale


def _ctx_attn_body(*refs, nb, nh, dq, dv, diff, out_scale):
    it = iter(refs)
    lam = next(it)[0] if diff else None
    q_ref, k_ref, v_ref = next(it), next(it), next(it)
    g = next(it)[...] if diff else None
    o_ref = refs[-1]
    s = q_ref.shape[1]
    for b in range(nb):
        for h in range(nh):
            o = _one_head(q_ref[b, :, h * dq:(h + 1) * dq], k_ref[b, :, h * dq:(h + 1) * dq],
                          v_ref[b, :, h * dv:(h + 1) * dv], diff=diff, lam=lam, g=g, out_scale=out_scale)
            o_ref[b * s:(b + 1) * s, h * dv:(h + 1) * dv] = o.astype(o_ref.dtype)


def _ctx_attn(q, k, v, *, li, nh, dq, dv, out_rows, diff=False, lam=None, g=None, out_scale=None, name):
    bsz, s, _ = q.shape
    nb = 2 if bsz % 2 == 0 else 1

    def kv_spec(a, width):
        if a.ndim == 4:
            return pl.BlockSpec((nb, None, s, width), lambda i: (i, li, 0, 0))
        return pl.BlockSpec((nb, s, width), lambda i: (i, 0, 0))

    in_specs, args = [], []
    if diff:
        in_specs.append(pl.BlockSpec(memory_space=pltpu.SMEM))
        args.append(lam)
    in_specs += [pl.BlockSpec((nb, s, nh * dq), lambda i: (i, 0, 0)), kv_spec(k, nh * dq), kv_spec(v, nh * dv)]
    args += [q, k, v]
    if diff:
        in_specs.append(pl.BlockSpec((1, dv), lambda i: (0, 0)))
        args.append(g)
    body = functools.partial(_ctx_attn_body, nb=nb, nh=nh, dq=dq, dv=dv, diff=diff, out_scale=out_scale)
    return pl.pallas_call(
        body, grid=(bsz // nb,), in_specs=in_specs,
        out_specs=pl.BlockSpec((nb * s, nh * dv), lambda i: (i, 0)),
        out_shape=jax.ShapeDtypeStruct((out_rows, nh * dv), BF16),
        compiler_params=_cparams(("arbitrary",), 4 * nb * s * nh * (2 * dq + dv) * 4 + 16 * s * s * 4 * nb * nh),
        name=name,
    )(*args)


def _softmax_parts_t(k, q):
    st = lax.dot_general(k, q, _NT, preferred_element_type=F32)
    p = jnp.exp2(st - jnp.max(st, axis=0, keepdims=True))
    return p.astype(BF16), 1.0 / jnp.sum(p, axis=0, keepdims=True)


def _lat_attn_t_body(*refs, hps, dq, dv, diff, out_scale):
    it = iter(refs)
    lam = next(it)[0] if diff else None
    q_ref, k_ref, vt_ref = next(it), next(it), next(it)
    g = next(it)[...] if diff else None
    o_ref = refs[-1]
    if diff:
        q, k, vt = q_ref[...], k_ref[...], vt_ref[...]
        p1, r1 = _softmax_parts_t(k[:, :HD_A], q[:, :HD_A])
        p2, r2 = _softmax_parts_t(k[:, HD_A:], q[:, HD_A:])
        ot = (jnp.dot(vt, p1, preferred_element_type=F32) * r1
              - jnp.dot(vt, p2, preferred_element_type=F32) * (lam * r2))
        o_ref[...] = (_rms(ot.T) * g * out_scale).astype(o_ref.dtype)
        return
    ps = [_softmax_parts_t(k_ref[:, h * dq:(h + 1) * dq], q_ref[:, h * dq:(h + 1) * dq]) for h in range(hps)]
    for h, (p, r) in enumerate(ps):
        ot = jnp.dot(vt_ref[h * dv:(h + 1) * dv, :], p, preferred_element_type=F32) * r
        o_ref[:, h * dv:(h + 1) * dv] = ot.T.astype(o_ref.dtype)


def _lat_attn_body(*refs, hps, dq, dv, diff, out_scale):
    it = iter(refs)
    lam = next(it)[0] if diff else None
    q_ref, k_ref, v_ref = next(it), next(it), next(it)
    g = next(it)[...] if diff else None
    o_ref = refs[-1]
    for h in range(hps):
        o = _one_head(q_ref[:, h * dq:(h + 1) * dq], k_ref[:, h * dq:(h + 1) * dq],
                      v_ref[:, h * dv:(h + 1) * dv], diff=diff, lam=lam, g=g, out_scale=out_scale)
        o_ref[:, h * dv:(h + 1) * dv] = o.astype(o_ref.dtype)


def _lat_attn(q, k, v, *, nh, hps, dq, dv, tq, diff=False, lam=None, g=None,
              out_scale=None, v_transposed=False, name):
    bsz, n, _ = q.shape
    nk = k.shape[1]
    assert nh % hps == 0 and n % tq == 0
    in_specs, args = [], []
    if diff:
        in_specs.append(pl.BlockSpec(memory_space=pltpu.SMEM))
        args.append(lam)
    v_spec = (pl.BlockSpec((None, hps * dv, nk), lambda b, h, i: (b, h, 0)) if v_transposed
              else pl.BlockSpec((None, nk, hps * dv), lambda b, h, i: (b, 0, h)))
    in_specs += [pl.BlockSpec((None, tq, hps * dq), lambda b, h, i: (b, i, h)),
                 pl.BlockSpec((None, nk, hps * dq), lambda b, h, i: (b, 0, h)), v_spec]
    args += [q, k, v]
    if diff:
        in_specs.append(pl.BlockSpec((1, dv), lambda b, h, i: (0, 0)))
        args.append(g)
    nqb = n // tq
    n_chain = hps * (2 if diff else 1)
    est = 2 * (tq * hps * dq + nk * hps * (dq + dv)) * 2 + 5 * n_chain * tq * nk * 4
    if v_transposed:
        assert hps == 1 or not diff
        body = functools.partial(_lat_attn_t_body, hps=hps, dq=dq, dv=dv, diff=diff, out_scale=out_scale)
    else:
        body = functools.partial(_lat_attn_body, hps=hps, dq=dq, dv=dv, diff=diff, out_scale=out_scale)
    return pl.pallas_call(
        body, grid=(bsz, nh // hps, nqb), in_specs=in_specs,
        out_specs=pl.BlockSpec((tq, hps * dv), lambda b, h, i: (b * nqb + i, h)),
        out_shape=jax.ShapeDtypeStruct((bsz * n, nh * dv), BF16),
        compiler_params=_cparams(("arbitrary", "arbitrary", "arbitrary"), est),
        name=name,
    )(*args)


def _na_geometry(rows):
    rpb = NA_ROWS_PER_BLOCK
    kr = min(NA_KR_MAX, rows)
    span = rpb + kr - 1
    assert rows % rpb == 0 and rows >= span, rows
    nblk = rows // rpb
    r = np.arange(rows)
    row_start = np.clip(r - kr // 2, 0, rows - kr)
    blk_start = np.clip(rpb * np.arange(nblk) - kr // 2, 0, rows - span)
    cols = np.arange(GRID_W)
    col_start = np.clip(cols - NA_KC // 2, 0, GRID_W - NA_KC)
    dc = cols[None, :] - cols[:, None] + (NA_KC - 1)
    ok_c = (cols[None, :] >= col_start[:, None]) & (cols[None, :] < col_start[:, None] + NA_KC)
    sigs, cls_of_blk, dr_l, ok_l = {}, [], [], []
    for b in range(nblk):
        qr = rpb * b + np.arange(rpb)
        key_row = blk_start[b] + np.arange(span)
        dr = key_row[None, :] - qr[:, None] + (NA_KR_MAX - 1)
        ok_r = (key_row[None, :] >= row_start[qr][:, None]) & (key_row[None, :] < row_start[qr][:, None] + kr)
        sig = (dr.tobytes(), ok_r.tobytes())
        if sig not in sigs:
            sigs[sig] = len(sigs)
            dr_l.append(np.clip(dr, 0, 2 * NA_KR_MAX - 2))
            ok = ok_r[:, None, :, None] & ok_c[None, :, None, :]
            ok_l.append(ok.reshape(rpb * GRID_W, span * GRID_W))
        cls_of_blk.append(sigs[sig])
    return dict(span=span, nblk=nblk, blk_start=blk_start.astype(np.int32), cls=np.asarray(cls_of_blk, np.int32),
                dr=np.stack(dr_l), dc=np.clip(dc, 0, 2 * NA_KC - 2), ok=np.stack(ok_l))


def _na_bias(tables, na):
    depth, nh, n_dr, n_dc = tables.shape
    ncls, rpb, span = na["dr"].shape
    oh_r = jax.nn.one_hot(na["dr"].reshape(-1), n_dr, dtype=F32)
    oh_c = jax.nn.one_hot(na["dc"].reshape(-1), n_dc, dtype=F32).T
    t = jnp.einsum("xr,lhrc->lhxc", oh_r, tables.astype(F32), precision=HIGHEST)
    t = jnp.einsum("lhxc,cy->lhxy", t, oh_c, precision=HIGHEST)
    t = t.reshape(depth, nh, ncls, rpb, span, GRID_W, GRID_W).transpose(0, 1, 2, 3, 5, 4, 6)
    t = t.reshape(depth, nh, ncls, rpb * GRID_W, span * GRID_W)
    return jnp.where(na["ok"][None, None], t * LOG2E, MASK_VALUE)


def _na_body(cls_ref, start_ref, q_ref, k_ref, v_ref, kc_ref, vc_ref, bias_ref, o_ref, *, span_tok, hps):
    del cls_ref
    blk = pl.program_id(2)
    start = pl.multiple_of(start_ref[blk] * GRID_W, GRID_W)
    scores = []
    for h in range(hps):
        c = slice(h * HD_C, (h + 1) * HD_C)
        q = q_ref[:, c]
        s_loc = lax.dot_general(q, k_ref[pl.ds(start, span_tok), c], _NT, preferred_element_type=F32) + bias_ref[h]
        s_ctx = lax.dot_general(q, kc_ref[:, c].astype(BF16), _NT, preferred_element_type=F32)
        scores.append((s_loc, s_ctx))
    for h, (s_loc, s_ctx) in enumerate(scores):
        c = slice(h * HD_C, (h + 1) * HD_C)
        mx = jnp.maximum(jnp.max(s_loc, axis=-1, keepdims=True), jnp.max(s_ctx, axis=-1, keepdims=True))
        p_loc = jnp.exp2(s_loc - mx)
        p_ctx = jnp.exp2(s_ctx - mx)
        r = 1.0 / (jnp.sum(p_loc, axis=-1, keepdims=True) + jnp.sum(p_ctx, axis=-1, keepdims=True))
        o = jnp.dot(p_loc.astype(BF16), v_ref[pl.ds(start, span_tok), c], preferred_element_type=F32)
        o = o + jnp.dot(p_ctx.astype(BF16), vc_ref[:, c].astype(BF16), preferred_element_type=F32)
        o_ref[:, c] = (o * r).astype(o_ref.dtype)


def _na_lat_attn(q, kv, cache_k, cache_v, bias, na, *, li):
    bsz, n, _ = q.shape
    past = cache_k.shape[2]
    rpb_tok, span_tok, nblk = NA_ROWS_PER_BLOCK * GRID_W, na["span"] * GRID_W, na["nblk"]
    hps = NA_HEADS_PER_STEP
    w = hps * HD_C
    ng = H_C // hps
    gs = pltpu.PrefetchScalarGridSpec(
        num_scalar_prefetch=2, grid=(bsz, ng, nblk),
        in_specs=[pl.BlockSpec((None, rpb_tok, w), lambda b, h, i, cl, st: (b, i, h)),
                  pl.BlockSpec((None, n, w), lambda b, h, i, cl, st: (b, 0, h)),
                  pl.BlockSpec((None, n, w), lambda b, h, i, cl, st: (b, 0, ng + h)),
                  pl.BlockSpec((None, None, past, w), lambda b, h, i, cl, st: (b, li, 0, h)),
                  pl.BlockSpec((None, None, past, w), lambda b, h, i, cl, st: (b, li, 0, h)),
                  pl.BlockSpec((None, hps, None, rpb_tok, span_tok), lambda b, h, i, cl, st: (li, h, cl[i], 0, 0))],
        out_specs=pl.BlockSpec((rpb_tok, w), lambda b, h, i, cl, st: (b * nblk + i, h)))
    return pl.pallas_call(
        functools.partial(_na_body, span_tok=span_tok, hps=hps),
        grid_spec=gs, out_shape=jax.ShapeDtypeStruct((bsz * n, H_C * HD_C), BF16),
        compiler_params=_cparams(("arbitrary", "arbitrary", "arbitrary"),
                                 8 * n * w * 2 + hps * 12 * rpb_tok * span_tok * 4),
        name="na_attn_lat",
    )(jnp.asarray(na["cls"]), jnp.asarray(na["blk_start"]), q, kv, kv, cache_k, cache_v, bias)


def _merge_body(*refs, n_ctx_tiles):
    br, (ga, gb, gc), (wa, wb, wc), o_ref = refs[:6], refs[6:9], refs[9:12], refs[12]
    is_ctx = pl.program_id(1) < n_ctx_tiles
    acc = None
    for s, (g_ref, w_ref) in enumerate(((ga, wa), (gb, wb), (gc, wc))):
        a = jnp.where(is_ctx, br[2 * s][...], br[2 * s + 1][...])
        t = g_ref[...].astype(F32) * jnp.dot(a, w_ref[...], preferred_element_type=F32)
        acc = t if acc is None else acc + t
    o_ref[...] = acc.astype(o_ref.dtype)


def _merge(branches, gates, wb, li):
    m_ctx, kb = branches[0][0].shape
    m = m_ctx + branches[0][1].shape[0]
    d = wb.shape[-1]
    tm, tn = _pick(math.gcd(m_ctx, m - m_ctx), 512), _pick(d, 1024, 256)
    nj, nct = d // tn, m_ctx // tm
    br_ctx = pl.BlockSpec((tm, kb), lambda j, i: (jnp.minimum(i, nct - 1), 0))
    br_lat = pl.BlockSpec((tm, kb), lambda j, i: (jnp.maximum(i - nct, 0), 0))
    gspec = [pl.BlockSpec((tm, tn), functools.partial(lambda j, i, s: (i, s * nj + j), s=s)) for s in range(N_BRANCH)]
    wspec = [pl.BlockSpec((None, None, kb, tn), functools.partial(lambda j, i, s: (li, s, 0, j), s=s)) for s in range(N_BRANCH)]
    est = 2 * (6 * tm * kb * 2 + 3 * tm * tn * 2 + 3 * kb * tn * 2 + tm * tn * 2) + 4 * tm * tn * 4
    return pl.pallas_call(
        functools.partial(_merge_body, n_ctx_tiles=nct), grid=(nj, m // tm),
        in_specs=[br_ctx, br_lat] * N_BRANCH + gspec + wspec,
        out_specs=pl.BlockSpec((tm, tn), lambda j, i: (i, j)),
        out_shape=jax.ShapeDtypeStruct((m, d), BF16),
        compiler_params=_cparams(("arbitrary", "arbitrary"), est),
        name="branch_merge",
    )(*[a for pair in branches for a in pair], gates, gates, gates, wb, wb, wb)


def _ffn1_body(te_ref, ts_ref, nu_ref, x_ref, w13_ref, o_ref):
    del te_ref, ts_ref

    @pl.when(pl.program_id(1) < nu_ref[0])
    def _():
        tf = o_ref.shape[1]
        h = jnp.dot(x_ref[...], w13_ref[...], preferred_element_type=F32)
        h1, h3 = h[:, :tf], h[:, tf:]
        o_ref[...] = (h1 * jax.nn.sigmoid(h1) * h3).astype(o_ref.dtype)

    @pl.when(pl.program_id(1) >= nu_ref[0])
    def _():
        o_ref[...] = jnp.zeros_like(o_ref)


def _ffn2_body(te_ref, ts_ref, nu_ref, h_ref, w2_ref, o_ref):
    del te_ref, ts_ref

    @pl.when(pl.program_id(1) < nu_ref[0])
    def _():
        o_ref[...] = jnp.dot(h_ref[...], w2_ref[...], preferred_element_type=F32).astype(o_ref.dtype)

    @pl.when(pl.program_id(1) >= nu_ref[0])
    def _():
        o_ref[...] = jnp.zeros_like(o_ref)


def _ffn_tile(f):
    return _pick(f, 1408, LANES)


def _cast_interleave_body(w1_ref, w3_ref, o_ref):
    tf = w1_ref.shape[1]
    o_ref[:, :tf] = w1_ref[...].astype(BF16)
    o_ref[:, tf:] = w3_ref[...].astype(BF16)


def _cast_interleave(w1, w3):
    nl, ne, d, f = w1.shape
    tf = _ffn_tile(f)
    tk = _pick(d, 1024)
    blk = pl.BlockSpec((None, None, tk, tf), lambda l, e, k, j: (l, e, k, j))
    return pl.pallas_call(
        _cast_interleave_body, grid=(nl, ne, d // tk, f // tf), in_specs=[blk, blk],
        out_specs=pl.BlockSpec((None, None, tk, 2 * tf), lambda l, e, k, j: (l, e, k, j)),
        out_shape=jax.ShapeDtypeStruct((nl, ne, d, 2 * f), BF16),
        compiler_params=_cparams(("arbitrary",) * 4, 2 * (2 * tk * tf * 4 + tk * 2 * tf * 2) + 2 * tk * tf * 4),
        name="cast_interleave",
    )(w1, w3)


def _grouped_ffn(xs, w13, w2, lsel, tile_expert, tile_src, n_used, tm, out_dtype=F32):
    r, d = xs.shape
    f = w2.shape[2]
    nt = r // tm
    tf = _ffn_tile(f)
    tn = _pick(d, 1024, 256)
    gs1 = pltpu.PrefetchScalarGridSpec(
        num_scalar_prefetch=3, grid=(f // tf, nt),
        in_specs=[pl.BlockSpec((tm, d), lambda j, t, te, ts, nu: (ts[t], 0)),
                  pl.BlockSpec((None, None, d, 2 * tf), lambda j, t, te, ts, nu: (lsel, te[t], 0, j))],
        out_specs=pl.BlockSpec((tm, tf), lambda j, t, te, ts, nu: (ts[t], j)))
    hid = pl.pallas_call(
        _ffn1_body, grid_spec=gs1, out_shape=jax.ShapeDtypeStruct((r, f), BF16),
        compiler_params=_cparams(("arbitrary", "arbitrary"),
                                 2 * (tm * d * 2 + 2 * d * tf * 2 + tm * tf * 2) + 4 * tm * tf * 4),
        name="ffn_up",
    )(tile_expert, tile_src, n_used, xs, w13)
    gs2 = pltpu.PrefetchScalarGridSpec(
        num_scalar_prefetch=3, grid=(d // tn, nt),
        in_specs=[pl.BlockSpec((tm, f), lambda j, t, te, ts, nu: (ts[t], 0)),
                  pl.BlockSpec((None, None, f, tn), lambda j, t, te, ts, nu: (lsel, te[t], 0, j))],
        out_specs=pl.BlockSpec((tm, tn), lambda j, t, te, ts, nu: (ts[t], j)))
    return pl.pallas_call(
        _ffn2_body, grid_spec=gs2, out_shape=jax.ShapeDtypeStruct((r, d), out_dtype),
        compiler_params=_cparams(("arbitrary", "arbitrary"),
                                 2 * (tm * f * 2 + f * tn * 2 + tm * tn * 4) + 2 * tm * tn * 4),
        name="ffn_down",
    )(tile_expert, tile_src, n_used, hid, w2)


def _moe_plan(logits, tm):
    m = logits.shape[0]
    top_v, top_i = lax.top_k(logits, TOP_K)
    top_w = jax.nn.softmax(top_v, axis=-1)
    flat_e = top_i.reshape(-1)
    onehot = (flat_e[:, None] == jnp.arange(N_EXPERTS)[None, :]).astype(jnp.int32)
    csum = jnp.cumsum(onehot, axis=0)
    counts = csum[-1]
    rank = jnp.take_along_axis(csum, flat_e[:, None], axis=1)[:, 0] - 1
    padded = ((counts + tm - 1) // tm) * tm
    ends = jnp.cumsum(padded)
    offs = ends - padded
    dest = offs[flat_e] + rank
    n_rows = TOP_K * m + N_EXPERTS * tm
    nt = n_rows // tm
    row_token = jnp.zeros((n_rows,), jnp.int32).at[dest].set(jnp.arange(TOP_K * m, dtype=jnp.int32) // TOP_K)
    n_used = (ends[-1] // tm).astype(jnp.int32)
    tile_src = jnp.arange(nt, dtype=jnp.int32)
    tile_expert = jnp.sum((jnp.minimum(tile_src, n_used - 1)[:, None] * tm >= ends[None, :]).astype(jnp.int32), axis=1)
    tile_expert = jnp.minimum(tile_expert, N_EXPERTS - 1).astype(jnp.int32)
    return row_token, dest.reshape(m, TOP_K), top_w, tile_expert, tile_src, n_used.reshape(1)


def _rows(a, idx):
    return a.at[idx].get(mode="promise_in_bounds")


def _rope_tables(n_tok, dim):
    half = dim // 2
    t = jnp.arange(n_tok)
    row = (t // GRID_W).astype(F32)
    col = (t % GRID_W).astype(F32)
    inv = ROPE_THETA ** (-jnp.arange(0, half, 2, dtype=F32) / half)
    ar = row[:, None] * inv[None, :]
    ac = col[:, None] * inv[None, :]
    ang = jnp.concatenate([ar, ar, ac, ac], axis=-1)
    cos, sin = jnp.cos(ang), jnp.sin(ang)
    first = (np.arange(dim) % half) < (dim // 4)
    sa = jnp.where(first[None, :], -sin, 0.0)
    sb = jnp.where(first[None, :], 0.0, sin)
    pad = LANES - dim
    if pad:
        cos = jnp.pad(cos, ((0, 0), (0, pad)), constant_values=1.0)
        sa = jnp.pad(sa, ((0, 0), (0, pad)))
        sb = jnp.pad(sb, ((0, 0), (0, pad)))
    return cos, sa, sb


def kernel(x_prompt, x_sample, cache_diff_k, cache_diff_v, cache_mla_ckv, cache_mla_krope, cache_na_k, cache_na_v, c, c_ctx, norm_g, ada_w, ada_b, w_in, diff_lambda, diff_subln_g, mla_q_norm_g, mla_kv_norm_g, mla_w_uq, mla_w_ukv, na_rel_bias, w_branch, w_out, ffn_w1, ffn_w3, ffn_w2, moe_router, moe_w1, moe_w3, moe_w2):
    batch, seq, d = x_prompt.shape
    dec_batch, dec_seq, _ = x_sample.shape
    depth = norm_g.shape[0]
    past = cache_diff_k.shape[2]
    q_lora, kv_lora = mla_q_norm_g.shape[1], mla_kv_norm_g.shape[1]
    m_ctx, m_lat = batch * seq, dec_batch * dec_seq
    m = m_ctx + m_lat
    nk_lat = dec_seq + past
    assert dec_seq % GRID_W == 0 and kv_lora % LANES == 0 and q_lora % LANES == 0

    a3 = 3 * BR_W
    col_m = a3
    col_kr = a3 + q_lora + kv_lora
    c_na = col_kr + ROPE_B
    n_rest = 3 * BR_W + N_BRANCH * d
    w_head = _cast_head(w_in, col_kr + LANES)
    w_rest = _cast_shift(w_in, c_na, n_rest, 768)
    col_c, col_g = 0, 3 * BR_W
    wq = mla_w_uq.reshape(depth, q_lora, H_B, NOPE_B + ROPE_B)
    w_uqp = jnp.pad(wq, ((0, 0), (0, 0), (0, 0), (0, 2 * LANES - NOPE_B - ROPE_B))).reshape(depth, q_lora, H_B * 2 * LANES).astype(BF16)
    wkv = mla_w_ukv.reshape(depth, kv_lora, H_B, NOPE_B + V_B)
    wk_top = jnp.pad(wkv[..., :NOPE_B], ((0, 0), (0, 0), (0, 0), (0, 2 * LANES - NOPE_B))).reshape(depth, kv_lora, H_B * 2 * LANES)
    eye = np.zeros((LANES, H_B, 2 * LANES), np.float32)
    for r_ in range(ROPE_B):
        eye[r_, :, NOPE_B + r_] = 1.0
    wk_aug = jnp.concatenate([wk_top, jnp.broadcast_to(jnp.asarray(eye.reshape(LANES, -1)), (depth, LANES, H_B * 2 * LANES))], axis=1).astype(BF16)
    wv_aug = jnp.pad(wkv[..., NOPE_B:].reshape(depth, kv_lora, H_B * V_B), ((0, 0), (0, LANES), (0, 0))).astype(BF16)
    w_branch_b = w_branch.astype(BF16)
    w_out_b = w_out.astype(BF16)
    ffn_w13b, ffn_w2b = _cast_interleave(ffn_w1[:, None], ffn_w3[:, None]), ffn_w2.astype(BF16)[:, None]
    if depth > 1:
        moe_w13b, moe_w2b = _cast_interleave(moe_w1, moe_w3), moe_w2.astype(BF16)
    router_p = jnp.pad(moe_router, ((0, 0), (0, 0), (0, LANES - N_EXPERTS)))
    norm_g4 = norm_g.reshape(depth, 4, 1, d)

    n_seg = 1 + dec_batch
    r_pad = -(-n_seg // 8) * 8
    cvec = jnp.zeros((r_pad, d), F32).at[0].set(c_ctx).at[1:n_seg].set(c)
    mods = _ada_mod(cvec, ada_w, ada_b).reshape(depth, r_pad, 6, 1, d)

    tabs_a = _rope_tables(dec_seq, HD_A)
    tabs_b = _rope_tables(dec_seq, ROPE_B)
    na = _na_geometry(dec_seq // GRID_W)
    na_bias = _na_bias(na_rel_bias, na)
    lam_inits = [0.8 - 0.6 * math.exp(-0.3 * li) for li in range(depth)]
    lp = diff_lambda.astype(F32)
    lams = jnp.exp(jnp.sum(lp[:, 0] * lp[:, 1], axis=-1)) - jnp.exp(jnp.sum(lp[:, 2] * lp[:, 3], axis=-1)) + jnp.asarray(lam_inits, F32)

    tm_big = _pick(math.gcd(m_ctx, dec_seq), 1024)
    assert tm_big % seq == 0
    tq_lat = _pick(dec_seq, 512)
    rn = functools.partial(_resid_norm, m_ctx=m_ctx, seg_len=dec_seq)
    mm = functools.partial(_matmul, tm=tm_big)
    qs_a, qs_b, qs_c = HD_A ** -0.5 * LOG2E, (NOPE_B + ROPE_B) ** -0.5 * LOG2E, HD_C ** -0.5 * LOG2E
    hb2 = H_B * 2 * LANES
    cache_k_c = cache_na_k.reshape(dec_batch, depth, past, H_C * HD_C)
    cache_v_c = cache_na_v.reshape(dec_batch, depth, past, H_C * HD_C)

    x = (x_prompt.reshape(m_ctx, d), x_sample.reshape(m_lat, d))
    (hmix,) = rn(x, None, mods, norm_g4, li_post=0, k_gate=2, li_pre=0, k_mod=0, want_x=False, want_h=True)
    st_dk = jnp.zeros((batch, depth, seq, H_A, 2 * HD_A), F32)
    st_dv = jnp.zeros((batch, depth, seq, H_A, 2 * HD_A), F32)
    st_nk = jnp.zeros((batch, depth, seq, H_C, HD_C), F32)
    st_nv = jnp.zeros((batch, depth, seq, H_C, HD_C), F32)
    st_ckv = jnp.zeros((batch, depth, seq, kv_lora), F32)
    st_kr = []

    for li in range(depth):
        def st(prev, n_heads=None):
            return (prev, li, depth, seq, n_heads)

        def b3(a):
            return a.reshape(batch, seq, a.shape[-1])

        ctx = dict(rows=m_ctx, row0=0, wsel=li, tn=1024)
        lat = dict(rows=m_lat, row0=m_ctx, wsel=li, tn=1024)
        qa_ctx = mm(hmix, w_head, col0=0, ncols=BR_W, out_dtype=BF16, scale=qs_a, name="in_qa_ctx", **ctx)
        ka_ctx, st_dk = mm(hmix, w_head, col0=BR_W, ncols=BR_W, out_dtype=BF16, state=st(st_dk, H_A), name="in_ka_ctx", **ctx)
        va_ctx, st_dv = mm(hmix, w_head, col0=2 * BR_W, ncols=BR_W, out_dtype=BF16, state=st(st_dv, H_A), name="in_va_ctx", **ctx)
        qa_lat = mm(hmix, w_head, col0=0, ncols=BR_W, out_dtype=BF16, scale=qs_a,
                    rope=(tabs_a, HD_A // 4, (True,)), name="in_qa_lat", **lat)
        ka_lat = mm(hmix, w_head, col0=BR_W, ncols=BR_W, out_dtype=BF16,
                    rope=(tabs_a, HD_A // 4, (True,)), name="in_ka_lat", **lat)
        va_lat = mm(hmix, w_head, col0=2 * BR_W, ncols=BR_W, out_dtype=BF16, name="in_va_lat", **lat)
        qc_ctx = mm(hmix, w_rest, col0=col_c, ncols=BR_W, out_dtype=BF16, scale=qs_c, name="in_qc_ctx", **ctx)
        kc_ctx, st_nk = mm(hmix, w_rest, col0=col_c + BR_W, ncols=BR_W, out_dtype=BF16, state=st(st_nk, H_C), name="in_kc_ctx", **ctx)
        vc_ctx, st_nv = mm(hmix, w_rest, col0=col_c + 2 * BR_W, ncols=BR_W, out_dtype=BF16, state=st(st_nv, H_C), name="in_vc_ctx", **ctx)
        qc_lat = mm(hmix, w_rest, col0=col_c, ncols=BR_W, out_dtype=BF16, scale=qs_c, name="in_qc_lat", **lat)
        kvc_lat = mm(hmix, w_rest, col0=col_c + BR_W, ncols=2 * BR_W, out_dtype=BF16, name="in_kvc_lat", **lat)
        gates = mm(hmix, w_rest, rows=m, row0=0, col0=col_g, ncols=N_BRANCH * d, wsel=li, out_dtype=BF16,
                   tn=_pick(N_BRANCH * d, 1024, 256), act="sigmoid", name="in_gates")
        zm = mm(hmix, w_head, rows=m, row0=0, col0=col_m, ncols=q_lora + kv_lora, wsel=li, out_dtype=F32,
                tn=q_lora + kv_lora, name="in_mla")
        kr = mm(hmix, w_head, rows=m, row0=0, col0=col_kr, ncols=LANES, wsel=li, out_dtype=F32, tn=LANES, name="in_krope")
        st_kr.append(kr[:m_ctx, :ROPE_B].reshape(batch, seq, ROPE_B))

        g_sub = diff_subln_g[li].reshape(1, 2 * HD_A)
        lam = lams[li].reshape(1)
        oa_ctx = _ctx_attn(b3(qa_ctx), b3(ka_ctx), b3(va_ctx), li=li, nh=H_A, dq=2 * HD_A, dv=2 * HD_A,
                           out_rows=m_ctx, diff=True, lam=lam, g=g_sub, out_scale=1.0 - lam_inits[li], name="diff_attn_ctx")
        ka_all = jnp.concatenate([ka_lat.reshape(dec_batch, dec_seq, BR_W),
                                  cache_diff_k[:, li].reshape(dec_batch, past, BR_W).astype(BF16)], axis=1)
        va_all = jnp.concatenate([va_lat.reshape(dec_batch, dec_seq, BR_W),
                                  cache_diff_v[:, li].reshape(dec_batch, past, BR_W).astype(BF16)], axis=1)
        oa_lat = _lat_attn(qa_lat.reshape(dec_batch, dec_seq, BR_W), ka_all, jnp.swapaxes(va_all, 1, 2), nh=H_A,
                           hps=1, dq=2 * HD_A, dv=2 * HD_A, tq=tq_lat, diff=True, lam=lam, g=g_sub,
                           out_scale=1.0 - lam_inits[li], v_transposed=True, name="diff_attn_lat")

        cqn, st_ckv, ckr = _mla_prep(zm, kr, mla_q_norm_g[li].reshape(1, q_lora), mla_kv_norm_g[li].reshape(1, kv_lora),
                                     tabs_b, st(st_ckv), q_lora=q_lora, kv_lora=kv_lora, m_ctx=m_ctx, seg_len=dec_seq)
        qb_ctx = mm(cqn, w_uqp, col0=0, ncols=hb2, out_dtype=BF16, scale=qs_b, name="mla_q_ctx", **ctx)
        qb_lat = mm(cqn, w_uqp, col0=0, ncols=hb2, out_dtype=BF16, scale=qs_b,
                    rope=(tabs_b, ROPE_B // 4, (False, True)), name="mla_q_lat", **lat)
        cache_ckr = jnp.concatenate([cache_mla_ckv[:, li], cache_mla_krope[:, li],
                                     jnp.zeros((dec_batch, past, LANES - ROPE_B), F32)], axis=-1).astype(BF16)
        ckr_lat = jnp.concatenate([ckr[m_ctx:].reshape(dec_batch, dec_seq, kv_lora + LANES), cache_ckr], axis=1)
        ckr_lat = ckr_lat.reshape(dec_batch * nk_lat, kv_lora + LANES)
        tm_kv = _pick(math.gcd(m_ctx, dec_batch * nk_lat), 1024)
        kv_mm = functools.partial(_matmul, row0=0, col0=0, wsel=li, out_dtype=BF16, tm=tm_kv, tn=1024)
        kb_ctx = kv_mm(ckr, wk_aug, rows=m_ctx, ncols=hb2, name="mla_k_ctx")
        vb_ctx = kv_mm(ckr, wv_aug, rows=m_ctx, ncols=H_B * V_B, name="mla_v_ctx")
        kb_lat = kv_mm(ckr_lat, wk_aug, rows=dec_batch * nk_lat, ncols=hb2, name="mla_k_lat")
        vb_lat = kv_mm(ckr_lat, wv_aug, rows=dec_batch * nk_lat, ncols=H_B * V_B, name="mla_v_lat")
        ob_ctx = _ctx_attn(qb_ctx.reshape(batch, seq, hb2), kb_ctx.reshape(batch, seq, hb2),
                           vb_ctx.reshape(batch, seq, H_B * V_B), li=li, nh=H_B, dq=2 * LANES, dv=V_B, out_rows=m_ctx,
                           name="mla_attn_ctx")
        vbt_lat = jnp.swapaxes(vb_lat.reshape(dec_batch, nk_lat, H_B * V_B), 1, 2)
        ob_lat = _lat_attn(qb_lat.reshape(dec_batch, dec_seq, hb2), kb_lat.reshape(dec_batch, nk_lat, hb2),
                           vbt_lat, nh=H_B, hps=2, dq=2 * LANES, dv=V_B,
                           tq=tq_lat, v_transposed=True, name="mla_attn_lat")

        oc_ctx = _ctx_attn(b3(qc_ctx), b3(kc_ctx), b3(vc_ctx), li=li, nh=H_C, dq=HD_C, dv=HD_C, out_rows=m_ctx,
                           name="na_attn_ctx")
        oc_lat = _na_lat_attn(qc_lat.reshape(dec_batch, dec_seq, BR_W), kvc_lat.reshape(dec_batch, dec_seq, 2 * BR_W),
                              cache_k_c, cache_v_c, na_bias, na, li=li)

        merged = _merge(((oa_ctx, oa_lat), (ob_ctx, ob_lat), (oc_ctx, oc_lat)), gates, w_branch_b, li)
        if li % 2 == 0:
            y = (merged, w_out_b)
        else:
            y = mm(merged, w_out_b, rows=m, row0=0, col0=0, ncols=d, wsel=li, out_dtype=F32,
                   tn=_pick(d, 1024, 256), name="w_out")

        j = li // 2
        if li % 2 == 0:
            x, hff = rn(x, y, mods, norm_g4, li_post=li, k_gate=2, li_pre=li, k_mod=3, want_x=True, want_h=True)
            tm_f = _pick(m, 512)
            nt = m // tm_f
            yff = _grouped_ffn(hff, ffn_w13b, ffn_w2b, j,
                               jnp.zeros((nt,), jnp.int32), jnp.arange(nt, dtype=jnp.int32),
                               jnp.full((1,), nt, jnp.int32), tm_f)
        else:
            x, hff, logits = rn(x, y, mods, norm_g4, li_post=li, k_gate=2, li_pre=li, k_mod=3, want_x=True, want_h=True,
                                router=router_p[j])
            tm_e = _pick(TOP_K * m, 512)
            row_token, dest, top_w, tile_expert, tile_src, n_used = _moe_plan(logits[:, :N_EXPERTS], tm_e)
            xs = _rows(hff, row_token)
            ys = _grouped_ffn(xs, moe_w13b, moe_w2b, j, tile_expert, tile_src, n_used, tm_e, out_dtype=BF16)
            yff = (_rows(ys, dest[:, 0]), _rows(ys, dest[:, 1]), top_w)
        if li + 1 < depth:
            x, hmix = rn(x, yff, mods, norm_g4, li_post=li, k_gate=5, li_pre=li + 1, k_mod=0, want_x=True, want_h=True)
        else:
            y_prompt, y_sample = rn(x, yff, mods, norm_g4, li_post=li, k_gate=5, li_pre=li, k_mod=0, want_x=True,
                                    want_h=False, out_split=True)

    return (y_prompt.reshape(batch, seq, d), y_sample.reshape(dec_batch, dec_seq, d),
            st_dk, st_dv, st_ckv, jnp.stack(st_kr, axis=1), st_nk, st_nv)
```

```python
import functools
import math

import numpy as np
import jax
import jax.numpy as jnp
from jax import lax
from jax.experimental import pallas as pl
from jax.experimental.pallas import tpu as pltpu

GRID_W = 64
RMS_EPS = 1e-6
ROPE_THETA = 10000.0
H_A, HD_A = 4, 128
H_B, NOPE_B, ROPE_B, V_B = 8, 128, 64, 128
H_C, HD_C = 8, 128
NA_KR_MAX, NA_KC = 8, 16
N_BRANCH = 3
N_EXPERTS = 8
TOP_K = 2
BR_W = 1024

LANES = 128
V7X_VMEM_BYTES = 64 * 1024 * 1024
VMEM_CAP_BYTES = V7X_VMEM_BYTES - 8 * 1024 * 1024
NA_ROWS_PER_BLOCK = 4
NA_HEADS_PER_STEP = 4
MASK_VALUE = -1e30
LOG2E = math.log2(math.e)

F32 = jnp.float32
BF16 = jnp.bfloat16
HIGHEST = lax.Precision.HIGHEST


def _pick(n, pref, mult=8):
    t = min(pref, n)
    t -= t % mult
    while t > mult and n % t:
        t -= mult
    assert t > 0 and n % t == 0, (n, pref, mult)
    return t


def _cparams(sems, vmem_est):
    limit = int(min(max(vmem_est, 16 * 1024 * 1024), VMEM_CAP_BYTES))
    return pltpu.CompilerParams(dimension_semantics=sems, vmem_limit_bytes=limit)


def _rms(v):
    return v * lax.rsqrt(jnp.mean(v * v, axis=-1, keepdims=True) + RMS_EPS)


def _rope128(a, cos, sa, sb, shift):
    return a * cos + pltpu.roll(a, LANES - shift, 1) * sa + pltpu.roll(a, shift, 1) * sb


def _ada_body(c_ref, w_ref, b_ref, o_ref):
    c = c_ref[...]
    s = (c * jax.nn.sigmoid(c)).astype(BF16)
    o_ref[...] = jnp.dot(s, w_ref[...].astype(BF16), preferred_element_type=F32) + b_ref[...]


def _ada_mod(cvec, ada_w, ada_b):
    depth, d, n = ada_w.shape
    r = cvec.shape[0]
    tn = _pick(n, 1024, LANES)
    return pl.pallas_call(
        _ada_body,
        grid=(depth, n // tn),
        in_specs=[pl.BlockSpec((r, d), lambda l, j: (0, 0)),
                  pl.BlockSpec((None, d, tn), lambda l, j: (l, 0, j)),
                  pl.BlockSpec((None, 1, tn), lambda l, j: (l, 0, j))],
        out_specs=pl.BlockSpec((None, r, tn), lambda l, j: (l, 0, j)),
        out_shape=jax.ShapeDtypeStruct((depth, r, n), F32),
        compiler_params=_cparams(("arbitrary", "arbitrary"), 3 * d * tn * 4 + d * tn * 2),
        name="ada_mod",
    )(cvec, ada_w, ada_b.reshape(depth, 1, n))


def _cast_shift_body(a_ref, b_ref, o_ref, *, tn):
    half = LANES // 2
    a = pltpu.roll(a_ref[...], tn - half, 1)
    b = pltpu.roll(b_ref[...], half, 1)
    lane = lax.broadcasted_iota(jnp.int32, b.shape, 1)
    o_ref[:, :tn - LANES] = a[:, :tn - LANES].astype(BF16)
    o_ref[:, tn - LANES:] = jnp.where(lane < half, a[:, tn - LANES:], b).astype(BF16)


def _cast_body(w_ref, o_ref):
    o_ref[...] = w_ref[...].astype(BF16)


def _cast_head(w, n):
    depth, k, _ = w.shape
    assert n % LANES == 0
    tk = _pick(k, 256)
    blk = pl.BlockSpec((None, tk, n), lambda l, i: (l, i, 0))
    return pl.pallas_call(
        _cast_body, grid=(depth, k // tk), in_specs=[blk], out_specs=blk,
        out_shape=jax.ShapeDtypeStruct((depth, k, n), BF16),
        compiler_params=_cparams(("arbitrary", "arbitrary"), 2 * tk * n * 6 + tk * n * 4),
        name="cast_head",
    )(w)


def _cast_shift(w, c0, n, tn):
    depth, k, _ = w.shape
    a0 = c0 - LANES // 2
    assert a0 % tn == 0 and n % tn == 0 and tn % LANES == 0
    ab, r = a0 // tn, tn // LANES
    return pl.pallas_call(
        functools.partial(_cast_shift_body, tn=tn), grid=(depth, n // tn),
        in_specs=[pl.BlockSpec((None, k, tn), lambda l, j: (l, 0, ab + j)),
                  pl.BlockSpec((None, k, LANES), lambda l, j: (l, 0, (ab + j + 1) * r))],
        out_specs=pl.BlockSpec((None, k, tn), lambda l, j: (l, 0, j)),
        out_shape=jax.ShapeDtypeStruct((depth, k, n), BF16),
        compiler_params=_cparams(("arbitrary", "arbitrary"), 2 * k * (tn + LANES) * 4 + 2 * k * tn * 2 + 2 * k * tn * 4),
        name="cast_shift",
    )(w, w)


def _resid_norm_body(*refs, n_ctx_tiles, x_split, has_y, want_x, out_split, want_h, has_router):
    it = iter(refs)
    i = pl.program_id(0)
    if x_split:
        xa_ref, xb_ref = next(it), next(it)
        x = jnp.where(i < n_ctx_tiles, xa_ref[...], xb_ref[...])
    else:
        x = next(it)[...]
    if has_y == "pair":
        ya_ref, yb_ref, w_ref = next(it), next(it), next(it)
        w = w_ref[...]
        y = w[:, 0:1] * ya_ref[...].astype(F32) + w[:, 1:2] * yb_ref[...].astype(F32)
    elif has_y == "proj":
        a_ref, w_ref = next(it), next(it)
        y = jnp.dot(a_ref[...], w_ref[...], preferred_element_type=F32)
    elif has_y:
        y = next(it)[...].astype(F32)
    if has_y:
        gpost_ref, gate_ref = next(it), next(it)
        x = x + gate_ref[...] * (_rms(y) * gpost_ref[...])
    if want_h:
        gpre_ref, sc_ref, sh_ref = next(it), next(it), next(it)
    if has_router:
        r_ref = next(it)
    if want_x and out_split:
        xa_o, xb_o = next(it), next(it)

        @pl.when(i < n_ctx_tiles)
        def _():
            xa_o[...] = x

        @pl.when(i >= n_ctx_tiles)
        def _():
            xb_o[...] = x
    elif want_x:
        next(it)[...] = x
    if want_h:
        h = _rms(x) * gpre_ref[...]
        h = h * (1.0 + sc_ref[...]) + sh_ref[...]
        next(it)[...] = h.astype(BF16)
        if has_router:
            next(it)[...] = jnp.dot(h, r_ref[...], precision=HIGHEST, preferred_element_type=F32)


def _resid_norm(x, y, mods, norm_g, *, li_post, k_gate, li_pre, k_mod, m_ctx, seg_len,
                want_x, want_h, out_split=False, router=None):
    x_split = isinstance(x, tuple)
    d = x[0].shape[1] if x_split else x.shape[1]
    m = m_ctx + x[1].shape[0] if x_split else x.shape[0]
    tm = _pick(math.gcd(m_ctx, seg_len), 256)
    nct = m_ctx // tm

    def seg(i):
        return jnp.maximum((i * tm - m_ctx) // seg_len + 1, 0)

    row = pl.BlockSpec((tm, d), lambda i: (i, 0))
    row_a = pl.BlockSpec((tm, d), lambda i: (jnp.minimum(i, nct - 1), 0))
    row_b = pl.BlockSpec((tm, d), lambda i: (jnp.maximum(i - nct, 0), 0))
    in_specs, args = ([row_a, row_b], list(x)) if x_split else ([row], [x])
    has_y, proj_bytes = False, 0
    if y is not None:
        k_post = 1 if k_gate == 2 else 3
        if isinstance(y, tuple) and len(y) == 2:
            has_y = "proj"
            a, w = y
            in_specs += [pl.BlockSpec((tm, a.shape[1]), lambda i: (i, 0)),
                         pl.BlockSpec((None, w.shape[1], d), lambda i: (li_post, 0, 0))]
            args += [a, w]
            proj_bytes = 2 * w.shape[1] * d * 2 + 2 * tm * a.shape[1] * 2
        elif isinstance(y, tuple):
            has_y = "pair"
            in_specs += [row, row, pl.BlockSpec((tm, y[2].shape[1]), lambda i: (i, 0))]
            args += list(y)
        else:
            has_y = True
            in_specs.append(row)
            args.append(y)
        in_specs += [pl.BlockSpec((None, None, 1, d), lambda i: (li_post, k_post, 0, 0)),
                     pl.BlockSpec((None, None, None, 1, d), lambda i: (li_post, seg(i), k_gate, 0, 0))]
        args += [norm_g, mods]
    if want_h:
        k_norm = 0 if k_mod == 0 else 2
        in_specs += [pl.BlockSpec((None, None, 1, d), lambda i: (li_pre, k_norm, 0, 0)),
                     pl.BlockSpec((None, None, None, 1, d), lambda i: (li_pre, seg(i), k_mod + 1, 0, 0)),
                     pl.BlockSpec((None, None, None, 1, d), lambda i: (li_pre, seg(i), k_mod, 0, 0))]
        args += [norm_g, mods, mods]
    if router is not None:
        in_specs.append(pl.BlockSpec(router.shape, lambda i: (0, 0)))
        args.append(router)
    out_specs, out_shape = [], []
    if want_x and out_split:
        out_specs += [row_a, row_b]
        out_shape += [jax.ShapeDtypeStruct((m_ctx, d), F32), jax.ShapeDtypeStruct((m - m_ctx, d), F32)]
    elif want_x:
        out_specs.append(row)
        out_shape.append(jax.ShapeDtypeStruct((m, d), F32))
    if want_h:
        out_specs.append(row)
        out_shape.append(jax.ShapeDtypeStruct((m, d), BF16))
        if router is not None:
            out_specs.append(pl.BlockSpec((tm, router.shape[1]), lambda i: (i, 0)))
            out_shape.append(jax.ShapeDtypeStruct((m, router.shape[1]), F32))
    body = functools.partial(_resid_norm_body, n_ctx_tiles=nct, x_split=x_split, has_y=has_y,
                             want_x=want_x, out_split=out_split, want_h=want_h,
                             has_router=router is not None)
    return pl.pallas_call(
        body, grid=(m // tm,), in_specs=in_specs, out_specs=out_specs, out_shape=out_shape,
        compiler_params=_cparams(("arbitrary",), 14 * tm * d * 4 + d * LANES * 8 + proj_bytes),
        name="resid_norm",
    )(*args)


def _matmul_body(*refs, tn, scale, act, rope_shift, rope_pattern, seq_split, n_heads):
    x_ref, w_ref = refs[0], refs[1]
    o_ref = refs[-1]
    acc = jnp.dot(x_ref[...], w_ref[...], preferred_element_type=F32)
    if scale is not None:
        acc = acc * scale
    if act == "sigmoid":
        acc = jax.nn.sigmoid(acc)
    if seq_split is not None:
        refs[-2][...] = acc.astype(BF16)
        dh = tn // n_heads
        for b in range(acc.shape[0] // seq_split):
            for h in range(n_heads):
                o_ref[b, :, h, :] = acc[b * seq_split:(b + 1) * seq_split, h * dh:(h + 1) * dh]
        return
    if rope_shift is None:
        o_ref[...] = acc.astype(o_ref.dtype)
        return
    cos, sa, sb = refs[2][...], refs[3][...], refs[4][...]
    for g in range(tn // LANES):
        a = acc[:, g * LANES:(g + 1) * LANES]
        if rope_pattern[g % len(rope_pattern)]:
            a = _rope128(a, cos, sa, sb, rope_shift)
        o_ref[:, g * LANES:(g + 1) * LANES] = a.astype(o_ref.dtype)


def _matmul(x, w, *, rows, row0, col0, ncols, wsel, out_dtype, tm, tn, scale=None, act=None,
            rope=None, state=None, name="matmul"):
    k = x.shape[1]
    assert row0 % tm == 0 and rows % tm == 0 and col0 % tn == 0 and ncols % tn == 0, (row0, rows, col0, ncols, tm, tn)
    rb0, cb0 = row0 // tm, col0 // tn
    in_specs = [pl.BlockSpec((tm, k), lambda j, i: (rb0 + i, 0)),
                pl.BlockSpec((None, k, tn), lambda j, i: (wsel, 0, cb0 + j))]
    args = [x, w]
    rope_shift = rope_pattern = None
    if rope is not None:
        tabs, rope_shift, rope_pattern = rope
        nper = tabs[0].shape[0] // tm
        assert tabs[0].shape[0] % tm == 0
        for t in tabs:
            in_specs.append(pl.BlockSpec((tm, LANES), lambda j, i: (i % nper, 0)))
            args.append(t)
    aliases, seq_split, n_heads = {}, None, None
    out_spec = pl.BlockSpec((tm, tn), lambda j, i: (i, j))
    out_shape = jax.ShapeDtypeStruct((rows, ncols), out_dtype)
    if state is not None:
        prev, li, depth, seq_split, n_heads = state
        assert tm % seq_split == 0 and rope is None and tn == ncols and out_dtype == BF16
        dh = ncols // n_heads
        out_spec = [out_spec, pl.BlockSpec((tm // seq_split, None, seq_split, n_heads, dh),
                                           lambda j, i: (i, li, 0, 0, 0))]
        out_shape = [out_shape, jax.ShapeDtypeStruct((rows // seq_split, depth, seq_split, n_heads, dh), F32)]
        if prev is not None:
            in_specs.append(pl.BlockSpec(memory_space=pl.ANY))
            aliases = {len(args): 1}
            args.append(prev)
    osz = jnp.dtype(out_dtype).itemsize
    est = 2 * (tm * k * 2 + k * tn * 2 + tm * tn * osz) + 3 * tm * tn * 4 + 6 * tm * LANES * 4
    body = functools.partial(_matmul_body, tn=tn, scale=scale, act=act, rope_shift=rope_shift,
                             rope_pattern=rope_pattern, seq_split=seq_split, n_heads=n_heads)
    return pl.pallas_call(
        body, grid=(ncols // tn, rows // tm), in_specs=in_specs, out_specs=out_spec, out_shape=out_shape,
        input_output_aliases=aliases,
        compiler_params=_cparams(("arbitrary", "arbitrary"), est),
        name=name,
    )(*args)


def _mla_prep_body(zm_ref, kr_ref, qg_ref, kg_ref, cos_ref, sa_ref, sb_ref, *rest, q_lora, kv_lora,
                   n_ctx_tiles, seq):
    cq_o, ckv_o, ckr_o = rest[-3:]
    z = zm_ref[...]
    cq_o[...] = (_rms(z[:, :q_lora]) * qg_ref[...]).astype(BF16)
    ckv = _rms(z[:, q_lora:]) * kg_ref[...]
    ckr_o[:, :kv_lora] = ckv.astype(BF16)
    i = pl.program_id(0)
    lane = lax.broadcasted_iota(jnp.int32, kr_ref.shape, 1)
    kr = jnp.where(lane < ROPE_B, kr_ref[...], 0.0)

    @pl.when(i < n_ctx_tiles)
    def _():
        ckv_o[...] = ckv.reshape(ckv.shape[0] // seq, seq, kv_lora)
        ckr_o[:, kv_lora:] = kr.astype(BF16)

    @pl.when(i >= n_ctx_tiles)
    def _():
        ckr_o[:, kv_lora:] = _rope128(kr, cos_ref[...], sa_ref[...], sb_ref[...], ROPE_B // 4).astype(BF16)


def _mla_prep(zm, kr, qg, kg, tabs, state, *, q_lora, kv_lora, m_ctx, seg_len):
    prev, li, depth, seq = state[:4]
    m = zm.shape[0]
    tm = _pick(math.gcd(m_ctx, seg_len), 512)
    assert tm % seq == 0
    nct, nper = m_ctx // tm, seg_len // tm
    tab_spec = pl.BlockSpec((tm, LANES), lambda i: (jnp.maximum(i - nct, 0) % nper, 0))
    in_specs = [pl.BlockSpec((tm, q_lora + kv_lora), lambda i: (i, 0)),
                pl.BlockSpec((tm, LANES), lambda i: (i, 0)),
                pl.BlockSpec((1, q_lora), lambda i: (0, 0)),
                pl.BlockSpec((1, kv_lora), lambda i: (0, 0)),
                tab_spec, tab_spec, tab_spec]
    args = [zm, kr, qg, kg, *tabs]
    aliases = {}
    if prev is not None:
        in_specs.append(pl.BlockSpec(memory_space=pl.ANY))
        aliases = {len(args): 1}
        args.append(prev)
    body = functools.partial(_mla_prep_body, q_lora=q_lora, kv_lora=kv_lora, n_ctx_tiles=nct, seq=seq)
    return pl.pallas_call(
        body, grid=(m // tm,), in_specs=in_specs,
        out_specs=[pl.BlockSpec((tm, q_lora), lambda i: (i, 0)),
                   pl.BlockSpec((tm // seq, None, seq, kv_lora), lambda i: (jnp.minimum(i, nct - 1), li, 0, 0)),
                   pl.BlockSpec((tm, kv_lora + LANES), lambda i: (i, 0))],
        out_shape=[jax.ShapeDtypeStruct((m, q_lora), BF16),
                   jax.ShapeDtypeStruct((m_ctx // seq, depth, seq, kv_lora), F32),
                   jax.ShapeDtypeStruct((m, kv_lora + LANES), BF16)],
        input_output_aliases=aliases,
        compiler_params=_cparams(("arbitrary",), 8 * tm * (q_lora + kv_lora + 4 * LANES) * 4),
        name="mla_prep",
    )(*args)


_NT = (((1,), (1,)), ((), ()))


def _softmax_parts(q, k):
    s = lax.dot_general(q, k, _NT, preferred_element_type=F32)
    p = jnp.exp2(s - jnp.max(s, axis=-1, keepdims=True))
    return p, 1.0 / jnp.sum(p, axis=-1, keepdims=True)


def _one_head(q, k, v, *, diff, lam, g, out_scale):
    k = k.astype(BF16)
    v = v.astype(BF16)
    if not diff:
        p, r = _softmax_parts(q, k)
        return jnp.dot(p.astype(BF16), v, preferred_element_type=F32) * r
    p1, r1 = _softmax_parts(q[:, :HD_A], k[:, :HD_A])
    p2, r2 = _softmax_parts(q[:, HD_A:], k[:, HD_A:])
    p = (p1 * r1 - p2 * (lam * r2)).astype(BF16)
    o = jnp.dot(p, v, preferred_element_type=F32)
    return _rms(o) * g * out_scale


def _ctx_attn_body(*refs, nb, nh, dq, dv, diff, out_scale):
    it = iter(refs)
    lam = next(it)[0] if diff else None
    q_ref, k_ref, v_ref = next(it), next(it), next(it)
    g = next(it)[...] if diff else None
    o_ref = refs[-1]
    s = q_ref.shape[1]
    for b in range(nb):
        for h in range(nh):
            o = _one_head(q_ref[b, :, h * dq:(h + 1) * dq], k_ref[b, :, h * dq:(h + 1) * dq],
                          v_ref[b, :, h * dv:(h + 1) * dv], diff=diff, lam=lam, g=g, out_scale=out_scale)
            o_ref[b * s:(b + 1) * s, h * dv:(h + 1) * dv] = o.astype(o_ref.dtype)


def _ctx_attn(q, k, v, *, li, nh, dq, dv, out_rows, diff=False, lam=None, g=None, out_scale=None, name):
    bsz, s, _ = q.shape
    nb = 2 if bsz % 2 == 0 else 1

    def kv_spec(a, width):
        if a.ndim == 4:
            return pl.BlockSpec((nb, None, s, width), lambda i: (i, li, 0, 0))
        return pl.BlockSpec((nb, s, width), lambda i: (i, 0, 0))

    in_specs, args = [], []
    if diff:
        in_specs.append(pl.BlockSpec(memory_space=pltpu.SMEM))
        args.append(lam)
    in_specs += [pl.BlockSpec((nb, s, nh * dq), lambda i: (i, 0, 0)), kv_spec(k, nh * dq), kv_spec(v, nh * dv)]
    args += [q, k, v]
    if diff:
        in_specs.append(pl.BlockSpec((1, dv), lambda i: (0, 0)))
        args.append(g)
    body = functools.partial(_ctx_attn_body, nb=nb, nh=nh, dq=dq, dv=dv, diff=diff, out_scale=out_scale)
    return pl.pallas_call(
        body, grid=(bsz // nb,), in_specs=in_specs,
        out_specs=pl.BlockSpec((nb * s, nh * dv), lambda i: (i, 0)),
        out_shape=jax.ShapeDtypeStruct((out_rows, nh * dv), BF16),
        compiler_params=_cparams(("arbitrary",), 4 * nb * s * nh * (2 * dq + dv) * 4 + 16 * s * s * 4 * nb * nh),
        name=name,
    )(*args)


def _softmax_parts_t(k, q):
    st = lax.dot_general(k, q, _NT, preferred_element_type=F32)
    p = jnp.exp2(st - jnp.max(st, axis=0, keepdims=True))
    return p.astype(BF16), 1.0 / jnp.sum(p, axis=0, keepdims=True)


def _lat_attn_t_body(*refs, hps, dq, dv, diff, out_scale):
    it = iter(refs)
    lam = next(it)[0] if diff else None
    q_ref, k_ref, vt_ref = next(it), next(it), next(it)
    g = next(it)[...] if diff else None
    o_ref = refs[-1]
    if diff:
        q, k, vt = q_ref[...], k_ref[...], vt_ref[...]
        p1, r1 = _softmax_parts_t(k[:, :HD_A], q[:, :HD_A])
        p2, r2 = _softmax_parts_t(k[:, HD_A:], q[:, HD_A:])
        ot = (jnp.dot(vt, p1, preferred_element_type=F32) * r1
              - jnp.dot(vt, p2, preferred_element_type=F32) * (lam * r2))
        o_ref[...] = (_rms(ot.T) * g * out_scale).astype(o_ref.dtype)
        return
    ps = [_softmax_parts_t(k_ref[:, h * dq:(h + 1) * dq], q_ref[:, h * dq:(h + 1) * dq]) for h in range(hps)]
    for h, (p, r) in enumerate(ps):
        ot = jnp.dot(vt_ref[h * dv:(h + 1) * dv, :], p, preferred_element_type=F32) * r
        o_ref[:, h * dv:(h + 1) * dv] = ot.T.astype(o_ref.dtype)


def _lat_attn_body(*refs, hps, dq, dv, diff, out_scale):
    it = iter(refs)
    lam = next(it)[0] if diff else None
    q_ref, k_ref, v_ref = next(it), next(it), next(it)
    g = next(it)[...] if diff else None
    o_ref = refs[-1]
    for h in range(hps):
        o = _one_head(q_ref[:, h * dq:(h + 1) * dq], k_ref[:, h * dq:(h + 1) * dq],
                      v_ref[:, h * dv:(h + 1) * dv], diff=diff, lam=lam, g=g, out_scale=out_scale)
        o_ref[:, h * dv:(h + 1) * dv] = o.astype(o_ref.dtype)


def _lat_attn(q, k, v, *, nh, hps, dq, dv, tq, diff=False, lam=None, g=None,
              out_scale=None, v_transposed=False, name):
    bsz, n, _ = q.shape
    nk = k.shape[1]
    assert nh % hps == 0 and n % tq == 0
    in_specs, args = [], []
    if diff:
        in_specs.append(pl.BlockSpec(memory_space=pltpu.SMEM))
        args.append(lam)
    v_spec = (pl.BlockSpec((None, hps * dv, nk), lambda b, h, i: (b, h, 0)) if v_transposed
              else pl.BlockSpec((None, nk, hps * dv), lambda b, h, i: (b, 0, h)))
    in_specs += [pl.BlockSpec((None, tq, hps * dq), lambda b, h, i: (b, i, h)),
                 pl.BlockSpec((None, nk, hps * dq), lambda b, h, i: (b, 0, h)), v_spec]
    args += [q, k, v]
    if diff:
        in_specs.append(pl.BlockSpec((1, dv), lambda b, h, i: (0, 0)))
        args.append(g)
    nqb = n // tq
    n_chain = hps * (2 if diff else 1)
    est = 2 * (tq * hps * dq + nk * hps * (dq + dv)) * 2 + 5 * n_chain * tq * nk * 4
    if v_transposed:
        assert hps == 1 or not diff
        body = functools.partial(_lat_attn_t_body, hps=hps, dq=dq, dv=dv, diff=diff, out_scale=out_scale)
    else:
        body = functools.partial(_lat_attn_body, hps=hps, dq=dq, dv=dv, diff=diff, out_scale=out_scale)
    return pl.pallas_call(
        body, grid=(bsz, nh // hps, nqb), in_specs=in_specs,
        out_specs=pl.BlockSpec((tq, hps * dv), lambda b, h, i: (b * nqb + i, h)),
        out_shape=jax.ShapeDtypeStruct((bsz * n, nh * dv), BF16),
        compiler_params=_cparams(("arbitrary", "arbitrary", "arbitrary"), est),
        name=name,
    )(*args)


def _na_geometry(rows):
    rpb = NA_ROWS_PER_BLOCK
    kr = min(NA_KR_MAX, rows)
    span = rpb + kr - 1
    assert rows % rpb == 0 and rows >= span, rows
    nblk = rows // rpb
    r = np.arange(rows)
    row_start = np.clip(r - kr // 2, 0, rows - kr)
    blk_start = np.clip(rpb * np.arange(nblk) - kr // 2, 0, rows - span)
    cols = np.arange(GRID_W)
    col_start = np.clip(cols - NA_KC // 2, 0, GRID_W - NA_KC)
    dc = cols[None, :] - cols[:, None] + (NA_KC - 1)
    ok_c = (cols[None, :] >= col_start[:, None]) & (cols[None, :] < col_start[:, None] + NA_KC)
    sigs, cls_of_blk, dr_l, ok_l = {}, [], [], []
    for b in range(nblk):
        qr = rpb * b + np.arange(rpb)
        key_row = blk_start[b] + np.arange(span)
        dr = key_row[None, :] - qr[:, None] + (NA_KR_MAX - 1)
        ok_r = (key_row[None, :] >= row_start[qr][:, None]) & (key_row[None, :] < row_start[qr][:, None] + kr)
        sig = (dr.tobytes(), ok_r.tobytes())
        if sig not in sigs:
            sigs[sig] = len(sigs)
            dr_l.append(np.clip(dr, 0, 2 * NA_KR_MAX - 2))
            ok = ok_r[:, None, :, None] & ok_c[None, :, None, :]
            ok_l.append(ok.reshape(rpb * GRID_W, span * GRID_W))
        cls_of_blk.append(sigs[sig])
    return dict(span=span, nblk=nblk, blk_start=blk_start.astype(np.int32), cls=np.asarray(cls_of_blk, np.int32),
                dr=np.stack(dr_l), dc=np.clip(dc, 0, 2 * NA_KC - 2), ok=np.stack(ok_l))


def _na_bias(tables, na):
    depth, nh, n_dr, n_dc = tables.shape
    ncls, rpb, span = na["dr"].shape
    oh_r = jax.nn.one_hot(na["dr"].reshape(-1), n_dr, dtype=F32)
    oh_c = jax.nn.one_hot(na["dc"].reshape(-1), n_dc, dtype=F32).T
    t = jnp.einsum("xr,lhrc->lhxc", oh_r, tables.astype(F32), precision=HIGHEST)
    t = jnp.einsum("lhxc,cy->lhxy", t, oh_c, precision=HIGHEST)
    t = t.reshape(depth, nh, ncls, rpb, span, GRID_W, GRID_W).transpose(0, 1, 2, 3, 5, 4, 6)
    t = t.reshape(depth, nh, ncls, rpb * GRID_W, span * GRID_W)
    return jnp.where(na["ok"][None, None], t * LOG2E, MASK_VALUE)


def _na_body(cls_ref, start_ref, q_ref, k_ref, v_ref, kc_ref, vc_ref, bias_ref, o_ref, *, span_tok, hps):
    del cls_ref
    blk = pl.program_id(2)
    start = pl.multiple_of(start_ref[blk] * GRID_W, GRID_W)
    scores = []
    for h in range(hps):
        c = slice(h * HD_C, (h + 1) * HD_C)
        q = q_ref[:, c]
        s_loc = lax.dot_general(q, k_ref[pl.ds(start, span_tok), c], _NT, preferred_element_type=F32) + bias_ref[h]
        s_ctx = lax.dot_general(q, kc_ref[:, c].astype(BF16), _NT, preferred_element_type=F32)
        scores.append((s_loc, s_ctx))
    for h, (s_loc, s_ctx) in enumerate(scores):
        c = slice(h * HD_C, (h + 1) * HD_C)
        mx = jnp.maximum(jnp.max(s_loc, axis=-1, keepdims=True), jnp.max(s_ctx, axis=-1, keepdims=True))
        p_loc = jnp.exp2(s_loc - mx)
        p_ctx = jnp.exp2(s_ctx - mx)
        r = 1.0 / (jnp.sum(p_loc, axis=-1, keepdims=True) + jnp.sum(p_ctx, axis=-1, keepdims=True))
        o = jnp.dot(p_loc.astype(BF16), v_ref[pl.ds(start, span_tok), c], preferred_element_type=F32)
        o = o + jnp.dot(p_ctx.astype(BF16), vc_ref[:, c].astype(BF16), preferred_element_type=F32)
        o_ref[:, c] = (o * r).astype(o_ref.dtype)


def _na_lat_attn(q, kv, cache_k, cache_v, bias, na, *, li):
    bsz, n, _ = q.shape
    past = cache_k.shape[2]
    rpb_tok, span_tok, nblk = NA_ROWS_PER_BLOCK * GRID_W, na["span"] * GRID_W, na["nblk"]
    hps = NA_HEADS_PER_STEP
    w = hps * HD_C
    ng = H_C // hps
    gs = pltpu.PrefetchScalarGridSpec(
        num_scalar_prefetch=2, grid=(bsz, ng, nblk),
        in_specs=[pl.BlockSpec((None, rpb_tok, w), lambda b, h, i, cl, st: (b, i, h)),
                  pl.BlockSpec((None, n, w), lambda b, h, i, cl, st: (b, 0, h)),
                  pl.BlockSpec((None, n, w), lambda b, h, i, cl, st: (b, 0, ng + h)),
                  pl.BlockSpec((None, None, past, w), lambda b, h, i, cl, st: (b, li, 0, h)),
                  pl.BlockSpec((None, None, past, w), lambda b, h, i, cl, st: (b, li, 0, h)),
                  pl.BlockSpec((None, hps, None, rpb_tok, span_tok), lambda b, h, i, cl, st: (li, h, cl[i], 0, 0))],
        out_specs=pl.BlockSpec((rpb_tok, w), lambda b, h, i, cl, st: (b * nblk + i, h)))
    return pl.pallas_call(
        functools.partial(_na_body, span_tok=span_tok, hps=hps),
        grid_spec=gs, out_shape=jax.ShapeDtypeStruct((bsz * n, H_C * HD_C), BF16),
        compiler_params=_cparams(("arbitrary", "arbitrary", "arbitrary"),
                                 8 * n * w * 2 + hps * 12 * rpb_tok * span_tok * 4),
        name="na_attn_lat",
    )(jnp.asarray(na["cls"]), jnp.asarray(na["blk_start"]), q, kv, kv, cache_k, cache_v, bias)


def _merge_body(*refs, n_ctx_tiles):
    br, (ga, gb, gc), (wa, wb, wc), o_ref = refs[:6], refs[6:9], refs[9:12], refs[12]
    is_ctx = pl.program_id(1) < n_ctx_tiles
    acc = None
    for s, (g_ref, w_ref) in enumerate(((ga, wa), (gb, wb), (gc, wc))):
        a = jnp.where(is_ctx, br[2 * s][...], br[2 * s + 1][...])
        t = g_ref[...].astype(F32) * jnp.dot(a, w_ref[...], preferred_element_type=F32)
        acc = t if acc is None else acc + t
    o_ref[...] = acc.astype(o_ref.dtype)


def _merge(branches, gates, wb, li):
    m_ctx, kb = branches[0][0].shape
    m = m_ctx + branches[0][1].shape[0]
    d = wb.shape[-1]
    tm, tn = _pick(math.gcd(m_ctx, m - m_ctx), 512), _pick(d, 1024, 256)
    nj, nct = d // tn, m_ctx // tm
    br_ctx = pl.BlockSpec((tm, kb), lambda j, i: (jnp.minimum(i, nct - 1), 0))
    br_lat = pl.BlockSpec((tm, kb), lambda j, i: (jnp.maximum(i - nct, 0), 0))
    gspec = [pl.BlockSpec((tm, tn), functools.partial(lambda j, i, s: (i, s * nj + j), s=s)) for s in range(N_BRANCH)]
    wspec = [pl.BlockSpec((None, None, kb, tn), functools.partial(lambda j, i, s: (li, s, 0, j), s=s)) for s in range(N_BRANCH)]
    est = 2 * (6 * tm * kb * 2 + 3 * tm * tn * 2 + 3 * kb * tn * 2 + tm * tn * 2) + 4 * tm * tn * 4
    return pl.pallas_call(
        functools.partial(_merge_body, n_ctx_tiles=nct), grid=(nj, m // tm),
        in_specs=[br_ctx, br_lat] * N_BRANCH + gspec + wspec,
        out_specs=pl.BlockSpec((tm, tn), lambda j, i: (i, j)),
        out_shape=jax.ShapeDtypeStruct((m, d), BF16),
        compiler_params=_cparams(("arbitrary", "arbitrary"), est),
        name="branch_merge",
    )(*[a for pair in branches for a in pair], gates, gates, gates, wb, wb, wb)


def _ffn1_body(te_ref, ts_ref, nu_ref, x_ref, w13_ref, o_ref):
    del te_ref, ts_ref

    @pl.when(pl.program_id(1) < nu_ref[0])
    def _():
        tf = o_ref.shape[1]
        h = jnp.dot(x_ref[...], w13_ref[...], preferred_element_type=F32)
        h1, h3 = h[:, :tf], h[:, tf:]
        o_ref[...] = (h1 * jax.nn.sigmoid(h1) * h3).astype(o_ref.dtype)

    @pl.when(pl.program_id(1) >= nu_ref[0])
    def _():
        o_ref[...] = jnp.zeros_like(o_ref)


def _ffn2_body(te_ref, ts_ref, nu_ref, h_ref, w2_ref, o_ref):
    del te_ref, ts_ref

    @pl.when(pl.program_id(1) < nu_ref[0])
    def _():
        o_ref[...] = jnp.dot(h_ref[...], w2_ref[...], preferred_element_type=F32).astype(o_ref.dtype)

    @pl.when(pl.program_id(1) >= nu_ref[0])
    def _():
        o_ref[...] = jnp.zeros_like(o_ref)


def _ffn_tile(f):
    return _pick(f, 1408, LANES)


def _cast_interleave_body(w1_ref, w3_ref, o_ref):
    tf = w1_ref.shape[1]
    o_ref[:, :tf] = w1_ref[...].astype(BF16)
    o_ref[:, tf:] = w3_ref[...].astype(BF16)


def _cast_interleave(w1, w3):
    nl, ne, d, f = w1.shape
    tf = _ffn_tile(f)
    tk = _pick(d, 1024)
    blk = pl.BlockSpec((None, None, tk, tf), lambda l, e, k, j: (l, e, k, j))
    return pl.pallas_call(
        _cast_interleave_body, grid=(nl, ne, d // tk, f // tf), in_specs=[blk, blk],
        out_specs=pl.BlockSpec((None, None, tk, 2 * tf), lambda l, e, k, j: (l, e, k, j)),
        out_shape=jax.ShapeDtypeStruct((nl, ne, d, 2 * f), BF16),
        compiler_params=_cparams(("arbitrary",) * 4, 2 * (2 * tk * tf * 4 + tk * 2 * tf * 2) + 2 * tk * tf * 4),
        name="cast_interleave",
    )(w1, w3)


def _grouped_ffn(xs, w13, w2, lsel, tile_expert, tile_src, n_used, tm, out_dtype=F32, tn_down=1024):
    r, d = xs.shape
    f = w2.shape[2]
    nt = r // tm
    tf = _ffn_tile(f)
    tn = _pick(d, tn_down, 256)
    gs1 = pltpu.PrefetchScalarGridSpec(
        num_scalar_prefetch=3, grid=(f // tf, nt),
        in_specs=[pl.BlockSpec((tm, d), lambda j, t, te, ts, nu: (ts[t], 0)),
                  pl.BlockSpec((None, None, d, 2 * tf), lambda j, t, te, ts, nu: (lsel, te[t], 0, j))],
        out_specs=pl.BlockSpec((tm, tf), lambda j, t, te, ts, nu: (ts[t], j)))
    hid = pl.pallas_call(
        _ffn1_body, grid_spec=gs1, out_shape=jax.ShapeDtypeStruct((r, f), BF16),
        compiler_params=_cparams(("arbitrary", "arbitrary"),
                                 2 * (tm * d * 2 + 2 * d * tf * 2 + tm * tf * 2) + 4 * tm * tf * 4),
        name="ffn_up",
    )(tile_expert, tile_src, n_used, xs, w13)
    gs2 = pltpu.PrefetchScalarGridSpec(
        num_scalar_prefetch=3, grid=(d // tn, nt),
        in_specs=[pl.BlockSpec((tm, f), lambda j, t, te, ts, nu: (ts[t], 0)),
                  pl.BlockSpec((None, None, f, tn), lambda j, t, te, ts, nu: (lsel, te[t], 0, j))],
        out_specs=pl.BlockSpec((tm, tn), lambda j, t, te, ts, nu: (ts[t], j)))
    return pl.pallas_call(
        _ffn2_body, grid_spec=gs2, out_shape=jax.ShapeDtypeStruct((r, d), out_dtype),
        compiler_params=_cparams(("arbitrary", "arbitrary"),
                                 2 * (tm * f * 2 + f * tn * 2 + tm * tn * 4) + 2 * tm * tn * 4),
        name="ffn_down",
    )(tile_expert, tile_src, n_used, hid, w2)


def _moe_plan(logits, tm):
    m = logits.shape[0]
    top_v, top_i = lax.top_k(logits, TOP_K)
    top_w = jax.nn.softmax(top_v, axis=-1)
    flat_e = top_i.reshape(-1)
    onehot = (flat_e[:, None] == jnp.arange(N_EXPERTS)[None, :]).astype(jnp.int32)
    csum = jnp.cumsum(onehot, axis=0)
    counts = csum[-1]
    rank = jnp.take_along_axis(csum, flat_e[:, None], axis=1)[:, 0] - 1
    padded = ((counts + tm - 1) // tm) * tm
    ends = jnp.cumsum(padded)
    offs = ends - padded
    dest = offs[flat_e] + rank
    n_rows = TOP_K * m + N_EXPERTS * tm
    nt = n_rows // tm
    row_token = jnp.zeros((n_rows,), jnp.int32).at[dest].set(jnp.arange(TOP_K * m, dtype=jnp.int32) // TOP_K)
    n_used = (ends[-1] // tm).astype(jnp.int32)
    tile_src = jnp.arange(nt, dtype=jnp.int32)
    tile_expert = jnp.sum((jnp.minimum(tile_src, n_used - 1)[:, None] * tm >= ends[None, :]).astype(jnp.int32), axis=1)
    tile_expert = jnp.minimum(tile_expert, N_EXPERTS - 1).astype(jnp.int32)
    return row_token, dest.reshape(m, TOP_K), top_w, tile_expert, tile_src, n_used.reshape(1)


def _rows(a, idx):
    return a.at[idx].get(mode="promise_in_bounds")


def _rope_tables(n_tok, dim):
    half = dim // 2
    t = jnp.arange(n_tok)
    row = (t // GRID_W).astype(F32)
    col = (t % GRID_W).astype(F32)
    inv = ROPE_THETA ** (-jnp.arange(0, half, 2, dtype=F32) / half)
    ar = row[:, None] * inv[None, :]
    ac = col[:, None] * inv[None, :]
    ang = jnp.concatenate([ar, ar, ac, ac], axis=-1)
    cos, sin = jnp.cos(ang), jnp.sin(ang)
    first = (np.arange(dim) % half) < (dim // 4)
    sa = jnp.where(first[None, :], -sin, 0.0)
    sb = jnp.where(first[None, :], 0.0, sin)
    pad = LANES - dim
    if pad:
        cos = jnp.pad(cos, ((0, 0), (0, pad)), constant_values=1.0)
        sa = jnp.pad(sa, ((0, 0), (0, pad)))
        sb = jnp.pad(sb, ((0, 0), (0, pad)))
    return cos, sa, sb


def kernel(x_prompt, x_sample, cache_diff_k, cache_diff_v, cache_mla_ckv, cache_mla_krope, cache_na_k, cache_na_v, c, c_ctx, norm_g, ada_w, ada_b, w_in, diff_lambda, diff_subln_g, mla_q_norm_g, mla_kv_norm_g, mla_w_uq, mla_w_ukv, na_rel_bias, w_branch, w_out, ffn_w1, ffn_w3, ffn_w2, moe_router, moe_w1, moe_w3, moe_w2):
    batch, seq, d = x_prompt.shape
    dec_batch, dec_seq, _ = x_sample.shape
    depth = norm_g.shape[0]
    past = cache_diff_k.shape[2]
    q_lora, kv_lora = mla_q_norm_g.shape[1], mla_kv_norm_g.shape[1]
    m_ctx, m_lat = batch * seq, dec_batch * dec_seq
    m = m_ctx + m_lat
    nk_lat = dec_seq + past
    assert dec_seq % GRID_W == 0 and kv_lora % LANES == 0 and q_lora % LANES == 0

    a3 = 3 * BR_W
    col_m = a3
    col_kr = a3 + q_lora + kv_lora
    c_na = col_kr + ROPE_B
    n_rest = 3 * BR_W + N_BRANCH * d
    w_head = _cast_head(w_in, col_kr + LANES)
    w_rest = _cast_shift(w_in, c_na, n_rest, 768)
    col_c, col_g = 0, 3 * BR_W
    wq = mla_w_uq.reshape(depth, q_lora, H_B, NOPE_B + ROPE_B)
    w_uqp = jnp.pad(wq, ((0, 0), (0, 0), (0, 0), (0, 2 * LANES - NOPE_B - ROPE_B))).reshape(depth, q_lora, H_B * 2 * LANES).astype(BF16)
    wkv = mla_w_ukv.reshape(depth, kv_lora, H_B, NOPE_B + V_B)
    wk_top = jnp.pad(wkv[..., :NOPE_B], ((0, 0), (0, 0), (0, 0), (0, 2 * LANES - NOPE_B))).reshape(depth, kv_lora, H_B * 2 * LANES)
    eye = np.zeros((LANES, H_B, 2 * LANES), np.float32)
    for r_ in range(ROPE_B):
        eye[r_, :, NOPE_B + r_] = 1.0
    wk_aug = jnp.concatenate([wk_top, jnp.broadcast_to(jnp.asarray(eye.reshape(LANES, -1)), (depth, LANES, H_B * 2 * LANES))], axis=1).astype(BF16)
    wv_aug = jnp.pad(wkv[..., NOPE_B:].reshape(depth, kv_lora, H_B * V_B), ((0, 0), (0, LANES), (0, 0))).astype(BF16)
    w_branch_b = w_branch.astype(BF16)
    w_out_b = w_out.astype(BF16)
    ffn_w13b, ffn_w2b = _cast_interleave(ffn_w1[:, None], ffn_w3[:, None]), ffn_w2.astype(BF16)[:, None]
    if depth > 1:
        moe_w13b, moe_w2b = _cast_interleave(moe_w1, moe_w3), moe_w2.astype(BF16)
    router_p = jnp.pad(moe_router, ((0, 0), (0, 0), (0, LANES - N_EXPERTS)))
    norm_g4 = norm_g.reshape(depth, 4, 1, d)

    n_seg = 1 + dec_batch
    r_pad = -(-n_seg // 8) * 8
    cvec = jnp.zeros((r_pad, d), F32).at[0].set(c_ctx).at[1:n_seg].set(c)
    mods = _ada_mod(cvec, ada_w, ada_b).reshape(depth, r_pad, 6, 1, d)

    tabs_a = _rope_tables(dec_seq, HD_A)
    tabs_b = _rope_tables(dec_seq, ROPE_B)
    na = _na_geometry(dec_seq // GRID_W)
    na_bias = _na_bias(na_rel_bias, na)
    lam_inits = [0.8 - 0.6 * math.exp(-0.3 * li) for li in range(depth)]
    lp = diff_lambda.astype(F32)
    lams = jnp.exp(jnp.sum(lp[:, 0] * lp[:, 1], axis=-1)) - jnp.exp(jnp.sum(lp[:, 2] * lp[:, 3], axis=-1)) + jnp.asarray(lam_inits, F32)

    tm_big = _pick(math.gcd(m_ctx, dec_seq), 1024)
    assert tm_big % seq == 0
    tq_lat = _pick(dec_seq, 512)
    rn = functools.partial(_resid_norm, m_ctx=m_ctx, seg_len=dec_seq)
    mm = functools.partial(_matmul, tm=tm_big)
    qs_a, qs_b, qs_c = HD_A ** -0.5 * LOG2E, (NOPE_B + ROPE_B) ** -0.5 * LOG2E, HD_C ** -0.5 * LOG2E
    hb2 = H_B * 2 * LANES
    cache_k_c = cache_na_k.reshape(dec_batch, depth, past, H_C * HD_C)
    cache_v_c = cache_na_v.reshape(dec_batch, depth, past, H_C * HD_C)

    x = (x_prompt.reshape(m_ctx, d), x_sample.reshape(m_lat, d))
    (hmix,) = rn(x, None, mods, norm_g4, li_post=0, k_gate=2, li_pre=0, k_mod=0, want_x=False, want_h=True)
    st_dk = jnp.zeros((batch, depth, seq, H_A, 2 * HD_A), F32)
    st_dv = jnp.zeros((batch, depth, seq, H_A, 2 * HD_A), F32)
    st_nk = jnp.zeros((batch, depth, seq, H_C, HD_C), F32)
    st_nv = jnp.zeros((batch, depth, seq, H_C, HD_C), F32)
    st_ckv = jnp.zeros((batch, depth, seq, kv_lora), F32)
    st_kr = []

    for li in range(depth):
        def st(prev, n_heads=None):
            return (prev, li, depth, seq, n_heads)

        def b3(a):
            return a.reshape(batch, seq, a.shape[-1])

        ctx = dict(rows=m_ctx, row0=0, wsel=li, tn=1024)
        lat = dict(rows=m_lat, row0=m_ctx, wsel=li, tn=1024)
        qa_ctx = mm(hmix, w_head, col0=0, ncols=BR_W, out_dtype=BF16, scale=qs_a, name="in_qa_ctx", **ctx)
        ka_ctx, st_dk = mm(hmix, w_head, col0=BR_W, ncols=BR_W, out_dtype=BF16, state=st(st_dk, H_A), name="in_ka_ctx", **ctx)
        va_ctx, st_dv = mm(hmix, w_head, col0=2 * BR_W, ncols=BR_W, out_dtype=BF16, state=st(st_dv, H_A), name="in_va_ctx", **ctx)
        qa_lat = mm(hmix, w_head, col0=0, ncols=BR_W, out_dtype=BF16, scale=qs_a,
                    rope=(tabs_a, HD_A // 4, (True,)), name="in_qa_lat", **lat)
        ka_lat = mm(hmix, w_head, col0=BR_W, ncols=BR_W, out_dtype=BF16,
                    rope=(tabs_a, HD_A // 4, (True,)), name="in_ka_lat", **lat)
        va_lat = mm(hmix, w_head, col0=2 * BR_W, ncols=BR_W, out_dtype=BF16, name="in_va_lat", **lat)
        qc_ctx = mm(hmix, w_rest, col0=col_c, ncols=BR_W, out_dtype=BF16, scale=qs_c, name="in_qc_ctx", **ctx)
        kc_ctx, st_nk = mm(hmix, w_rest, col0=col_c + BR_W, ncols=BR_W, out_dtype=BF16, state=st(st_nk, H_C), name="in_kc_ctx", **ctx)
        vc_ctx, st_nv = mm(hmix, w_rest, col0=col_c + 2 * BR_W, ncols=BR_W, out_dtype=BF16, state=st(st_nv, H_C), name="in_vc_ctx", **ctx)
        qc_lat = mm(hmix, w_rest, col0=col_c, ncols=BR_W, out_dtype=BF16, scale=qs_c, name="in_qc_lat", **lat)
        kvc_lat = mm(hmix, w_rest, col0=col_c + BR_W, ncols=2 * BR_W, out_dtype=BF16, name="in_kvc_lat", **lat)
        gates = mm(hmix, w_rest, rows=m, row0=0, col0=col_g, ncols=N_BRANCH * d, wsel=li, out_dtype=BF16,
                   tn=_pick(N_BRANCH * d, 1024, 256), act="sigmoid", name="in_gates")
        zm = mm(hmix, w_head, rows=m, row0=0, col0=col_m, ncols=q_lora + kv_lora, wsel=li, out_dtype=F32,
                tn=q_lora + kv_lora, name="in_mla")
        kr = mm(hmix, w_head, rows=m, row0=0, col0=col_kr, ncols=LANES, wsel=li, out_dtype=F32, tn=LANES, name="in_krope")
        st_kr.append(kr[:m_ctx, :ROPE_B].reshape(batch, seq, ROPE_B))

        g_sub = diff_subln_g[li].reshape(1, 2 * HD_A)
        lam = lams[li].reshape(1)
        oa_ctx = _ctx_attn(b3(qa_ctx), b3(ka_ctx), b3(va_ctx), li=li, nh=H_A, dq=2 * HD_A, dv=2 * HD_A,
                           out_rows=m_ctx, diff=True, lam=lam, g=g_sub, out_scale=1.0 - lam_inits[li], name="diff_attn_ctx")
        ka_all = jnp.concatenate([ka_lat.reshape(dec_batch, dec_seq, BR_W),
                                  cache_diff_k[:, li].reshape(dec_batch, past, BR_W).astype(BF16)], axis=1)
        va_all = jnp.concatenate([va_lat.reshape(dec_batch, dec_seq, BR_W),
                                  cache_diff_v[:, li].reshape(dec_batch, past, BR_W).astype(BF16)], axis=1)
        oa_lat = _lat_attn(qa_lat.reshape(dec_batch, dec_seq, BR_W), ka_all, jnp.swapaxes(va_all, 1, 2), nh=H_A,
                           hps=1, dq=2 * HD_A, dv=2 * HD_A, tq=tq_lat, diff=True, lam=lam, g=g_sub,
                           out_scale=1.0 - lam_inits[li], v_transposed=True, name="diff_attn_lat")

        cqn, st_ckv, ckr = _mla_prep(zm, kr, mla_q_norm_g[li].reshape(1, q_lora), mla_kv_norm_g[li].reshape(1, kv_lora),
                                     tabs_b, st(st_ckv), q_lora=q_lora, kv_lora=kv_lora, m_ctx=m_ctx, seg_len=dec_seq)
        qb_ctx = mm(cqn, w_uqp, col0=0, ncols=hb2, out_dtype=BF16, scale=qs_b, name="mla_q_ctx", **ctx)
        qb_lat = mm(cqn, w_uqp, col0=0, ncols=hb2, out_dtype=BF16, scale=qs_b,
                    rope=(tabs_b, ROPE_B // 4, (False, True)), name="mla_q_lat", **lat)
        cache_ckr = jnp.concatenate([cache_mla_ckv[:, li], cache_mla_krope[:, li],
                                     jnp.zeros((dec_batch, past, LANES - ROPE_B), F32)], axis=-1).astype(BF16)
        ckr_lat = jnp.concatenate([ckr[m_ctx:].reshape(dec_batch, dec_seq, kv_lora + LANES), cache_ckr], axis=1)
        ckr_lat = ckr_lat.reshape(dec_batch * nk_lat, kv_lora + LANES)
        tm_kv = _pick(math.gcd(m_ctx, dec_batch * nk_lat), 1024)
        kv_mm = functools.partial(_matmul, row0=0, col0=0, wsel=li, out_dtype=BF16, tm=tm_kv, tn=1024)
        kb_ctx = kv_mm(ckr, wk_aug, rows=m_ctx, ncols=hb2, name="mla_k_ctx")
        vb_ctx = kv_mm(ckr, wv_aug, rows=m_ctx, ncols=H_B * V_B, name="mla_v_ctx")
        kb_lat = kv_mm(ckr_lat, wk_aug, rows=dec_batch * nk_lat, ncols=hb2, name="mla_k_lat")
        vb_lat = kv_mm(ckr_lat, wv_aug, rows=dec_batch * nk_lat, ncols=H_B * V_B, name="mla_v_lat")
        ob_ctx = _ctx_attn(qb_ctx.reshape(batch, seq, hb2), kb_ctx.reshape(batch, seq, hb2),
                           vb_ctx.reshape(batch, seq, H_B * V_B), li=li, nh=H_B, dq=2 * LANES, dv=V_B, out_rows=m_ctx,
                           name="mla_attn_ctx")
        vbt_lat = jnp.swapaxes(vb_lat.reshape(dec_batch, nk_lat, H_B * V_B), 1, 2)
        ob_lat = _lat_attn(qb_lat.reshape(dec_batch, dec_seq, hb2), kb_lat.reshape(dec_batch, nk_lat, hb2),
                           vbt_lat, nh=H_B, hps=2, dq=2 * LANES, dv=V_B,
                           tq=tq_lat, v_transposed=True, name="mla_attn_lat")

        oc_ctx = _ctx_attn(b3(qc_ctx), b3(kc_ctx), b3(vc_ctx), li=li, nh=H_C, dq=HD_C, dv=HD_C, out_rows=m_ctx,
                           name="na_attn_ctx")
        oc_lat = _na_lat_attn(qc_lat.reshape(dec_batch, dec_seq, BR_W), kvc_lat.reshape(dec_batch, dec_seq, 2 * BR_W),
                              cache_k_c, cache_v_c, na_bias, na, li=li)

        merged = _merge(((oa_ctx, oa_lat), (ob_ctx, ob_lat), (oc_ctx, oc_lat)), gates, w_branch_b, li)
        if li % 2 == 0:
            y = (merged, w_out_b)
        else:
            y = mm(merged, w_out_b, rows=m, row0=0, col0=0, ncols=d, wsel=li, out_dtype=F32,
                   tn=_pick(d, 1024, 256), name="w_out")

        j = li // 2
        if li % 2 == 0:
            x, hff = rn(x, y, mods, norm_g4, li_post=li, k_gate=2, li_pre=li, k_mod=3, want_x=True, want_h=True)
            tm_f = _pick(m, 512)
            nt = m // tm_f
            yff = _grouped_ffn(hff, ffn_w13b, ffn_w2b, j,
                               jnp.zeros((nt,), jnp.int32), jnp.arange(nt, dtype=jnp.int32),
                               jnp.full((1,), nt, jnp.int32), tm_f)
        else:
            x, hff, logits = rn(x, y, mods, norm_g4, li_post=li, k_gate=2, li_pre=li, k_mod=3, want_x=True, want_h=True,
                                router=router_p[j])
            tm_e = _pick(TOP_K * m, 512)
            row_token, dest, top_w, tile_expert, tile_src, n_used = _moe_plan(logits[:, :N_EXPERTS], tm_e)
            xs = _rows(hff, row_token)
            ys = _grouped_ffn(xs, moe_w13b, moe_w2b, j, tile_expert, tile_src, n_used, tm_e, out_dtype=BF16, tn_down=2048)
            yff = (_rows(ys, dest[:, 0]), _rows(ys, dest[:, 1]), top_w)
        if li + 1 < depth:
            x, hmix = rn(x, yff, mods, norm_g4, li_post=li, k_gate=5, li_pre=li + 1, k_mod=0, want_x=True, want_h=True)
        else:
            y_prompt, y_sample = rn(x, yff, mods, norm_g4, li_post=li, k_gate=5, li_pre=li, k_mod=0, want_x=True,
                                    want_h=False, out_split=True)

    return (y_prompt.reshape(batch, seq, d), y_sample.reshape(dec_batch, dec_seq, d),
            st_dk, st_dv, st_ckv, jnp.stack(st_kr, axis=1), st_nk, st_nv)
```
